```python
import math
import jax, jax.numpy as jnp
from jax import lax
import numpy as np

D_MODEL = 1024
BATCH = 8
SEQ = 4096
DEPTH = 1

PLE_DIM = 256
MLA_HEADS = 8
QK_NOPE_DIM = 64
QK_ROPE_DIM = 32
V_HEAD_DIM = 64
Q_LORA_RANK = 384
KV_LORA_RANK = 256
ROPE_THETA = 10000.0
Q_BLOCK = 128
SSM_WIDTH = D_MODEL // 2
SSM_GROUP = 16
SSM_GROUPS = SSM_WIDTH // SSM_GROUP
SSM_STATE = 64
DT_MIN = 1e-3
DT_MAX = 1e-1
D_FF = 4 * D_MODEL
LN_EPS = 1e-5
RMS_EPS = 1e-6
DEEPNORM_ALPHA = (2.0 * DEPTH) ** 0.25
DEEPNORM_BETA = (8.0 * DEPTH) ** -0.25
IN_SPLITS = (Q_LORA_RANK, KV_LORA_RANK, QK_ROPE_DIM, SSM_WIDTH, D_MODEL, D_MODEL)
IN_WIDTH = sum(IN_SPLITS)

kernel_name = "hybrid_mla_s5_gated_deepnorm_block"


def layer_norm(x, g, b):
    xf = x.astype(jnp.float32)
    mu = jnp.mean(xf, axis=-1, keepdims=True)
    xc = xf - mu
    var = jnp.mean(xc * xc, axis=-1, keepdims=True)
    y = xc * lax.rsqrt(var + LN_EPS) * g.astype(jnp.float32) + b.astype(jnp.float32)
    return y.astype(x.dtype)


def rms_norm(x, g):
    xf = x.astype(jnp.float32)
    y = xf * lax.rsqrt(jnp.mean(xf * xf, axis=-1, keepdims=True) + RMS_EPS) * g.astype(jnp.float32)
    return y.astype(x.dtype)


def rope_tables(positions, dim):
    inv_freq = ROPE_THETA ** (-jnp.arange(0, dim, 2, dtype=jnp.float32) / dim)
    ang = positions.astype(jnp.float32)[..., None] * inv_freq
    return jnp.cos(ang), jnp.sin(ang)


def apply_rope(x, cos, sin):
    xf = x.astype(jnp.float32)
    half = xf.shape[-1] // 2
    x1, x2 = xf[..., :half], xf[..., half:]
    out = jnp.concatenate([x1 * cos - x2 * sin, x2 * cos + x1 * sin], axis=-1)
    return out.astype(x.dtype)


def split_cols(t, sizes):
    outs, start = [], 0
    for s in sizes:
        outs.append(t[..., start:start + s])
        start += s
    return outs


def mla_attention(cq_raw, ckv_raw, k_rope_raw, positions, q_norm_g, w_uq, kv_norm_g, w_ukv):
    bsz, seq, _ = cq_raw.shape
    cq = rms_norm(cq_raw, q_norm_g)
    q = (cq @ w_uq).reshape(bsz, seq, MLA_HEADS, QK_NOPE_DIM + QK_ROPE_DIM)
    q_nope, q_rope = q[..., :QK_NOPE_DIM], q[..., QK_NOPE_DIM:]
    ckv = rms_norm(ckv_raw, kv_norm_g)
    kv = (ckv @ w_ukv).reshape(bsz, seq, MLA_HEADS, QK_NOPE_DIM + V_HEAD_DIM)
    k_nope, v = kv[..., :QK_NOPE_DIM], kv[..., QK_NOPE_DIM:]
    cos, sin = rope_tables(positions, QK_ROPE_DIM)
    q_rope = apply_rope(q_rope, cos[:, :, None, :], sin[:, :, None, :])
    k_rope = apply_rope(k_rope_raw, cos, sin)
    scale = (QK_NOPE_DIM + QK_ROPE_DIM) ** -0.5
    nblk = seq // Q_BLOCK
    qn_blocks = q_nope.reshape(bsz, nblk, Q_BLOCK, MLA_HEADS, QK_NOPE_DIM).transpose(1, 0, 2, 3, 4)
    qr_blocks = q_rope.reshape(bsz, nblk, Q_BLOCK, MLA_HEADS, QK_ROPE_DIM).transpose(1, 0, 2, 3, 4)
    key_idx = jnp.arange(seq)

    def one_block(args):
        blk, qn, qr = args
        s = (jnp.einsum('bqhd,bkhd->bhqk', qn, k_nope)
             + jnp.einsum('bqhr,bkr->bhqk', qr, k_rope)).astype(jnp.float32) * scale
        q_idx = blk * Q_BLOCK + jnp.arange(Q_BLOCK)
        causal = key_idx[None, :] <= q_idx[:, None]
        s = jnp.where(causal[None, None], s, -1e30)
        probs = jax.nn.softmax(s, axis=-1).astype(v.dtype)
        return jnp.einsum('bhqk,bkhd->bqhd', probs, v)

    out = lax.map(one_block, (jnp.arange(nblk), qn_blocks, qr_blocks))
    return out.transpose(1, 0, 2, 3, 4).reshape(bsz, seq, MLA_HEADS * V_HEAD_DIM)


def s5_ssm(u, a_re, a_im, log_dt, b_re, b_im, c_re, c_im, d_skip):
    bsz, seq, _ = u.shape
    f32 = jnp.float32
    uf = u.astype(f32).reshape(bsz, seq, SSM_GROUPS, SSM_GROUP)
    dt = jnp.exp(log_dt.astype(f32))[:, None]
    lam_re = jnp.minimum(a_re.astype(f32), -1e-4)
    lam_im = a_im.astype(f32)
    mag = jnp.exp(lam_re * dt)
    ang = lam_im * dt
    abar_re = mag * jnp.cos(ang)
    abar_im = mag * jnp.sin(ang)
    den = lam_re * lam_re + lam_im * lam_im
    nr = abar_re - 1.0
    ni = abar_im
    f_re = (nr * lam_re + ni * lam_im) / den
    f_im = (ni * lam_re - nr * lam_im) / den
    br = b_re.astype(f32)
    bi = b_im.astype(f32)
    bbar_re = f_re[..., None] * br - f_im[..., None] * bi
    bbar_im = f_re[..., None] * bi + f_im[..., None] * br
    bu_re = jnp.einsum('bsgn,gpn->bsgp', uf, bbar_re)
    bu_im = jnp.einsum('bsgn,gpn->bsgp', uf, bbar_im)
    full = (bsz, seq, SSM_GROUPS, SSM_STATE)
    ar_seq = jnp.broadcast_to(abar_re[None, None], full)
    ai_seq = jnp.broadcast_to(abar_im[None, None], full)

    def combine(left, right):
        ar1, ai1, xr1, xi1 = left
        ar2, ai2, xr2, xi2 = right
        return (ar2 * ar1 - ai2 * ai1,
                ar2 * ai1 + ai2 * ar1,
                ar2 * xr1 - ai2 * xi1 + xr2,
                ar2 * xi1 + ai2 * xr1 + xi2)

    _, _, x_re, x_im = lax.associative_scan(combine, (ar_seq, ai_seq, bu_re, bu_im), axis=1)
    y = (jnp.einsum('bsgp,gnp->bsgn', x_re, c_re.astype(f32))
         - jnp.einsum('bsgp,gnp->bsgn', x_im, c_im.astype(f32)))
    y = y + d_skip.astype(f32).reshape(SSM_GROUPS, SSM_GROUP) * uf
    return y.reshape(bsz, seq, SSM_WIDTH).astype(u.dtype)


def setup_inputs(seed: int = 0) -> dict:
    key = jax.random.key(seed)
    ks = iter(jax.random.split(key, 40))
    f32 = jnp.float32
    L = DEPTH

    def w(shape, fan_in, scale=1.0):
        return jax.random.normal(next(ks), shape, f32) * (fan_in ** -0.5) * scale

    def gain(shape):
        return 1.0 + 0.02 * jax.random.normal(next(ks), shape, f32)

    def bias(shape):
        return 0.02 * jax.random.normal(next(ks), shape, f32)

    x = jax.random.normal(next(ks), (BATCH, SEQ, D_MODEL), f32)
    p = jax.random.normal(next(ks), (DEPTH, BATCH, SEQ, PLE_DIM), f32)
    start = jax.random.randint(next(ks), (BATCH, 1), 0, SEQ, dtype=jnp.int32)
    positions = start + jnp.arange(SEQ, dtype=jnp.int32)[None, :]

    n_idx = jnp.arange(SSM_STATE, dtype=f32)
    a_re = -0.5 + 0.01 * jax.random.normal(next(ks), (L, SSM_GROUPS, SSM_STATE), f32)
    a_im = math.pi * n_idx[None, None, :] + 0.01 * jax.random.normal(next(ks), (L, SSM_GROUPS, SSM_STATE), f32)
    log_dt = jax.random.uniform(next(ks), (L, SSM_GROUPS), f32, math.log(DT_MIN), math.log(DT_MAX))

    return {
        "x": x,
        "p": p,
        "positions": positions,
        "ln_in_g": gain((D_MODEL,)),
        "ln_in_b": bias((D_MODEL,)),
        "w_in": w((L, D_MODEL, IN_WIDTH), D_MODEL),
        "b_gate": bias((L, 2 * D_MODEL)),
        "q_norm_g": gain((L, Q_LORA_RANK)),
        "w_uq": w((L, Q_LORA_RANK, MLA_HEADS * (QK_NOPE_DIM + QK_ROPE_DIM)), Q_LORA_RANK),
        "kv_norm_g": gain((L, KV_LORA_RANK)),
        "w_ukv": w((L, KV_LORA_RANK, MLA_HEADS * (QK_NOPE_DIM + V_HEAD_DIM)), KV_LORA_RANK),
        "w_attn_br": w((L, MLA_HEADS * V_HEAD_DIM, D_MODEL), MLA_HEADS * V_HEAD_DIM),
        "a_re": a_re,
        "a_im": a_im,
        "log_dt": log_dt,
        "b_re": w((L, SSM_GROUPS, SSM_STATE, SSM_GROUP), 2 * SSM_GROUP),
        "b_im": w((L, SSM_GROUPS, SSM_STATE, SSM_GROUP), 2 * SSM_GROUP),
        "c_re": w((L, SSM_GROUPS, SSM_GROUP, SSM_STATE), 2 * SSM_STATE),
        "c_im": w((L, SSM_GROUPS, SSM_GROUP, SSM_STATE), 2 * SSM_STATE),
        "d_skip": jax.random.normal(next(ks), (L, SSM_WIDTH), f32),
        "w_glu": w((L, SSM_WIDTH, SSM_WIDTH), SSM_WIDTH),
        "b_glu": bias((L, SSM_WIDTH)),
        "w_ssm_br": w((L, SSM_WIDTH, D_MODEL), SSM_WIDTH),
        "w_o": w((L, D_MODEL, D_MODEL), D_MODEL, DEEPNORM_BETA),
        "ln1_g": gain((L, D_MODEL)),
        "ln1_b": bias((L, D_MODEL)),
        "w_up": w((L, D_MODEL, D_FF), D_MODEL),
        "w_down": w((L, D_FF, D_MODEL), D_FF, DEEPNORM_BETA),
        "ln2_g": gain((L, D_MODEL)),
        "ln2_b": bias((L, D_MODEL)),
        "w_ple_gate": w((L, D_MODEL, D_MODEL), D_MODEL),
        "b_ple_gate": bias((L, D_MODEL)),
        "w_ple": w((L, PLE_DIM, D_MODEL), PLE_DIM, DEEPNORM_BETA),
        "ln3_g": gain((L, D_MODEL)),
        "ln3_b": bias((L, D_MODEL)),
    }


def reference(x, p, positions, ln_in_g, ln_in_b, w_in, b_gate, q_norm_g, w_uq, kv_norm_g, w_ukv,
              w_attn_br, a_re, a_im, log_dt, b_re, b_im, c_re, c_im, d_skip, w_glu, b_glu,
              w_ssm_br, w_o, ln1_g, ln1_b, w_up, w_down, ln2_g, ln2_b, w_ple_gate, b_ple_gate,
              w_ple, ln3_g, ln3_b):
    h = layer_norm(x, ln_in_g, ln_in_b)
    for l in range(DEPTH):
        proj = h @ w_in[l]
        cq, ckv, kr, u, g_a, g_b = split_cols(proj, IN_SPLITS)
        attn = mla_attention(cq, ckv, kr, positions, q_norm_g[l], w_uq[l], kv_norm_g[l], w_ukv[l])
        branch_a = attn @ w_attn_br[l]
        y = s5_ssm(u, a_re[l], a_im[l], log_dt[l], b_re[l], b_im[l], c_re[l], c_im[l], d_skip[l])
        y = jax.nn.gelu(y)
        y = y * jax.nn.sigmoid(y @ w_glu[l] + b_glu[l])
        branch_b = y @ w_ssm_br[l]
        bg = b_gate[l]
        merged = (jax.nn.sigmoid(g_a + bg[:D_MODEL]) * branch_a
                  + jax.nn.sigmoid(g_b + bg[D_MODEL:]) * branch_b)
        h = layer_norm(DEEPNORM_ALPHA * h + merged @ w_o[l], ln1_g[l], ln1_b[l])
        ff = jnp.square(jax.nn.relu(h @ w_up[l])) @ w_down[l]
        h = layer_norm(DEEPNORM_ALPHA * h + ff, ln2_g[l], ln2_b[l])
        ple = jax.nn.sigmoid(h @ w_ple_gate[l] + b_ple_gate[l]) * (p[l] @ w_ple[l])
        h = layer_norm(DEEPNORM_ALPHA * h + ple, ln3_g[l], ln3_b[l])
    return h
```

```python
import functools
import math

import jax
import jax.numpy as jnp
from jax import lax
from jax.experimental import pallas as pl
from jax.experimental.pallas import tpu as pltpu

F32 = jnp.float32
BF16 = jnp.bfloat16

MLA_HEADS = 8
QK_NOPE_DIM = 64
QK_ROPE_DIM = 32
V_HEAD_DIM = 64
ROPE_THETA = 10000.0
SSM_GROUP = 16
LN_EPS = 1e-5
RMS_EPS = 1e-6
DEPTH = 1
DEEPNORM_ALPHA = (2.0 * DEPTH) ** 0.25

LANES = 128
MXU_DIM = 256
HEAD_PAD = LANES
SSM_SUB = MXU_DIM // SSM_GROUP
SSM_CHUNK = 64
VMEM_LIMIT = 56 * 1024 * 1024


def _const_spec(shape):
    nd = len(shape)
    return pl.BlockSpec(shape, lambda *_: (0,) * nd, pipeline_mode=pl.Buffered(1))


def _layer_norm(x, g, b):
    mu = jnp.mean(x, axis=-1, keepdims=True)
    xc = x - mu
    var = jnp.mean(xc * xc, axis=-1, keepdims=True)
    return xc * lax.rsqrt(var + LN_EPS) * g + b


def _rms_norm(x, g):
    return x * lax.rsqrt(jnp.mean(x * x, axis=-1, keepdims=True) + RMS_EPS) * g


def _dot(a, b):
    return jnp.dot(a, b, preferred_element_type=F32)


def _dot_t(a, b):
    return lax.dot_general(a, b, (((1,), (1,)), ((), ())), preferred_element_type=F32)


def _in_proj_kernel(x_ref, pos_ref, lng_ref, lnb_ref, wa_ref, bg_ref, qg_ref, wq_ref, kvg_ref,
                    wk_ref, wv_ref, invf_ref,
                    q_ref, k_ref, v_ref, u_ref, ga_ref, gb_ref, *, cols, scale):
    h = _layer_norm(x_ref[...], lng_ref[...], lnb_ref[...]).astype(BF16)

    def proj(name):
        a, b = cols[name]
        return _dot(h, wa_ref[:, a:b])

    ang = pos_ref[...].astype(F32) * invf_ref[...]
    cos = jnp.cos(ang)
    sin = jnp.sin(ang)
    cos2 = jnp.concatenate([cos, cos], axis=1)
    sin2 = jnp.concatenate([sin, sin], axis=1)

    cqn = _rms_norm(proj("cq"), qg_ref[...]).astype(BF16)
    ckvn = _rms_norm(proj("ckv"), kvg_ref[...]).astype(BF16)
    k_rope = proj("kr") * cos + proj("kr_sw") * sin
    k_rope2 = jnp.concatenate([k_rope, k_rope], axis=1)

    width = wk_ref.shape[1]
    for c in range(width // MXU_DIM):
        sl = slice(c * MXU_DIM, (c + 1) * MXU_DIM)
        q = _dot(cqn, wq_ref[:, sl])
        q_sw = _dot(cqn, wq_ref[:, width + c * MXU_DIM: width + (c + 1) * MXU_DIM])
        q_ref[:, sl] = ((q * cos2 + q_sw * sin2) * scale).astype(BF16)
        k_ref[:, sl] = (_dot(ckvn, wk_ref[:, sl]) + k_rope2).astype(BF16)
    v_ref[...] = _dot(ckvn, wv_ref[...]).astype(BF16)
    u_ref[...] = proj("u").astype(BF16)
    d = ga_ref.shape[1]
    ga_ref[...] = jax.nn.sigmoid(proj("ga") + bg_ref[:, :d]).astype(BF16)
    gb_ref[...] = jax.nn.sigmoid(proj("gb") + bg_ref[:, d:]).astype(BF16)


def _rope_swap(w):
    half = w.shape[-1] // 2
    return jnp.concatenate([-w[..., half:], w[..., :half]], axis=-1)


def _head_slot(nope, rope):
    k, h, _ = rope.shape
    parts = [nope if nope is not None else jnp.zeros((k, h, QK_NOPE_DIM), rope.dtype), rope]
    used = QK_NOPE_DIM + QK_ROPE_DIM
    parts.append(jnp.zeros((k, h, HEAD_PAD - used), rope.dtype))
    return jnp.concatenate(parts, axis=-1).reshape(k, h * HEAD_PAD)


def _in_proj(x2, pos2, ln_g, ln_b, w_in, b_gate, q_norm_g, w_uq, kv_norm_g, w_ukv, tm):
    t, d = x2.shape
    q_lora = q_norm_g.shape[0]
    kv_lora = kv_norm_g.shape[0]
    ssm_w = w_in.shape[1] - (q_lora + kv_lora + QK_ROPE_DIM + 2 * d)
    heads = MLA_HEADS

    o = 0
    w_cq = w_in[:, o:o + q_lora]; o += q_lora
    w_ckv = w_in[:, o:o + kv_lora]; o += kv_lora
    w_kr = w_in[:, o:o + QK_ROPE_DIM]; o += QK_ROPE_DIM
    w_u = w_in[:, o:o + ssm_w]; o += ssm_w
    w_ga = w_in[:, o:o + d]; o += d
    w_gb = w_in[:, o:o + d]
    kr_slot = _head_slot(None, w_kr[:, None, :])
    kr_sw_slot = _head_slot(None, _rope_swap(w_kr)[:, None, :])
    pieces = [("cq", w_cq), ("ckv", w_ckv), ("kr", kr_slot), ("kr_sw", kr_sw_slot),
              ("u", w_u), ("ga", w_ga), ("gb", w_gb)]
    cols, o = {}, 0
    for name, w in pieces:
        cols[name] = (o, o + w.shape[1])
        o += w.shape[1]
    w_a = jnp.concatenate([w for _, w in pieces], axis=1).astype(BF16)

    wq = w_uq.reshape(q_lora, heads, QK_NOPE_DIM + QK_ROPE_DIM)
    wq_nope, wq_rope = wq[..., :QK_NOPE_DIM], wq[..., QK_NOPE_DIM:]
    w_q = jnp.concatenate([_head_slot(wq_nope, wq_rope),
                           _head_slot(jnp.zeros_like(wq_nope), _rope_swap(wq_rope))], axis=1).astype(BF16)
    wkv = w_ukv.reshape(kv_lora, heads, QK_NOPE_DIM + V_HEAD_DIM)
    w_k = _head_slot(wkv[..., :QK_NOPE_DIM], jnp.zeros((kv_lora, heads, QK_ROPE_DIM), F32)).astype(BF16)
    w_v = wkv[..., QK_NOPE_DIM:].reshape(kv_lora, heads * V_HEAD_DIM).astype(BF16)

    inv_freq = ROPE_THETA ** (-jnp.arange(0, QK_ROPE_DIM, 2, dtype=F32) / QK_ROPE_DIM)
    invf = _head_slot(None, jnp.concatenate([inv_freq, inv_freq])[None, None, :])

    hw = heads * HEAD_PAD
    row = lambda w: pl.BlockSpec((tm, w), lambda i: (i, 0))
    scale = (QK_NOPE_DIM + QK_ROPE_DIM) ** -0.5
    return pl.pallas_call(
        functools.partial(_in_proj_kernel, cols=cols, scale=scale),
        grid=(t // tm,),
        in_specs=[row(d), row(1), _const_spec((1, d)), _const_spec((1, d)), _const_spec(w_a.shape),
                  _const_spec((1, 2 * d)), _const_spec((1, q_lora)), _const_spec(w_q.shape),
                  _const_spec((1, kv_lora)), _const_spec(w_k.shape), _const_spec(w_v.shape),
                  _const_spec((1, HEAD_PAD))],
        out_specs=[row(hw), row(hw), row(heads * V_HEAD_DIM), row(ssm_w), row(d), row(d)],
        out_shape=[jax.ShapeDtypeStruct((t, hw), BF16), jax.ShapeDtypeStruct((t, hw), BF16),
                   jax.ShapeDtypeStruct((t, heads * V_HEAD_DIM), BF16),
                   jax.ShapeDtypeStruct((t, ssm_w), BF16),
                   jax.ShapeDtypeStruct((t, d), BF16), jax.ShapeDtypeStruct((t, d), BF16)],
        compiler_params=pltpu.CompilerParams(dimension_semantics=("parallel",),
                                             vmem_limit_bytes=VMEM_LIMIT),
        name="in_proj",
    )(x2, pos2, ln_g[None], ln_b[None], w_a, b_gate[None], q_norm_g[None], w_q, kv_norm_g[None],
      w_k, w_v, invf)


def _attn_kernel(q_ref, k_ref, v_ref, o_ref, *, tq):
    i = pl.program_id(2)
    row = lax.broadcasted_iota(jnp.int32, (tq, tq), 0)
    col = lax.broadcasted_iota(jnp.int32, (tq, tq), 1)
    outs = []
    for hh in range(2):
        hs = slice(hh * HEAD_PAD, (hh + 1) * HEAD_PAD)
        q = q_ref[0, :, hs]

        def step(j, carry, masked):
            m, l, acc = carry
            start = pl.multiple_of(j * tq, tq)
            s = _dot_t(q, k_ref[0, pl.ds(start, tq), hs])
            if masked:
                s = jnp.where(col <= row, s, -1e30)
            m_new = jnp.maximum(m, jnp.max(s, axis=-1, keepdims=True))
            a = jnp.exp(m - m_new)
            p = jnp.exp(s - m_new)
            l = a * l + jnp.sum(p, axis=-1, keepdims=True)
            acc = a * acc + _dot(p.astype(BF16), v_ref[0, pl.ds(start, tq), :])
            return m_new, l, acc

        init = (jnp.full((tq, 1), -1e30, F32), jnp.zeros((tq, 1), F32),
                jnp.zeros((tq, 2 * V_HEAD_DIM), F32))
        carry = lax.fori_loop(0, i, lambda j, c: step(j, c, False), init)
        _, l, acc = step(i, carry, True)
        outs.append(acc / l)
    lane = lax.broadcasted_iota(jnp.int32, (tq, 2 * V_HEAD_DIM), 1)
    o_ref[0] = jnp.where(lane < V_HEAD_DIM, outs[0], outs[1]).astype(BF16)


def _attention(q, k, v, tq):
    b, s, _ = q.shape
    pairs = MLA_HEADS // 2
    return pl.pallas_call(
        functools.partial(_attn_kernel, tq=tq),
        grid=(b, pairs, s // tq),
        in_specs=[pl.BlockSpec((1, tq, 2 * HEAD_PAD), lambda bi, hp, i: (bi, i, hp)),
                  pl.BlockSpec((1, s, 2 * HEAD_PAD), lambda bi, hp, i: (bi, 0, hp)),
                  pl.BlockSpec((1, s, 2 * V_HEAD_DIM), lambda bi, hp, i: (bi, 0, hp))],
        out_specs=pl.BlockSpec((1, tq, 2 * V_HEAD_DIM), lambda bi, hp, i: (bi, i, hp)),
        out_shape=jax.ShapeDtypeStruct((b, s, MLA_HEADS * V_HEAD_DIM), BF16),
        compiler_params=pltpu.CompilerParams(
            dimension_semantics=("parallel", "parallel", "arbitrary"), vmem_limit_bytes=VMEM_LIMIT),
        name="attention",
    )(q, k, v)


def _s5_kernel(u_ref, t_ref, w_ref, vt_ref, al_ref, d_ref, y_ref, z_ref, xp_ref, *, batch, chunks):
    u = u_ref[0]
    z_ref[...] = _dot(u, w_ref[0])

    a_l = al_ref[0]
    half = a_l.shape[1] // 2
    a_same = jnp.concatenate([a_l[:, :half], a_l[:, :half]], axis=1)
    a_cross = jnp.concatenate([-a_l[:, half:], a_l[:, half:]], axis=1)

    def carry(c, x):
        rows = pl.ds(pl.multiple_of(c * batch, batch), batch)
        xp_ref[rows, :] = x
        return x * a_same + pltpu.roll(x, half, axis=1) * a_cross + z_ref[rows, :]

    lax.fori_loop(0, chunks, carry, jnp.zeros((batch, 2 * half), F32))

    xp = xp_ref[...].astype(BF16)
    tiles = t_ref.shape[1]
    for t in range(tiles):
        ts = slice(t * MXU_DIM, (t + 1) * MXU_DIM)
        acc = _dot_t(xp, vt_ref[0, ts, :])
        for s in range(t + 1):
            acc += _dot(u[:, s * MXU_DIM:(s + 1) * MXU_DIM], t_ref[0, t - s])
        y = acc + d_ref[0, :, ts] * u[:, ts].astype(F32)
        y_ref[0, :, ts] = jax.nn.gelu(y).astype(BF16)


def _s5_weights(a_re, a_im, log_dt, b_re, b_im, c_re, c_im, chunk):
    g, p = a_re.shape
    n = b_re.shape[-1]
    hi = lax.Precision.HIGHEST
    dt = jnp.exp(log_dt)[:, None]
    lam_re = jnp.minimum(a_re, -1e-4)
    lam_im = a_im
    j = jnp.arange(chunk + 1, dtype=F32)[None, :, None]
    mag = jnp.exp((lam_re * dt)[:, None, :] * j)
    ang = (lam_im * dt)[:, None, :] * j
    pr, pi = mag * jnp.cos(ang), mag * jnp.sin(ang)
    den = lam_re * lam_re + lam_im * lam_im
    nr, ni = pr[:, 1] - 1.0, pi[:, 1]
    f_re = (nr * lam_re + ni * lam_im) / den
    f_im = (ni * lam_re - nr * lam_im) / den
    bt_re = jnp.swapaxes(f_re[..., None] * b_re - f_im[..., None] * b_im, 1, 2)
    bt_im = jnp.swapaxes(f_re[..., None] * b_im + f_im[..., None] * b_re, 1, 2)
    pr4, pi4 = pr[:, :, None, :], pi[:, :, None, :]
    cr4, ci4 = c_re[:, None], c_im[:, None]
    ccat = jnp.concatenate([cr4 * pr4 - ci4 * pi4, -(cr4 * pi4 + ci4 * pr4)], axis=-1)
    bcat = jnp.concatenate([bt_re, bt_im], axis=-1)
    kern = jnp.einsum("gjnq,gmq->gjnm", ccat[:, :chunk], bcat, precision=hi)
    pw_r, pw_i = pr[:, chunk - 1::-1][:, :chunk, None, :], pi[:, chunk - 1::-1][:, :chunk, None, :]
    w_re = pw_r * bt_re[:, None] - pw_i * bt_im[:, None]
    w_im = pw_r * bt_im[:, None] + pw_i * bt_re[:, None]
    w = jnp.concatenate([w_re, w_im], axis=-1).reshape(g, chunk * n, 2 * p)
    vt = ccat[:, 1:].reshape(g, chunk * n, 2 * p)
    a_l = jnp.concatenate([pr[:, chunk], pi[:, chunk]], axis=-1)[:, None, :]
    sub = SSM_SUB
    tiles = chunk // sub
    lag = (sub * jnp.arange(tiles)[:, None, None] + jnp.arange(sub)[None, None, :]
           - jnp.arange(sub)[None, :, None])
    blocks = jnp.where((lag >= 0)[None, ..., None, None], kern[:, jnp.maximum(lag, 0)], 0.0)
    toe = blocks.transpose(0, 1, 2, 5, 3, 4).reshape(g, tiles, sub * n, sub * n)
    return toe.astype(BF16), w.astype(BF16), vt.astype(BF16), a_l


def _s5(u2, batch, a_re, a_im, log_dt, b_re, b_im, c_re, c_im, d_skip):
    t, width = u2.shape
    seq = t // batch
    n = SSM_GROUP
    g = width // n
    chunk = SSM_CHUNK
    chunks = seq // chunk
    toe, w, vt, a_l = _s5_weights(a_re, a_im, log_dt, b_re, b_im, c_re, c_im, chunk)
    dvec = jnp.tile(d_skip.reshape(g, 1, n), (1, 1, chunk))
    rows, cw = chunks * batch, chunk * n
    up = u2.reshape(batch, chunks, chunk, g, n).transpose(3, 1, 0, 2, 4).reshape(g, rows, cw)
    grp = lambda shape: pl.BlockSpec((1,) + shape, lambda i: (i,) + (0,) * len(shape))
    p2 = a_l.shape[-1]
    yp = pl.pallas_call(
        functools.partial(_s5_kernel, batch=batch, chunks=chunks),
        grid=(g,),
        in_specs=[grp((rows, cw)), grp(toe.shape[1:]), grp((cw, p2)), grp((cw, p2)), grp((1, p2)),
                  grp((1, cw))],
        out_specs=grp((rows, cw)),
        out_shape=jax.ShapeDtypeStruct((g, rows, cw), BF16),
        scratch_shapes=[pltpu.VMEM((rows, p2), F32), pltpu.VMEM((rows, p2), F32)],
        compiler_params=pltpu.CompilerParams(dimension_semantics=("parallel",),
                                             vmem_limit_bytes=VMEM_LIMIT),
        name="s5",
    )(up, toe, w, vt, a_l, dvec)
    return yp.reshape(g, chunks, batch, chunk, n).transpose(2, 1, 3, 0, 4).reshape(t, width)


def _merge_kernel(x_ref, attn_ref, y_ref, ga_ref, gb_ref, lng_ref, lnb_ref, wglu_ref, bglu_ref,
                  wsb_ref, wab_ref, wo_ref, g1_ref, b1_ref, o_ref):
    h = _layer_norm(x_ref[...], lng_ref[...], lnb_ref[...])
    y = y_ref[...]
    yg = (y.astype(F32) * jax.nn.sigmoid(_dot(y, wglu_ref[...]) + bglu_ref[...])).astype(BF16)
    branch_a = _dot(attn_ref[...], wab_ref[...])
    branch_b = _dot(yg, wsb_ref[...])
    merged = ga_ref[...].astype(F32) * branch_a + gb_ref[...].astype(F32) * branch_b
    r = DEEPNORM_ALPHA * h + _dot(merged.astype(BF16), wo_ref[...])
    o_ref[...] = _layer_norm(r, g1_ref[...], b1_ref[...])


def _merge(x2, attn, y, ga, gb, ln_g, ln_b, w_glu, b_glu, w_ssm_br, w_attn_br, w_o, ln1_g, ln1_b, tm):
    t, d = x2.shape
    row = lambda w: pl.BlockSpec((tm, w), lambda i: (i, 0))
    vec = lambda v: v[None]
    ws = [w_glu.astype(BF16), vec(b_glu), w_ssm_br.astype(BF16), w_attn_br.astype(BF16),
          w_o.astype(BF16), vec(ln1_g), vec(ln1_b)]
    return pl.pallas_call(
        _merge_kernel,
        grid=(t // tm,),
        in_specs=[row(d), row(attn.shape[1]), row(y.shape[1]), row(d), row(d),
                  _const_spec((1, d)), _const_spec((1, d))] + [_const_spec(w.shape) for w in ws],
        out_specs=row(d),
        out_shape=jax.ShapeDtypeStruct((t, d), F32),
        compiler_params=pltpu.CompilerParams(dimension_semantics=("parallel",),
                                             vmem_limit_bytes=VMEM_LIMIT),
        name="merge",
    )(x2, attn, y, ga, gb, vec(ln_g), vec(ln_b), *ws)


def _ffn_kernel(h_ref, p_ref, wup_ref, wdn_ref, g2_ref, b2_ref, wpg_ref, bpg_ref, wple_ref,
                g3_ref, b3_ref, o_ref, *, ff_chunk):
    h = h_ref[...]
    hb = h.astype(BF16)
    ff = jnp.zeros_like(h)
    for c in range(wup_ref.shape[1] // ff_chunk):
        cs = slice(c * ff_chunk, (c + 1) * ff_chunk)
        up = jnp.maximum(_dot(hb, wup_ref[:, cs]), 0.0)
        ff += _dot((up * up).astype(BF16), wdn_ref[cs, :])
    h2 = _layer_norm(DEEPNORM_ALPHA * h + ff, g2_ref[...], b2_ref[...])
    gate = jax.nn.sigmoid(_dot(h2.astype(BF16), wpg_ref[...]) + bpg_ref[...])
    ple = gate * _dot(p_ref[...].astype(BF16), wple_ref[...])
    o_ref[...] = _layer_norm(DEEPNORM_ALPHA * h2 + ple, g3_ref[...], b3_ref[...])


def _ffn(h1, p2, w_up, w_down, ln2_g, ln2_b, w_pg, b_pg, w_ple, ln3_g, ln3_b, tm):
    t, d = h1.shape
    row = lambda w: pl.BlockSpec((tm, w), lambda i: (i, 0))
    vec = lambda v: v[None]
    ws = [w_up.astype(BF16), w_down.astype(BF16), vec(ln2_g), vec(ln2_b), w_pg.astype(BF16), vec(b_pg),
          w_ple.astype(BF16), vec(ln3_g), vec(ln3_b)]
    return pl.pallas_call(
        functools.partial(_ffn_kernel, ff_chunk=1024),
        grid=(t // tm,),
        in_specs=[row(d), row(p2.shape[1])] + [_const_spec(w.shape) for w in ws],
        out_specs=row(d),
        out_shape=jax.ShapeDtypeStruct((t, d), F32),
        compiler_params=pltpu.CompilerParams(dimension_semantics=("parallel",),
                                             vmem_limit_bytes=VMEM_LIMIT),
        name="ffn",
    )(h1, p2, *ws)


def kernel(x, p, positions, ln_in_g, ln_in_b, w_in, b_gate, q_norm_g, w_uq, kv_norm_g, w_ukv, w_attn_br, a_re, a_im, log_dt, b_re, b_im, c_re, c_im, d_skip, w_glu, b_glu, w_ssm_br, w_o, ln1_g, ln1_b, w_up, w_down, ln2_g, ln2_b, w_ple_gate, b_ple_gate, w_ple, ln3_g, ln3_b):
    assert w_in.shape[0] == DEPTH
    b, s, d = x.shape
    t = b * s
    tm = min(512, s)
    tq = min(256, s)
    x2 = x.reshape(t, d)
    pos2 = positions.reshape(t, 1)

    q, k, v, u, ga, gb = _in_proj(x2, pos2, ln_in_g, ln_in_b, w_in[0], b_gate[0], q_norm_g[0], w_uq[0],
                                  kv_norm_g[0], w_ukv[0], tm)
    attn = _attention(q.reshape(b, s, -1), k.reshape(b, s, -1), v.reshape(b, s, -1), tq).reshape(t, -1)
    y = _s5(u, b, a_re[0], a_im[0], log_dt[0], b_re[0], b_im[0], c_re[0], c_im[0], d_skip[0])
    h1 = _merge(x2, attn, y, ga, gb, ln_in_g, ln_in_b, w_glu[0], b_glu[0], w_ssm_br[0], w_attn_br[0],
                w_o[0], ln1_g[0], ln1_b[0], tm)
    out = _ffn(h1, p[0].reshape(t, -1), w_up[0], w_down[0], ln2_g[0], ln2_b[0], w_ple_gate[0],
               b_ple_gate[0], w_ple[0], ln3_g[0], ln3_b[0], tm)
    return out.reshape(b, s, d)
```

```python
import functools
import math

import jax
import jax.numpy as jnp
from jax import lax
from jax.experimental import pallas as pl
from jax.experimental.pallas import tpu as pltpu

F32 = jnp.float32
BF16 = jnp.bfloat16

MLA_HEADS = 8
QK_NOPE_DIM = 64
QK_ROPE_DIM = 32
V_HEAD_DIM = 64
ROPE_THETA = 10000.0
SSM_GROUP = 16
LN_EPS = 1e-5
RMS_EPS = 1e-6
DEPTH = 1
DEEPNORM_ALPHA = (2.0 * DEPTH) ** 0.25

LANES = 128
MXU_DIM = 256
HEAD_PAD = LANES
SSM_SUB = MXU_DIM // SSM_GROUP
SSM_CHUNK = 64
VMEM_LIMIT = 56 * 1024 * 1024


def _const_spec(shape):
    nd = len(shape)
    return pl.BlockSpec(shape, lambda *_: (0,) * nd, pipeline_mode=pl.Buffered(1))


def _layer_norm(x, g, b):
    mu = jnp.mean(x, axis=-1, keepdims=True)
    xc = x - mu
    var = jnp.mean(xc * xc, axis=-1, keepdims=True)
    return xc * lax.rsqrt(var + LN_EPS) * g + b


def _rms_norm(x, g):
    return x * lax.rsqrt(jnp.mean(x * x, axis=-1, keepdims=True) + RMS_EPS) * g


def _dot(a, b):
    return jnp.dot(a, b, preferred_element_type=F32)


def _dot_t(a, b):
    return lax.dot_general(a, b, (((1,), (1,)), ((), ())), preferred_element_type=F32)


def _in_proj_kernel(x_ref, pos_ref, lng_ref, lnb_ref, wa_ref, bg_ref, qg_ref, wq_ref, kvg_ref,
                    wk_ref, wv_ref, invf_ref,
                    q_ref, k_ref, v_ref, u_ref, ga_ref, gb_ref, *, cols, scale):
    h = _layer_norm(x_ref[...], lng_ref[...], lnb_ref[...]).astype(BF16)

    def proj(name):
        a, b = cols[name]
        return _dot(h, wa_ref[:, a:b])

    ang = pos_ref[...].astype(F32) * invf_ref[...]
    cos = jnp.cos(ang)
    sin = jnp.sin(ang)
    cos2 = jnp.concatenate([cos, cos], axis=1)
    sin2 = jnp.concatenate([sin, sin], axis=1)

    cqn = _rms_norm(proj("cq"), qg_ref[...]).astype(BF16)
    ckvn = _rms_norm(proj("ckv"), kvg_ref[...]).astype(BF16)
    k_rope = proj("kr") * cos + proj("kr_sw") * sin
    k_rope2 = jnp.concatenate([k_rope, k_rope], axis=1)

    width = wk_ref.shape[1]
    for c in range(width // MXU_DIM):
        sl = slice(c * MXU_DIM, (c + 1) * MXU_DIM)
        q = _dot(cqn, wq_ref[:, sl])
        q_sw = _dot(cqn, wq_ref[:, width + c * MXU_DIM: width + (c + 1) * MXU_DIM])
        q_ref[:, sl] = ((q * cos2 + q_sw * sin2) * scale).astype(BF16)
        k_ref[:, sl] = (_dot(ckvn, wk_ref[:, sl]) + k_rope2).astype(BF16)
    v_t = _dot_t(wv_ref[...], ckvn).astype(BF16)
    tk = v_ref.shape[-1]
    for c in range(v_ref.shape[1]):
        v_ref[0, c] = v_t[:, c * tk:(c + 1) * tk]
    u_ref[...] = proj("u").astype(BF16)
    d = ga_ref.shape[1]
    ga_ref[...] = jax.nn.sigmoid(proj("ga") + bg_ref[:, :d]).astype(BF16)
    gb_ref[...] = jax.nn.sigmoid(proj("gb") + bg_ref[:, d:]).astype(BF16)


def _rope_swap(w):
    half = w.shape[-1] // 2
    return jnp.concatenate([-w[..., half:], w[..., :half]], axis=-1)


def _head_slot(nope, rope):
    k, h, _ = rope.shape
    parts = [nope if nope is not None else jnp.zeros((k, h, QK_NOPE_DIM), rope.dtype), rope]
    used = QK_NOPE_DIM + QK_ROPE_DIM
    parts.append(jnp.zeros((k, h, HEAD_PAD - used), rope.dtype))
    return jnp.concatenate(parts, axis=-1).reshape(k, h * HEAD_PAD)


def _in_proj(x2, pos2, ln_g, ln_b, w_in, b_gate, q_norm_g, w_uq, kv_norm_g, w_ukv, seq, tm, tk):
    t, d = x2.shape
    tiles_per_seq = seq // tm
    q_lora = q_norm_g.shape[0]
    kv_lora = kv_norm_g.shape[0]
    ssm_w = w_in.shape[1] - (q_lora + kv_lora + QK_ROPE_DIM + 2 * d)
    heads = MLA_HEADS

    o = 0
    w_cq = w_in[:, o:o + q_lora]; o += q_lora
    w_ckv = w_in[:, o:o + kv_lora]; o += kv_lora
    w_kr = w_in[:, o:o + QK_ROPE_DIM]; o += QK_ROPE_DIM
    w_u = w_in[:, o:o + ssm_w]; o += ssm_w
    w_ga = w_in[:, o:o + d]; o += d
    w_gb = w_in[:, o:o + d]
    kr_slot = _head_slot(None, w_kr[:, None, :])
    kr_sw_slot = _head_slot(None, _rope_swap(w_kr)[:, None, :])
    pieces = [("cq", w_cq), ("ckv", w_ckv), ("kr", kr_slot), ("kr_sw", kr_sw_slot),
              ("u", w_u), ("ga", w_ga), ("gb", w_gb)]
    cols, o = {}, 0
    for name, w in pieces:
        cols[name] = (o, o + w.shape[1])
        o += w.shape[1]
    w_a = jnp.concatenate([w for _, w in pieces], axis=1).astype(BF16)

    wq = w_uq.reshape(q_lora, heads, QK_NOPE_DIM + QK_ROPE_DIM)
    wq_nope, wq_rope = wq[..., :QK_NOPE_DIM], wq[..., QK_NOPE_DIM:]
    w_q = jnp.concatenate([_head_slot(wq_nope, wq_rope),
                           _head_slot(jnp.zeros_like(wq_nope), _rope_swap(wq_rope))], axis=1).astype(BF16)
    wkv = w_ukv.reshape(kv_lora, heads, QK_NOPE_DIM + V_HEAD_DIM)
    w_k = _head_slot(wkv[..., :QK_NOPE_DIM], jnp.zeros((kv_lora, heads, QK_ROPE_DIM), F32)).astype(BF16)
    w_v = wkv[..., QK_NOPE_DIM:].reshape(kv_lora, heads * V_HEAD_DIM).T.astype(BF16)

    inv_freq = ROPE_THETA ** (-jnp.arange(0, QK_ROPE_DIM, 2, dtype=F32) / QK_ROPE_DIM)
    invf = _head_slot(None, jnp.concatenate([inv_freq, inv_freq])[None, None, :])

    hw = heads * HEAD_PAD
    hv = heads * V_HEAD_DIM
    row = lambda w: pl.BlockSpec((tm, w), lambda i: (i, 0))
    v_spec = pl.BlockSpec((1, tm // tk, hv, tk), lambda i: (i // tiles_per_seq, i % tiles_per_seq, 0, 0))
    scale = (QK_NOPE_DIM + QK_ROPE_DIM) ** -0.5 * math.log2(math.e)
    return pl.pallas_call(
        functools.partial(_in_proj_kernel, cols=cols, scale=scale),
        grid=(t // tm,),
        in_specs=[row(d), row(1), _const_spec((1, d)), _const_spec((1, d)), _const_spec(w_a.shape),
                  _const_spec((1, 2 * d)), _const_spec((1, q_lora)), _const_spec(w_q.shape),
                  _const_spec((1, kv_lora)), _const_spec(w_k.shape), _const_spec(w_v.shape),
                  _const_spec((1, HEAD_PAD))],
        out_specs=[row(hw), row(hw), v_spec, row(ssm_w), row(d), row(d)],
        out_shape=[jax.ShapeDtypeStruct((t, hw), BF16), jax.ShapeDtypeStruct((t, hw), BF16),
                   jax.ShapeDtypeStruct((t // seq, seq // tk, hv, tk), BF16),
                   jax.ShapeDtypeStruct((t, ssm_w), BF16),
                   jax.ShapeDtypeStruct((t, d), BF16), jax.ShapeDtypeStruct((t, d), BF16)],
        compiler_params=pltpu.CompilerParams(dimension_semantics=("parallel",),
                                             vmem_limit_bytes=VMEM_LIMIT),
        name="in_proj",
    )(x2, pos2, ln_g[None], ln_b[None], w_a, b_gate[None], q_norm_g[None], w_q, kv_norm_g[None],
      w_k, w_v, invf)


def _attn_kernel(q_ref, k_ref, v_ref, o_ref, *, tq):
    i = pl.program_id(2)
    key = lax.broadcasted_iota(jnp.int32, (tq, tq), 0)
    qry = lax.broadcasted_iota(jnp.int32, (tq, tq), 1)
    heads = range(2)
    qs = [q_ref[0, :, hh * HEAD_PAD:(hh + 1) * HEAD_PAD] for hh in heads]

    def scores(j):
        start = pl.multiple_of(j * tq, tq)
        return tuple(_dot_t(k_ref[0, pl.ds(start, tq), hh * HEAD_PAD:(hh + 1) * HEAD_PAD], qs[hh])
                     for hh in heads)

    def update(j, s_both, state, masked):
        ps, stats = [], []
        for hh in heads:
            m, l, _ = state[hh]
            s = s_both[hh]
            if masked:
                s = jnp.where(key <= qry, s, -1e30)
            m_new = jnp.maximum(m, jnp.max(s, axis=0, keepdims=True))
            a = jnp.exp2(m - m_new)
            p = jnp.exp2(s - m_new)
            stats.append((m_new, a, a * l + jnp.sum(p, axis=0, keepdims=True)))
            ps.append(p.astype(BF16))
        out = []
        for hh in heads:
            m_new, a, l_new = stats[hh]
            v_t = v_ref[0, j, hh * V_HEAD_DIM:(hh + 1) * V_HEAD_DIM, :]
            out.append((m_new, l_new, a * state[hh][2] + _dot(v_t, ps[hh])))
        return tuple(out)

    def body(j, carry):
        s_cur, state = carry
        s_next = scores(j + 1)
        return s_next, update(j, s_cur, state, False)

    init = tuple((jnp.full((1, tq), -1e30, F32), jnp.zeros((1, tq), F32),
                  jnp.zeros((V_HEAD_DIM, tq), F32)) for _ in heads)
    s_last, state = lax.fori_loop(0, i, body, (scores(0), init))
    state = update(i, s_last, state, True)
    o_t = jnp.concatenate([acc / l for _, l, acc in state], axis=0)
    o_ref[0] = o_t.T.astype(BF16)


def _attention(q, k, v_t, tq):
    b, s, _ = q.shape
    pairs = MLA_HEADS // 2
    nkv, _, tk = v_t.shape[1:]
    assert tk == tq
    return pl.pallas_call(
        functools.partial(_attn_kernel, tq=tq),
        grid=(b, pairs, s // tq),
        in_specs=[pl.BlockSpec((1, tq, 2 * HEAD_PAD), lambda bi, hp, i: (bi, i, hp)),
                  pl.BlockSpec((1, s, 2 * HEAD_PAD), lambda bi, hp, i: (bi, 0, hp)),
                  pl.BlockSpec((1, nkv, 2 * V_HEAD_DIM, tk), lambda bi, hp, i: (bi, 0, hp, 0))],
        out_specs=pl.BlockSpec((1, tq, 2 * V_HEAD_DIM), lambda bi, hp, i: (bi, i, hp)),
        out_shape=jax.ShapeDtypeStruct((b, s, MLA_HEADS * V_HEAD_DIM), BF16),
        compiler_params=pltpu.CompilerParams(
            dimension_semantics=("parallel", "parallel", "arbitrary"), vmem_limit_bytes=VMEM_LIMIT),
        name="attention",
    )(q, k, v_t)


def _s5_kernel(u_ref, t_ref, w_ref, vt_ref, al_ref, d_ref, y_ref, z_ref, xp_ref, *, batch, chunks):
    u = u_ref[0]
    z_ref[...] = _dot(u, w_ref[0])

    a_l = al_ref[0]
    half = a_l.shape[1] // 2
    a_same = jnp.concatenate([a_l[:, :half], a_l[:, :half]], axis=1)
    a_cross = jnp.concatenate([-a_l[:, half:], a_l[:, half:]], axis=1)

    def carry(c, x):
        rows = pl.ds(pl.multiple_of(c * batch, batch), batch)
        xp_ref[rows, :] = x
        return x * a_same + pltpu.roll(x, half, axis=1) * a_cross + z_ref[rows, :]

    lax.fori_loop(0, chunks, carry, jnp.zeros((batch, 2 * half), F32))

    xp = xp_ref[...].astype(BF16)
    tiles = t_ref.shape[1]
    for t in range(tiles):
        ts = slice(t * MXU_DIM, (t + 1) * MXU_DIM)
        acc = _dot_t(xp, vt_ref[0, ts, :])
        for s in range(t + 1):
            acc += _dot(u[:, s * MXU_DIM:(s + 1) * MXU_DIM], t_ref[0, t - s])
        y = acc + d_ref[0, :, ts] * u[:, ts].astype(F32)
        y_ref[0, :, ts] = jax.nn.gelu(y).astype(BF16)


def _s5_weights(a_re, a_im, log_dt, b_re, b_im, c_re, c_im, chunk):
    g, p = a_re.shape
    n = b_re.shape[-1]
    hi = lax.Precision.HIGHEST
    dt = jnp.exp(log_dt)[:, None]
    lam_re = jnp.minimum(a_re, -1e-4)
    lam_im = a_im
    j = jnp.arange(chunk + 1, dtype=F32)[None, :, None]
    mag = jnp.exp((lam_re * dt)[:, None, :] * j)
    ang = (lam_im * dt)[:, None, :] * j
    pr, pi = mag * jnp.cos(ang), mag * jnp.sin(ang)
    den = lam_re * lam_re + lam_im * lam_im
    nr, ni = pr[:, 1] - 1.0, pi[:, 1]
    f_re = (nr * lam_re + ni * lam_im) / den
    f_im = (ni * lam_re - nr * lam_im) / den
    bt_re = jnp.swapaxes(f_re[..., None] * b_re - f_im[..., None] * b_im, 1, 2)
    bt_im = jnp.swapaxes(f_re[..., None] * b_im + f_im[..., None] * b_re, 1, 2)
    pr4, pi4 = pr[:, :, None, :], pi[:, :, None, :]
    cr4, ci4 = c_re[:, None], c_im[:, None]
    ccat = jnp.concatenate([cr4 * pr4 - ci4 * pi4, -(cr4 * pi4 + ci4 * pr4)], axis=-1)
    bcat = jnp.concatenate([bt_re, bt_im], axis=-1)
    kern = jnp.einsum("gjnq,gmq->gjnm", ccat[:, :chunk], bcat, precision=hi)
    pw_r, pw_i = pr[:, chunk - 1::-1][:, :chunk, None, :], pi[:, chunk - 1::-1][:, :chunk, None, :]
    w_re = pw_r * bt_re[:, None] - pw_i * bt_im[:, None]
    w_im = pw_r * bt_im[:, None] + pw_i * bt_re[:, None]
    w = jnp.concatenate([w_re, w_im], axis=-1).reshape(g, chunk * n, 2 * p)
    vt = ccat[:, 1:].reshape(g, chunk * n, 2 * p)
    a_l = jnp.concatenate([pr[:, chunk], pi[:, chunk]], axis=-1)[:, None, :]
    sub = SSM_SUB
    tiles = chunk // sub
    lag = (sub * jnp.arange(tiles)[:, None, None] + jnp.arange(sub)[None, None, :]
           - jnp.arange(sub)[None, :, None])
    blocks = jnp.where((lag >= 0)[None, ..., None, None], kern[:, jnp.maximum(lag, 0)], 0.0)
    toe = blocks.transpose(0, 1, 2, 5, 3, 4).reshape(g, tiles, sub * n, sub * n)
    return toe.astype(BF16), w.astype(BF16), vt.astype(BF16), a_l


def _s5(u2, batch, a_re, a_im, log_dt, b_re, b_im, c_re, c_im, d_skip):
    t, width = u2.shape
    seq = t // batch
    n = SSM_GROUP
    g = width // n
    chunk = SSM_CHUNK
    chunks = seq // chunk
    toe, w, vt, a_l = _s5_weights(a_re, a_im, log_dt, b_re, b_im, c_re, c_im, chunk)
    dvec = jnp.tile(d_skip.reshape(g, 1, n), (1, 1, chunk))
    rows, cw = chunks * batch, chunk * n
    up = u2.reshape(batch, chunks, chunk, g, n).transpose(3, 1, 0, 2, 4).reshape(g, rows, cw)
    grp = lambda shape: pl.BlockSpec((1,) + shape, lambda i: (i,) + (0,) * len(shape))
    p2 = a_l.shape[-1]
    yp = pl.pallas_call(
        functools.partial(_s5_kernel, batch=batch, chunks=chunks),
        grid=(g,),
        in_specs=[grp((rows, cw)), grp(toe.shape[1:]), grp((cw, p2)), grp((cw, p2)), grp((1, p2)),
                  grp((1, cw))],
        out_specs=grp((rows, cw)),
        out_shape=jax.ShapeDtypeStruct((g, rows, cw), BF16),
        scratch_shapes=[pltpu.VMEM((rows, p2), F32), pltpu.VMEM((rows, p2), F32)],
        compiler_params=pltpu.CompilerParams(dimension_semantics=("parallel",),
                                             vmem_limit_bytes=VMEM_LIMIT),
        name="s5",
    )(up, toe, w, vt, a_l, dvec)
    return yp.reshape(g, chunks, batch, chunk, n).transpose(2, 1, 3, 0, 4).reshape(t, width)


def _merge_kernel(x_ref, attn_ref, y_ref, ga_ref, gb_ref, lng_ref, lnb_ref, wglu_ref, bglu_ref,
                  wsb_ref, wab_ref, wo_ref, g1_ref, b1_ref, o_ref):
    h = _layer_norm(x_ref[...], lng_ref[...], lnb_ref[...])
    y = y_ref[...]
    yg = (y.astype(F32) * jax.nn.sigmoid(_dot(y, wglu_ref[...]) + bglu_ref[...])).astype(BF16)
    branch_a = _dot(attn_ref[...], wab_ref[...])
    branch_b = _dot(yg, wsb_ref[...])
    merged = ga_ref[...].astype(F32) * branch_a + gb_ref[...].astype(F32) * branch_b
    r = DEEPNORM_ALPHA * h + _dot(merged.astype(BF16), wo_ref[...])
    o_ref[...] = _layer_norm(r, g1_ref[...], b1_ref[...])


def _merge(x2, attn, y, ga, gb, ln_g, ln_b, w_glu, b_glu, w_ssm_br, w_attn_br, w_o, ln1_g, ln1_b, tm):
    t, d = x2.shape
    row = lambda w: pl.BlockSpec((tm, w), lambda i: (i, 0))
    vec = lambda v: v[None]
    ws = [w_glu.astype(BF16), vec(b_glu), w_ssm_br.astype(BF16), w_attn_br.astype(BF16),
          w_o.astype(BF16), vec(ln1_g), vec(ln1_b)]
    return pl.pallas_call(
        _merge_kernel,
        grid=(t // tm,),
        in_specs=[row(d), row(attn.shape[1]), row(y.shape[1]), row(d), row(d),
                  _const_spec((1, d)), _const_spec((1, d))] + [_const_spec(w.shape) for w in ws],
        out_specs=row(d),
        out_shape=jax.ShapeDtypeStruct((t, d), F32),
        compiler_params=pltpu.CompilerParams(dimension_semantics=("parallel",),
                                             vmem_limit_bytes=VMEM_LIMIT),
        name="merge",
    )(x2, attn, y, ga, gb, vec(ln_g), vec(ln_b), *ws)


def _ffn_kernel(h_ref, p_ref, wup_ref, wdn_ref, g2_ref, b2_ref, wpg_ref, bpg_ref, wple_ref,
                g3_ref, b3_ref, o_ref, *, ff_chunk):
    h = h_ref[...]
    hb = h.astype(BF16)
    ff = jnp.zeros_like(h)
    for c in range(wup_ref.shape[1] // ff_chunk):
        cs = slice(c * ff_chunk, (c + 1) * ff_chunk)
        up = jnp.maximum(_dot(hb, wup_ref[:, cs]), 0.0)
        ff += _dot((up * up).astype(BF16), wdn_ref[cs, :])
    h2 = _layer_norm(DEEPNORM_ALPHA * h + ff, g2_ref[...], b2_ref[...])
    gate = jax.nn.sigmoid(_dot(h2.astype(BF16), wpg_ref[...]) + bpg_ref[...])
    ple = gate * _dot(p_ref[...].astype(BF16), wple_ref[...])
    o_ref[...] = _layer_norm(DEEPNORM_ALPHA * h2 + ple, g3_ref[...], b3_ref[...])


def _ffn(h1, p2, w_up, w_down, ln2_g, ln2_b, w_pg, b_pg, w_ple, ln3_g, ln3_b, tm):
    t, d = h1.shape
    row = lambda w: pl.BlockSpec((tm, w), lambda i: (i, 0))
    vec = lambda v: v[None]
    ws = [w_up.astype(BF16), w_down.astype(BF16), vec(ln2_g), vec(ln2_b), w_pg.astype(BF16), vec(b_pg),
          w_ple.astype(BF16), vec(ln3_g), vec(ln3_b)]
    return pl.pallas_call(
        functools.partial(_ffn_kernel, ff_chunk=1024),
        grid=(t // tm,),
        in_specs=[row(d), row(p2.shape[1])] + [_const_spec(w.shape) for w in ws],
        out_specs=row(d),
        out_shape=jax.ShapeDtypeStruct((t, d), F32),
        compiler_params=pltpu.CompilerParams(dimension_semantics=("parallel",),
                                             vmem_limit_bytes=VMEM_LIMIT),
        name="ffn",
    )(h1, p2, *ws)


def kernel(x, p, positions, ln_in_g, ln_in_b, w_in, b_gate, q_norm_g, w_uq, kv_norm_g, w_ukv, w_attn_br, a_re, a_im, log_dt, b_re, b_im, c_re, c_im, d_skip, w_glu, b_glu, w_ssm_br, w_o, ln1_g, ln1_b, w_up, w_down, ln2_g, ln2_b, w_ple_gate, b_ple_gate, w_ple, ln3_g, ln3_b):
    assert w_in.shape[0] == DEPTH
    b, s, d = x.shape
    t = b * s
    tm = min(512, s)
    tq = min(256, s)
    x2 = x.reshape(t, d)
    pos2 = positions.reshape(t, 1)

    q, k, v_t, u, ga, gb = _in_proj(x2, pos2, ln_in_g, ln_in_b, w_in[0], b_gate[0], q_norm_g[0], w_uq[0],
                                    kv_norm_g[0], w_ukv[0], s, tm, tq)
    attn = _attention(q.reshape(b, s, -1), k.reshape(b, s, -1), v_t, tq).reshape(t, -1)
    y = _s5(u, b, a_re[0], a_im[0], log_dt[0], b_re[0], b_im[0], c_re[0], c_im[0], d_skip[0])
    h1 = _merge(x2, attn, y, ga, gb, ln_in_g, ln_in_b, w_glu[0], b_glu[0], w_ssm_br[0], w_attn_br[0],
                w_o[0], ln1_g[0], ln1_b[0], tm)
    out = _ffn(h1, p[0].reshape(t, -1), w_up[0], w_down[0], ln2_g[0], ln2_b[0], w_ple_gate[0],
               b_ple_gate[0], w_ple[0], ln3_g[0], ln3_b[0], tm)
    return out.reshape(b, s, d)
```

```python
import functools
import math

import jax
import jax.numpy as jnp
from jax import lax
from jax.experimental import pallas as pl
from jax.experimental.pallas import tpu as pltpu

F32 = jnp.float32
BF16 = jnp.bfloat16

MLA_HEADS = 8
QK_NOPE_DIM = 64
QK_ROPE_DIM = 32
V_HEAD_DIM = 64
ROPE_THETA = 10000.0
SSM_GROUP = 16
LN_EPS = 1e-5
RMS_EPS = 1e-6
DEPTH = 1
DEEPNORM_ALPHA = (2.0 * DEPTH) ** 0.25

LANES = 128
MXU_DIM = 256
HEAD_PAD = LANES
SSM_SUB = MXU_DIM // SSM_GROUP
SSM_CHUNK = 64
VMEM_LIMIT = 56 * 1024 * 1024


def _const_spec(shape):
    nd = len(shape)
    return pl.BlockSpec(shape, lambda *_: (0,) * nd, pipeline_mode=pl.Buffered(1))


def _layer_norm(x, g, b):
    mu = jnp.mean(x, axis=-1, keepdims=True)
    xc = x - mu
    var = jnp.mean(xc * xc, axis=-1, keepdims=True)
    return xc * lax.rsqrt(var + LN_EPS) * g + b


def _rms_norm(x, g):
    return x * lax.rsqrt(jnp.mean(x * x, axis=-1, keepdims=True) + RMS_EPS) * g


def _dot(a, b):
    return jnp.dot(a, b, preferred_element_type=F32)


def _dot_t(a, b):
    return lax.dot_general(a, b, (((1,), (1,)), ((), ())), preferred_element_type=F32)


def _in_proj_kernel(x_ref, pos_ref, lng_ref, lnb_ref, wa_ref, bg_ref, qg_ref, wq_ref, kvg_ref,
                    wk_ref, wv_ref, invf_ref,
                    q_ref, k_ref, v_ref, u_ref, ga_ref, gb_ref, *, cols, scale):
    h = _layer_norm(x_ref[...], lng_ref[...], lnb_ref[...]).astype(BF16)

    def proj(name):
        a, b = cols[name]
        return _dot(h, wa_ref[:, a:b])

    ang = pos_ref[...].astype(F32) * invf_ref[...]
    cos = jnp.cos(ang)
    sin = jnp.sin(ang)
    cos2 = jnp.concatenate([cos, cos], axis=1)
    sin2 = jnp.concatenate([sin, sin], axis=1)

    cqn = _rms_norm(proj("cq"), qg_ref[...]).astype(BF16)
    ckvn = _rms_norm(proj("ckv"), kvg_ref[...]).astype(BF16)
    k_rope = proj("kr") * cos + proj("kr_sw") * sin
    k_rope2 = jnp.concatenate([k_rope, k_rope], axis=1)

    width = wk_ref.shape[1]
    for c in range(width // MXU_DIM):
        sl = slice(c * MXU_DIM, (c + 1) * MXU_DIM)
        q = _dot(cqn, wq_ref[:, sl])
        q_sw = _dot(cqn, wq_ref[:, width + c * MXU_DIM: width + (c + 1) * MXU_DIM])
        q_ref[:, sl] = ((q * cos2 + q_sw * sin2) * scale).astype(BF16)
        k_ref[:, sl] = (_dot(ckvn, wk_ref[:, sl]) + k_rope2).astype(BF16)
    v_t = _dot_t(wv_ref[...], ckvn).astype(BF16)
    tk = v_ref.shape[-1]
    for c in range(v_ref.shape[1]):
        v_ref[0, c] = v_t[:, c * tk:(c + 1) * tk]
    u_ref[...] = proj("u").astype(BF16)
    d = ga_ref.shape[1]
    ga_ref[...] = jax.nn.sigmoid(proj("ga") + bg_ref[:, :d]).astype(BF16)
    gb_ref[...] = jax.nn.sigmoid(proj("gb") + bg_ref[:, d:]).astype(BF16)


def _rope_swap(w):
    half = w.shape[-1] // 2
    return jnp.concatenate([-w[..., half:], w[..., :half]], axis=-1)


def _head_slot(nope, rope):
    k, h, _ = rope.shape
    parts = [nope if nope is not None else jnp.zeros((k, h, QK_NOPE_DIM), rope.dtype), rope]
    used = QK_NOPE_DIM + QK_ROPE_DIM
    parts.append(jnp.zeros((k, h, HEAD_PAD - used), rope.dtype))
    return jnp.concatenate(parts, axis=-1).reshape(k, h * HEAD_PAD)


def _in_proj(x2, pos2, ln_g, ln_b, w_in, b_gate, q_norm_g, w_uq, kv_norm_g, w_ukv, seq, tm, tk):
    t, d = x2.shape
    tiles_per_seq = seq // tm
    q_lora = q_norm_g.shape[0]
    kv_lora = kv_norm_g.shape[0]
    ssm_w = w_in.shape[1] - (q_lora + kv_lora + QK_ROPE_DIM + 2 * d)
    heads = MLA_HEADS

    o = 0
    w_cq = w_in[:, o:o + q_lora]; o += q_lora
    w_ckv = w_in[:, o:o + kv_lora]; o += kv_lora
    w_kr = w_in[:, o:o + QK_ROPE_DIM]; o += QK_ROPE_DIM
    w_u = w_in[:, o:o + ssm_w]; o += ssm_w
    w_ga = w_in[:, o:o + d]; o += d
    w_gb = w_in[:, o:o + d]
    kr_slot = _head_slot(None, w_kr[:, None, :])
    kr_sw_slot = _head_slot(None, _rope_swap(w_kr)[:, None, :])
    pieces = [("cq", w_cq), ("ckv", w_ckv), ("kr", kr_slot), ("kr_sw", kr_sw_slot),
              ("u", w_u), ("ga", w_ga), ("gb", w_gb)]
    cols, o = {}, 0
    for name, w in pieces:
        cols[name] = (o, o + w.shape[1])
        o += w.shape[1]
    w_a = jnp.concatenate([w for _, w in pieces], axis=1).astype(BF16)

    wq = w_uq.reshape(q_lora, heads, QK_NOPE_DIM + QK_ROPE_DIM)
    wq_nope, wq_rope = wq[..., :QK_NOPE_DIM], wq[..., QK_NOPE_DIM:]
    w_q = jnp.concatenate([_head_slot(wq_nope, wq_rope),
                           _head_slot(jnp.zeros_like(wq_nope), _rope_swap(wq_rope))], axis=1).astype(BF16)
    wkv = w_ukv.reshape(kv_lora, heads, QK_NOPE_DIM + V_HEAD_DIM)
    w_k = _head_slot(wkv[..., :QK_NOPE_DIM], jnp.zeros((kv_lora, heads, QK_ROPE_DIM), F32)).astype(BF16)
    w_v = wkv[..., QK_NOPE_DIM:].reshape(kv_lora, heads * V_HEAD_DIM).T.astype(BF16)

    inv_freq = ROPE_THETA ** (-jnp.arange(0, QK_ROPE_DIM, 2, dtype=F32) / QK_ROPE_DIM)
    invf = _head_slot(None, jnp.concatenate([inv_freq, inv_freq])[None, None, :])

    hw = heads * HEAD_PAD
    hv = heads * V_HEAD_DIM
    row = lambda w: pl.BlockSpec((tm, w), lambda i: (i, 0))
    v_spec = pl.BlockSpec((1, tm // tk, hv, tk), lambda i: (i // tiles_per_seq, i % tiles_per_seq, 0, 0))
    scale = (QK_NOPE_DIM + QK_ROPE_DIM) ** -0.5 * math.log2(math.e)
    return pl.pallas_call(
        functools.partial(_in_proj_kernel, cols=cols, scale=scale),
        grid=(t // tm,),
        in_specs=[row(d), row(1), _const_spec((1, d)), _const_spec((1, d)), _const_spec(w_a.shape),
                  _const_spec((1, 2 * d)), _const_spec((1, q_lora)), _const_spec(w_q.shape),
                  _const_spec((1, kv_lora)), _const_spec(w_k.shape), _const_spec(w_v.shape),
                  _const_spec((1, HEAD_PAD))],
        out_specs=[row(hw), row(hw), v_spec, row(ssm_w), row(d), row(d)],
        out_shape=[jax.ShapeDtypeStruct((t, hw), BF16), jax.ShapeDtypeStruct((t, hw), BF16),
                   jax.ShapeDtypeStruct((t // seq, seq // tk, hv, tk), BF16),
                   jax.ShapeDtypeStruct((t, ssm_w), BF16),
                   jax.ShapeDtypeStruct((t, d), BF16), jax.ShapeDtypeStruct((t, d), BF16)],
        compiler_params=pltpu.CompilerParams(dimension_semantics=("parallel",),
                                             vmem_limit_bytes=VMEM_LIMIT),
        name="in_proj",
    )(x2, pos2, ln_g[None], ln_b[None], w_a, b_gate[None], q_norm_g[None], w_q, kv_norm_g[None],
      w_k, w_v, invf)


def _attn_kernel(q_ref, k_ref, v_ref, o_ref, s_buf, *, tq, tk):
    i = pl.program_id(2)
    key = lax.broadcasted_iota(jnp.int32, (tk, tq), 0)
    qry = lax.broadcasted_iota(jnp.int32, (tk, tq), 1)
    heads = range(2)
    qs = [q_ref[0, :, hh * HEAD_PAD:(hh + 1) * HEAD_PAD] for hh in heads]

    def scores(j, slot):
        start = pl.multiple_of(j * tk, tk)
        for hh in heads:
            s_buf[slot, hh] = _dot_t(k_ref[0, pl.ds(start, tk), hh * HEAD_PAD:(hh + 1) * HEAD_PAD], qs[hh])

    def update(j, slot, state, mask_offset):
        ps, stats = [], []
        for hh in heads:
            m, l, _ = state[hh]
            s = s_buf[slot, hh]
            if mask_offset is not None:
                s = jnp.where(key + mask_offset <= qry, s, -1e30)
            m_new = jnp.maximum(m, jnp.max(s, axis=0, keepdims=True))
            a = jnp.exp2(m - m_new)
            p = jnp.exp2(s - m_new)
            stats.append((m_new, a, a * l + jnp.sum(p, axis=0, keepdims=True)))
            ps.append(p.astype(BF16))
        out = []
        for hh in heads:
            m_new, a, l_new = stats[hh]
            v_t = v_ref[0, j, hh * V_HEAD_DIM:(hh + 1) * V_HEAD_DIM, :]
            out.append((m_new, l_new, a * state[hh][2] + _dot(v_t, ps[hh])))
        return tuple(out)

    def body(pair, state):
        blk = 2 * pair
        scores(blk + 1, 1)
        state = update(blk, 0, state, None)
        scores(blk + 2, 0)
        return update(blk + 1, 1, state, None)

    init = tuple((jnp.full((1, tq), -1e30, F32), jnp.zeros((1, tq), F32),
                  jnp.zeros((V_HEAD_DIM, tq), F32)) for _ in heads)
    scores(0, 0)
    state = lax.fori_loop(0, i, body, init)
    diag = 2 * i
    scores(diag + 1, 1)
    state = update(diag, 0, state, 0)
    state = update(diag + 1, 1, state, tk)
    o_t = jnp.concatenate([acc / l for _, l, acc in state], axis=0)
    o_ref[0] = o_t.T.astype(BF16)


def _attention(q, k, v_t, tq):
    b, s, _ = q.shape
    pairs = MLA_HEADS // 2
    nkv, _, tk = v_t.shape[1:]
    assert tq == 2 * tk
    return pl.pallas_call(
        functools.partial(_attn_kernel, tq=tq, tk=tk),
        grid=(b, pairs, s // tq),
        in_specs=[pl.BlockSpec((1, tq, 2 * HEAD_PAD), lambda bi, hp, i: (bi, i, hp)),
                  pl.BlockSpec((1, s, 2 * HEAD_PAD), lambda bi, hp, i: (bi, 0, hp)),
                  pl.BlockSpec((1, nkv, 2 * V_HEAD_DIM, tk), lambda bi, hp, i: (bi, 0, hp, 0))],
        out_specs=pl.BlockSpec((1, tq, 2 * V_HEAD_DIM), lambda bi, hp, i: (bi, i, hp)),
        out_shape=jax.ShapeDtypeStruct((b, s, MLA_HEADS * V_HEAD_DIM), BF16),
        scratch_shapes=[pltpu.VMEM((2, 2, tk, tq), F32)],
        compiler_params=pltpu.CompilerParams(
            dimension_semantics=("parallel", "parallel", "arbitrary"), vmem_limit_bytes=VMEM_LIMIT),
        name="attention",
    )(q, k, v_t)


def _s5_kernel(u_ref, t_ref, w_ref, vt_ref, al_ref, d_ref, y_ref, z_ref, xp_ref, *, batch, chunks):
    u = u_ref[0]
    z_ref[...] = _dot(u, w_ref[0])

    a_l = al_ref[0]
    half = a_l.shape[1] // 2
    a_same = jnp.concatenate([a_l[:, :half], a_l[:, :half]], axis=1)
    a_cross = jnp.concatenate([-a_l[:, half:], a_l[:, half:]], axis=1)

    def carry(c, x):
        rows = pl.ds(pl.multiple_of(c * batch, batch), batch)
        xp_ref[rows, :] = x
        return x * a_same + pltpu.roll(x, half, axis=1) * a_cross + z_ref[rows, :]

    lax.fori_loop(0, chunks, carry, jnp.zeros((batch, 2 * half), F32))

    xp = xp_ref[...].astype(BF16)
    tiles = t_ref.shape[1]
    for t in range(tiles):
        ts = slice(t * MXU_DIM, (t + 1) * MXU_DIM)
        acc = _dot_t(xp, vt_ref[0, ts, :])
        for s in range(t + 1):
            acc += _dot(u[:, s * MXU_DIM:(s + 1) * MXU_DIM], t_ref[0, t - s])
        y = acc + d_ref[0, :, ts] * u[:, ts].astype(F32)
        y_ref[0, :, ts] = jax.nn.gelu(y).astype(BF16)


def _s5_param_kernel(are_ref, aim_ref, ldt_ref, btr_ref, bti_ref, cr_ref, ci_ref,
                     toe_ref, w_ref, vt_ref, al_ref, *, chunk):
    n, p = cr_ref.shape[1:]
    hi = lax.Precision.HIGHEST
    dt = jnp.exp(ldt_ref[0])
    lam_re = jnp.minimum(are_ref[0], -1e-4)
    lam_im = aim_ref[0]
    twice = lambda x: jnp.concatenate([x, x], axis=1)
    rate, freq = twice(lam_re) * dt, twice(lam_im) * dt
    first = lax.broadcasted_iota(jnp.int32, (1, 2 * p), 1) < p

    def powers(j):
        ang = freq * j
        return jnp.exp(rate * j) * jnp.where(first, jnp.cos(ang), jnp.sin(ang))

    j0 = lax.broadcasted_iota(jnp.int32, (chunk, 1), 0).astype(F32)
    one = jnp.ones((1, 1), F32)
    abar = powers(one)
    al_ref[0] = powers(one * chunk)
    nr, ni = abar[:, :p] - 1.0, abar[:, p:]
    den = lam_re * lam_re + lam_im * lam_im
    f_re = (nr * lam_re + ni * lam_im) / den
    f_im = (ni * lam_re - nr * lam_im) / den
    bt_re = f_re * btr_ref[0] - f_im * bti_ref[0]
    bt_im = f_re * bti_ref[0] + f_im * btr_ref[0]

    rows = chunk * n
    r_idx = lax.broadcasted_iota(jnp.int32, (rows, chunk), 0)
    j_idx = lax.broadcasted_iota(jnp.int32, (rows, chunk), 1)
    rep = (r_idx // n == j_idx).astype(F32)

    def per_step(pw):
        x = jnp.dot(rep, pw, precision=hi, preferred_element_type=F32)
        return x, pltpu.roll(x, p, axis=1)

    def per_chan(x):
        return jnp.broadcast_to(twice(x)[None], (chunk, n, 2 * p)).reshape(rows, 2 * p)

    cr, ci = per_chan(cr_ref[0]), per_chan(ci_ref[0])

    def out_rows(pw):
        x, x_sw = per_step(pw)
        a, b = cr * x, ci * x_sw
        return jnp.where(first, a - b, -(a + b))

    vt_ref[0] = out_rows(powers(j0 + 1.0)).astype(BF16)
    x, x_sw = per_step(powers(chunk - 1.0 - j0))
    a, b = x * per_chan(bt_re), x_sw * per_chan(bt_im)
    w_ref[0] = jnp.where(first, a - b, a + b).astype(BF16)

    k_t = lax.dot_general(jnp.concatenate([bt_re, bt_im], axis=1), out_rows(powers(j0)),
                          (((1,), (1,)), ((), ())), precision=hi, preferred_element_type=F32)
    sub = SSM_SUB
    k_t = jnp.concatenate([jnp.zeros((n, (sub - 1) * n), F32), k_t], axis=1)
    for d in range(chunk // sub):
        for s in range(sub):
            off = (sub - 1 + sub * d - s) * n
            toe_ref[0, d, s * n:(s + 1) * n, :] = k_t[:, off:off + sub * n].astype(BF16)


def _s5_weights(a_re, a_im, log_dt, b_re, b_im, c_re, c_im, chunk):
    g, p = a_re.shape
    n = b_re.shape[-1]
    tiles, tw, cw = chunk // SSM_SUB, SSM_SUB * n, chunk * n
    grp = lambda shape: pl.BlockSpec((1,) + shape, lambda i: (i,) + (0,) * len(shape))
    return pl.pallas_call(
        functools.partial(_s5_param_kernel, chunk=chunk),
        grid=(g,),
        in_specs=[grp((1, p)), grp((1, p)), grp((1, 1)), grp((n, p)), grp((n, p)), grp((n, p)), grp((n, p))],
        out_specs=[grp((tiles, tw, tw)), grp((cw, 2 * p)), grp((cw, 2 * p)), grp((1, 2 * p))],
        out_shape=[jax.ShapeDtypeStruct((g, tiles, tw, tw), BF16), jax.ShapeDtypeStruct((g, cw, 2 * p), BF16),
                   jax.ShapeDtypeStruct((g, cw, 2 * p), BF16), jax.ShapeDtypeStruct((g, 1, 2 * p), F32)],
        compiler_params=pltpu.CompilerParams(dimension_semantics=("parallel",)),
        name="s5_params",
    )(a_re[:, None], a_im[:, None], log_dt[:, None, None], jnp.swapaxes(b_re, 1, 2),
      jnp.swapaxes(b_im, 1, 2), c_re, c_im)


def _s5(u2, batch, a_re, a_im, log_dt, b_re, b_im, c_re, c_im, d_skip):
    t, width = u2.shape
    seq = t // batch
    n = SSM_GROUP
    g = width // n
    chunk = SSM_CHUNK
    chunks = seq // chunk
    toe, w, vt, a_l = _s5_weights(a_re, a_im, log_dt, b_re, b_im, c_re, c_im, chunk)
    dvec = jnp.tile(d_skip.reshape(g, 1, n), (1, 1, chunk))
    rows, cw = chunks * batch, chunk * n
    up = u2.reshape(batch, chunks, chunk, g, n).transpose(3, 1, 0, 2, 4).reshape(g, rows, cw)
    grp = lambda shape: pl.BlockSpec((1,) + shape, lambda i: (i,) + (0,) * len(shape))
    p2 = a_l.shape[-1]
    yp = pl.pallas_call(
        functools.partial(_s5_kernel, batch=batch, chunks=chunks),
        grid=(g,),
        in_specs=[grp((rows, cw)), grp(toe.shape[1:]), grp((cw, p2)), grp((cw, p2)), grp((1, p2)),
                  grp((1, cw))],
        out_specs=grp((rows, cw)),
        out_shape=jax.ShapeDtypeStruct((g, rows, cw), BF16),
        scratch_shapes=[pltpu.VMEM((rows, p2), F32), pltpu.VMEM((rows, p2), F32)],
        compiler_params=pltpu.CompilerParams(dimension_semantics=("parallel",),
                                             vmem_limit_bytes=VMEM_LIMIT),
        name="s5",
    )(up, toe, w, vt, a_l, dvec)
    return yp.reshape(g, chunks, batch, chunk, n).transpose(2, 1, 3, 0, 4).reshape(t, width)


def _merge_kernel(x_ref, attn_ref, y_ref, ga_ref, gb_ref, lng_ref, lnb_ref, wglu_ref, bglu_ref,
                  wsb_ref, wab_ref, wo_ref, g1_ref, b1_ref, o_ref):
    h = _layer_norm(x_ref[...], lng_ref[...], lnb_ref[...])
    y = y_ref[...]
    yg = (y.astype(F32) * jax.nn.sigmoid(_dot(y, wglu_ref[...]) + bglu_ref[...])).astype(BF16)
    branch_a = _dot(attn_ref[...], wab_ref[...])
    branch_b = _dot(yg, wsb_ref[...])
    merged = ga_ref[...].astype(F32) * branch_a + gb_ref[...].astype(F32) * branch_b
    r = DEEPNORM_ALPHA * h + _dot(merged.astype(BF16), wo_ref[...])
    o_ref[...] = _layer_norm(r, g1_ref[...], b1_ref[...])


def _merge(x2, attn, y, ga, gb, ln_g, ln_b, w_glu, b_glu, w_ssm_br, w_attn_br, w_o, ln1_g, ln1_b, tm):
    t, d = x2.shape
    row = lambda w: pl.BlockSpec((tm, w), lambda i: (i, 0))
    vec = lambda v: v[None]
    ws = [w_glu.astype(BF16), vec(b_glu), w_ssm_br.astype(BF16), w_attn_br.astype(BF16),
          w_o.astype(BF16), vec(ln1_g), vec(ln1_b)]
    return pl.pallas_call(
        _merge_kernel,
        grid=(t // tm,),
        in_specs=[row(d), row(attn.shape[1]), row(y.shape[1]), row(d), row(d),
                  _const_spec((1, d)), _const_spec((1, d))] + [_const_spec(w.shape) for w in ws],
        out_specs=row(d),
        out_shape=jax.ShapeDtypeStruct((t, d), F32),
        compiler_params=pltpu.CompilerParams(dimension_semantics=("parallel",),
                                             vmem_limit_bytes=VMEM_LIMIT),
        name="merge",
    )(x2, attn, y, ga, gb, vec(ln_g), vec(ln_b), *ws)


def _ffn_kernel(h_ref, p_ref, wup_ref, wdn_ref, g2_ref, b2_ref, wpg_ref, bpg_ref, wple_ref,
                g3_ref, b3_ref, o_ref, *, ff_chunk):
    h = h_ref[...]
    hb = h.astype(BF16)
    ff = jnp.zeros_like(h)
    for c in range(wup_ref.shape[1] // ff_chunk):
        cs = slice(c * ff_chunk, (c + 1) * ff_chunk)
        up = jnp.maximum(_dot(hb, wup_ref[:, cs]), 0.0)
        ff += _dot((up * up).astype(BF16), wdn_ref[cs, :])
    h2 = _layer_norm(DEEPNORM_ALPHA * h + ff, g2_ref[...], b2_ref[...])
    gate = jax.nn.sigmoid(_dot(h2.astype(BF16), wpg_ref[...]) + bpg_ref[...])
    ple = gate * _dot(p_ref[...].astype(BF16), wple_ref[...])
    o_ref[...] = _layer_norm(DEEPNORM_ALPHA * h2 + ple, g3_ref[...], b3_ref[...])


def _ffn(h1, p2, w_up, w_down, ln2_g, ln2_b, w_pg, b_pg, w_ple, ln3_g, ln3_b, tm):
    t, d = h1.shape
    row = lambda w: pl.BlockSpec((tm, w), lambda i: (i, 0))
    vec = lambda v: v[None]
    ws = [w_up.astype(BF16), w_down.astype(BF16), vec(ln2_g), vec(ln2_b), w_pg.astype(BF16), vec(b_pg),
          w_ple.astype(BF16), vec(ln3_g), vec(ln3_b)]
    return pl.pallas_call(
        functools.partial(_ffn_kernel, ff_chunk=1024),
        grid=(t // tm,),
        in_specs=[row(d), row(p2.shape[1])] + [_const_spec(w.shape) for w in ws],
        out_specs=row(d),
        out_shape=jax.ShapeDtypeStruct((t, d), F32),
        compiler_params=pltpu.CompilerParams(dimension_semantics=("parallel",),
                                             vmem_limit_bytes=VMEM_LIMIT),
        name="ffn",
    )(h1, p2, *ws)


def kernel(x, p, positions, ln_in_g, ln_in_b, w_in, b_gate, q_norm_g, w_uq, kv_norm_g, w_ukv, w_attn_br, a_re, a_im, log_dt, b_re, b_im, c_re, c_im, d_skip, w_glu, b_glu, w_ssm_br, w_o, ln1_g, ln1_b, w_up, w_down, ln2_g, ln2_b, w_ple_gate, b_ple_gate, w_ple, ln3_g, ln3_b):
    assert w_in.shape[0] == DEPTH
    b, s, d = x.shape
    t = b * s
    tm = min(512, s)
    tq = min(512, s)
    x2 = x.reshape(t, d)
    pos2 = positions.reshape(t, 1)

    q, k, v_t, u, ga, gb = _in_proj(x2, pos2, ln_in_g, ln_in_b, w_in[0], b_gate[0], q_norm_g[0], w_uq[0],
                                    kv_norm_g[0], w_ukv[0], s, tm, tq // 2)
    attn = _attention(q.reshape(b, s, -1), k.reshape(b, s, -1), v_t, tq).reshape(t, -1)
    y = _s5(u, b, a_re[0], a_im[0], log_dt[0], b_re[0], b_im[0], c_re[0], c_im[0], d_skip[0])
    h1 = _merge(x2, attn, y, ga, gb, ln_in_g, ln_in_b, w_glu[0], b_glu[0], w_ssm_br[0], w_attn_br[0],
                w_o[0], ln1_g[0], ln1_b[0], tm)
    out = _ffn(h1, p[0].reshape(t, -1), w_up[0], w_down[0], ln2_g[0], ln2_b[0], w_ple_gate[0],
               b_ple_gate[0], w_ple[0], ln3_g[0], ln3_b[0], tm)
    return out.reshape(b, s, d)
```

```python
import functools
import math

import jax
import jax.numpy as jnp
from jax import lax
from jax.experimental import pallas as pl
from jax.experimental.pallas import tpu as pltpu

F32 = jnp.float32
BF16 = jnp.bfloat16

MLA_HEADS = 8
QK_NOPE_DIM = 64
QK_ROPE_DIM = 32
V_HEAD_DIM = 64
ROPE_THETA = 10000.0
SSM_GROUP = 16
LN_EPS = 1e-5
RMS_EPS = 1e-6
DEPTH = 1
DEEPNORM_ALPHA = (2.0 * DEPTH) ** 0.25

LANES = 128
MXU_DIM = 256
HEAD_PAD = LANES
SSM_SUB = MXU_DIM // SSM_GROUP
SSM_CHUNK = 64
SCAN_STEPS = 8
VMEM_LIMIT = 56 * 1024 * 1024


def _const_spec(shape):
    nd = len(shape)
    return pl.BlockSpec(shape, lambda *_: (0,) * nd, pipeline_mode=pl.Buffered(1))


def _layer_norm(x, g, b):
    mu = jnp.mean(x, axis=-1, keepdims=True)
    xc = x - mu
    var = jnp.mean(xc * xc, axis=-1, keepdims=True)
    return xc * lax.rsqrt(var + LN_EPS) * g + b


def _rms_norm(x, g):
    return x * lax.rsqrt(jnp.mean(x * x, axis=-1, keepdims=True) + RMS_EPS) * g


def _dot(a, b):
    return jnp.dot(a, b, preferred_element_type=F32)


def _dot_t(a, b):
    return lax.dot_general(a, b, (((1,), (1,)), ((), ())), preferred_element_type=F32)


def _in_proj_kernel(x_ref, pos_ref, lng_ref, lnb_ref, wa_ref, bg_ref, qg_ref, wq_ref, kvg_ref,
                    wk_ref, wv_ref, wu_ref, invf_ref,
                    q_ref, k_ref, v_ref, ut_ref, ga_ref, gb_ref, *, cols, scale):
    h = _layer_norm(x_ref[...], lng_ref[...], lnb_ref[...]).astype(BF16)

    def proj(name):
        a, b = cols[name]
        return _dot(h, wa_ref[:, a:b])

    ang = pos_ref[...].astype(F32) * invf_ref[...]
    cos = jnp.cos(ang)
    sin = jnp.sin(ang)
    cos2 = jnp.concatenate([cos, cos], axis=1)
    sin2 = jnp.concatenate([sin, sin], axis=1)

    cqn = _rms_norm(proj("cq"), qg_ref[...]).astype(BF16)
    ckvn = _rms_norm(proj("ckv"), kvg_ref[...]).astype(BF16)
    k_rope = proj("kr") * cos + proj("kr_sw") * sin
    k_rope2 = jnp.concatenate([k_rope, k_rope], axis=1)

    width = wk_ref.shape[1]
    for c in range(width // MXU_DIM):
        sl = slice(c * MXU_DIM, (c + 1) * MXU_DIM)
        q = _dot(cqn, wq_ref[:, sl])
        q_sw = _dot(cqn, wq_ref[:, width + c * MXU_DIM: width + (c + 1) * MXU_DIM])
        q_ref[:, sl] = ((q * cos2 + q_sw * sin2) * scale).astype(BF16)
        k_ref[:, sl] = (_dot(ckvn, wk_ref[:, sl]) + k_rope2).astype(BF16)
    v_ref[...] = _dot(ckvn, wv_ref[...]).astype(BF16)
    ut_ref[...] = _dot_t(wu_ref[...], h).astype(BF16).reshape(ut_ref.shape)
    d = ga_ref.shape[1]
    ga_ref[...] = jax.nn.sigmoid(proj("ga") + bg_ref[:, :d]).astype(BF16)
    gb_ref[...] = jax.nn.sigmoid(proj("gb") + bg_ref[:, d:]).astype(BF16)


def _rope_swap(w):
    half = w.shape[-1] // 2
    return jnp.concatenate([-w[..., half:], w[..., :half]], axis=-1)


def _head_slot(nope, rope):
    k, h, _ = rope.shape
    parts = [nope if nope is not None else jnp.zeros((k, h, QK_NOPE_DIM), rope.dtype), rope]
    used = QK_NOPE_DIM + QK_ROPE_DIM
    parts.append(jnp.zeros((k, h, HEAD_PAD - used), rope.dtype))
    return jnp.concatenate(parts, axis=-1).reshape(k, h * HEAD_PAD)


def _in_proj(x2, pos3, ln_g, ln_b, w_in, b_gate, q_norm_g, w_uq, kv_norm_g, w_ukv):
    steps, nc, _ = pos3.shape
    d = x2.shape[1] // steps
    q_lora = q_norm_g.shape[0]
    kv_lora = kv_norm_g.shape[0]
    ssm_w = w_in.shape[1] - (q_lora + kv_lora + QK_ROPE_DIM + 2 * d)
    heads = MLA_HEADS

    o = 0
    w_cq = w_in[:, o:o + q_lora]; o += q_lora
    w_ckv = w_in[:, o:o + kv_lora]; o += kv_lora
    w_kr = w_in[:, o:o + QK_ROPE_DIM]; o += QK_ROPE_DIM
    w_u = w_in[:, o:o + ssm_w]; o += ssm_w
    w_ga = w_in[:, o:o + d]; o += d
    w_gb = w_in[:, o:o + d]
    kr_slot = _head_slot(None, w_kr[:, None, :])
    kr_sw_slot = _head_slot(None, _rope_swap(w_kr)[:, None, :])
    pieces = [("cq", w_cq), ("ckv", w_ckv), ("kr", kr_slot), ("kr_sw", kr_sw_slot),
              ("ga", w_ga), ("gb", w_gb)]
    cols, o = {}, 0
    for name, w in pieces:
        cols[name] = (o, o + w.shape[1])
        o += w.shape[1]
    w_a = jnp.concatenate([w for _, w in pieces], axis=1).astype(BF16)

    wq = w_uq.reshape(q_lora, heads, QK_NOPE_DIM + QK_ROPE_DIM)
    wq_nope, wq_rope = wq[..., :QK_NOPE_DIM], wq[..., QK_NOPE_DIM:]
    w_q = jnp.concatenate([_head_slot(wq_nope, wq_rope),
                           _head_slot(jnp.zeros_like(wq_nope), _rope_swap(wq_rope))], axis=1).astype(BF16)
    wkv = w_ukv.reshape(kv_lora, heads, QK_NOPE_DIM + V_HEAD_DIM)
    w_k = _head_slot(wkv[..., :QK_NOPE_DIM], jnp.zeros((kv_lora, heads, QK_ROPE_DIM), F32)).astype(BF16)
    w_v = wkv[..., QK_NOPE_DIM:].reshape(kv_lora, heads * V_HEAD_DIM).astype(BF16)
    w_ut = w_u.T.astype(BF16)

    inv_freq = ROPE_THETA ** (-jnp.arange(0, QK_ROPE_DIM, 2, dtype=F32) / QK_ROPE_DIM)
    invf = _head_slot(None, jnp.concatenate([inv_freq, inv_freq])[None, None, :])

    hw = heads * HEAD_PAD
    hv = heads * V_HEAD_DIM
    groups = ssm_w // SSM_GROUP
    col = lambda w: pl.BlockSpec((nc, w), lambda i: (0, i))
    scale = (QK_NOPE_DIM + QK_ROPE_DIM) ** -0.5 * math.log2(math.e)
    flat = lambda w: jax.ShapeDtypeStruct((nc, steps * w), BF16)
    return pl.pallas_call(
        functools.partial(_in_proj_kernel, cols=cols, scale=scale),
        grid=(steps,),
        in_specs=[col(d), pl.BlockSpec((None, nc, 1), lambda i: (i, 0, 0)),
                  _const_spec((1, d)), _const_spec((1, d)), _const_spec(w_a.shape),
                  _const_spec((1, 2 * d)), _const_spec((1, q_lora)), _const_spec(w_q.shape),
                  _const_spec((1, kv_lora)), _const_spec(w_k.shape), _const_spec(w_v.shape),
                  _const_spec(w_ut.shape), _const_spec((1, HEAD_PAD))],
        out_specs=[col(hw), col(hw), col(hv),
                   pl.BlockSpec((groups, SSM_GROUP, nc), lambda i: (0, i, 0)), col(d), col(d)],
        out_shape=[flat(hw), flat(hw), flat(hv),
                   jax.ShapeDtypeStruct((groups, steps * SSM_GROUP, nc), BF16), flat(d), flat(d)],
        compiler_params=pltpu.CompilerParams(dimension_semantics=("parallel",),
                                             vmem_limit_bytes=VMEM_LIMIT),
        name="in_proj",
    )(x2, pos3, ln_g[None], ln_b[None], w_a, b_gate[None], q_norm_g[None], w_q, kv_norm_g[None],
      w_k, w_v, w_ut, invf)


def _attn_kernel(q_ref, k_ref, v_ref, o_ref, s_buf, vt_buf, *, tq, tk):
    i = pl.program_id(2)

    @pl.when(i == 0)
    def _():
        for j in range(vt_buf.shape[0]):
            vt_buf[j] = v_ref[0, j * tk:(j + 1) * tk, :].T

    key = lax.broadcasted_iota(jnp.int32, (tk, tq), 0)
    qry = lax.broadcasted_iota(jnp.int32, (tk, tq), 1)
    heads = range(2)
    qs = [q_ref[0, :, hh * HEAD_PAD:(hh + 1) * HEAD_PAD] for hh in heads]

    def scores(j, slot):
        start = pl.multiple_of(j * tk, tk)
        for hh in heads:
            s_buf[slot, hh] = _dot_t(k_ref[0, pl.ds(start, tk), hh * HEAD_PAD:(hh + 1) * HEAD_PAD], qs[hh])

    def update(j, slot, state, mask_offset):
        ps, stats = [], []
        for hh in heads:
            m, l, _ = state[hh]
            s = s_buf[slot, hh]
            if mask_offset is not None:
                s = jnp.where(key + mask_offset <= qry, s, -1e30)
            m_new = jnp.maximum(m, jnp.max(s, axis=0, keepdims=True))
            a = jnp.exp2(m - m_new)
            p = jnp.exp2(s - m_new)
            stats.append((m_new, a, a * l + jnp.sum(p, axis=0, keepdims=True)))
            ps.append(p.astype(BF16))
        out = []
        for hh in heads:
            m_new, a, l_new = stats[hh]
            v_t = vt_buf[j, hh * V_HEAD_DIM:(hh + 1) * V_HEAD_DIM, :]
            out.append((m_new, l_new, a * state[hh][2] + _dot(v_t, ps[hh])))
        return tuple(out)

    def body(pair, state):
        blk = 2 * pair
        scores(blk + 1, 1)
        state = update(blk, 0, state, None)
        scores(blk + 2, 0)
        return update(blk + 1, 1, state, None)

    init = tuple((jnp.full((1, tq), -1e30, F32), jnp.zeros((1, tq), F32),
                  jnp.zeros((V_HEAD_DIM, tq), F32)) for _ in heads)
    scores(0, 0)
    state = lax.fori_loop(0, i, body, init)
    diag = 2 * i
    scores(diag + 1, 1)
    state = update(diag, 0, state, 0)
    state = update(diag + 1, 1, state, tk)
    o_t = jnp.concatenate([acc / l for _, l, acc in state], axis=0)
    o_ref[0] = o_t.T.astype(BF16)


def _attention(q, k, v, tq):
    b, s, _ = q.shape
    pairs = MLA_HEADS // 2
    tk = tq // 2
    nkv = s // tk
    return pl.pallas_call(
        functools.partial(_attn_kernel, tq=tq, tk=tk),
        grid=(b, pairs, s // tq),
        in_specs=[pl.BlockSpec((1, tq, 2 * HEAD_PAD), lambda bi, hp, i: (bi, i, hp)),
                  pl.BlockSpec((1, s, 2 * HEAD_PAD), lambda bi, hp, i: (bi, 0, hp)),
                  pl.BlockSpec((1, s, 2 * V_HEAD_DIM), lambda bi, hp, i: (bi, 0, hp))],
        out_specs=pl.BlockSpec((1, tq, 2 * V_HEAD_DIM), lambda bi, hp, i: (bi, i, hp)),
        out_shape=jax.ShapeDtypeStruct((b, s, MLA_HEADS * V_HEAD_DIM), BF16),
        scratch_shapes=[pltpu.VMEM((2, 2, tk, tq), F32), pltpu.VMEM((nkv, 2 * V_HEAD_DIM, tk), BF16)],
        compiler_params=pltpu.CompilerParams(
            dimension_semantics=("parallel", "parallel", "arbitrary"), vmem_limit_bytes=VMEM_LIMIT),
        name="attention",
    )(q, k, v)


def _s5_kernel(ut_ref, tt_ref, wt_ref, vt_ref, sc_ref, d_ref, y_ref, *, seq_chunks):
    nc = ut_ref.shape[2]
    z = _dot(wt_ref[0], ut_ref[0])
    half = z.shape[0] // 2
    pos = lax.broadcasted_iota(jnp.int32, (1, nc), 1) % seq_chunks

    x = z
    for k in range(seq_chunks.bit_length() - 1):
        sh = 1 << k
        a = sc_ref[0, :, k:k + 1]
        a_re, a_im = a[:half], a[half:]
        xs = jnp.where(pos >= sh, pltpu.roll(x, sh, axis=1), 0.0)
        xs_re, xs_im = xs[:half], xs[half:]
        x = x + jnp.concatenate([a_re * xs_re - a_im * xs_im, a_re * xs_im + a_im * xs_re], axis=0)
    x_in = jnp.where(pos >= 1, pltpu.roll(x, 1, axis=1), 0.0).astype(BF16)

    for t in range(tt_ref.shape[1]):
        ts = slice(t * MXU_DIM, (t + 1) * MXU_DIM)
        acc = _dot(vt_ref[0, ts, :], x_in)
        for s in range(t + 1):
            acc += _dot(tt_ref[0, t - s], ut_ref[0, s * MXU_DIM:(s + 1) * MXU_DIM, :])
        y = acc + d_ref[0, ts, :] * ut_ref[0, ts, :].astype(F32)
        y_ref[0, ts, :] = jax.nn.gelu(y).astype(BF16)


def _s5_param_kernel(are_ref, aim_ref, ldt_ref, btr_ref, bti_ref, cr_ref, ci_ref,
                     tt_ref, wt_ref, vt_ref, sc_ref, *, chunk):
    n, p = cr_ref.shape[1:]
    hi = lax.Precision.HIGHEST
    dt = jnp.exp(ldt_ref[0])
    lam_re = jnp.minimum(are_ref[0], -1e-4)
    lam_im = aim_ref[0]
    twice = lambda x: jnp.concatenate([x, x], axis=1)
    rate, freq = twice(lam_re) * dt, twice(lam_im) * dt
    first = lax.broadcasted_iota(jnp.int32, (1, 2 * p), 1) < p

    def powers(j):
        ang = freq * j
        return jnp.exp(rate * j) * jnp.where(first, jnp.cos(ang), jnp.sin(ang))

    j0 = lax.broadcasted_iota(jnp.int32, (chunk, 1), 0).astype(F32)
    abar = powers(jnp.ones((1, 1), F32))
    k_idx = lax.broadcasted_iota(jnp.int32, (sc_ref.shape[2], 1), 0).astype(F32)
    sc_ref[0] = powers(jnp.exp2(k_idx) * chunk).T
    nr, ni = abar[:, :p] - 1.0, abar[:, p:]
    den = lam_re * lam_re + lam_im * lam_im
    f_re = (nr * lam_re + ni * lam_im) / den
    f_im = (ni * lam_re - nr * lam_im) / den
    bt_re = f_re * btr_ref[0] - f_im * bti_ref[0]
    bt_im = f_re * bti_ref[0] + f_im * btr_ref[0]

    rows = chunk * n
    r_idx = lax.broadcasted_iota(jnp.int32, (rows, chunk), 0)
    j_idx = lax.broadcasted_iota(jnp.int32, (rows, chunk), 1)
    rep = (r_idx // n == j_idx).astype(F32)

    def per_step(pw):
        x = jnp.dot(rep, pw, precision=hi, preferred_element_type=F32)
        return x, pltpu.roll(x, p, axis=1)

    def per_chan(x):
        return jnp.broadcast_to(twice(x)[None], (chunk, n, 2 * p)).reshape(rows, 2 * p)

    x, x_sw = per_step(powers(j0 + 1.0))
    a, b = per_chan(cr_ref[0]) * x, per_chan(ci_ref[0]) * x_sw
    vt_ref[0] = jnp.where(first, a - b, -(a + b)).astype(BF16)
    x, x_sw = per_step(powers(chunk - 1.0 - j0))
    a, b = x * per_chan(bt_re), x_sw * per_chan(bt_im)
    w = jnp.where(first, a - b, a + b)
    wt_ref[0] = w.T.astype(BF16)

    c_cat = jnp.concatenate([cr_ref[0], -ci_ref[0]], axis=1)
    k_rev = lax.dot_general(c_cat, w, (((1,), (1,)), ((), ())), precision=hi, preferred_element_type=F32)
    sub = SSM_SUB
    k_rev = jnp.concatenate([k_rev, jnp.zeros((n, (sub - 1) * n), F32)], axis=1)
    for d in range(chunk // sub):
        for t in range(sub):
            off = (chunk - 1 - sub * d - t) * n
            tt_ref[0, d, t * n:(t + 1) * n, :] = k_rev[:, off:off + sub * n].astype(BF16)


def _s5_weights(a_re, a_im, log_dt, b_re, b_im, c_re, c_im, chunk):
    g, p = a_re.shape
    n = b_re.shape[-1]
    tiles, tw, cw = chunk // SSM_SUB, SSM_SUB * n, chunk * n
    grp = lambda shape: pl.BlockSpec((1,) + shape, lambda i: (i,) + (0,) * len(shape))
    return pl.pallas_call(
        functools.partial(_s5_param_kernel, chunk=chunk),
        grid=(g,),
        in_specs=[grp((1, p)), grp((1, p)), grp((1, 1)), grp((n, p)), grp((n, p)), grp((n, p)), grp((n, p))],
        out_specs=[grp((tiles, tw, tw)), grp((2 * p, cw)), grp((cw, 2 * p)), grp((2 * p, SCAN_STEPS))],
        out_shape=[jax.ShapeDtypeStruct((g, tiles, tw, tw), BF16), jax.ShapeDtypeStruct((g, 2 * p, cw), BF16),
                   jax.ShapeDtypeStruct((g, cw, 2 * p), BF16), jax.ShapeDtypeStruct((g, 2 * p, SCAN_STEPS), F32)],
        compiler_params=pltpu.CompilerParams(dimension_semantics=("parallel",)),
        name="s5_params",
    )(a_re[:, None], a_im[:, None], log_dt[:, None, None], jnp.swapaxes(b_re, 1, 2),
      jnp.swapaxes(b_im, 1, 2), c_re, c_im)


def _s5(u_t, seq_chunks, a_re, a_im, log_dt, b_re, b_im, c_re, c_im, d_skip):
    g, cw, nc = u_t.shape
    n = SSM_GROUP
    chunk = cw // n
    assert seq_chunks & (seq_chunks - 1) == 0 and seq_chunks <= 1 << SCAN_STEPS
    tt, wt, vt, sc = _s5_weights(a_re, a_im, log_dt, b_re, b_im, c_re, c_im, chunk)
    dcol = jnp.tile(d_skip.reshape(g, 1, n), (1, chunk, 1)).reshape(g, cw, 1)
    grp = lambda shape: pl.BlockSpec((1,) + shape, lambda i: (i,) + (0,) * len(shape))
    return pl.pallas_call(
        functools.partial(_s5_kernel, seq_chunks=seq_chunks),
        grid=(g,),
        in_specs=[grp((cw, nc)), grp(tt.shape[1:]), grp(wt.shape[1:]), grp(vt.shape[1:]), grp(sc.shape[1:]),
                  grp((cw, 1))],
        out_specs=grp((cw, nc)),
        out_shape=jax.ShapeDtypeStruct((g, cw, nc), BF16),
        compiler_params=pltpu.CompilerParams(dimension_semantics=("parallel",),
                                             vmem_limit_bytes=VMEM_LIMIT),
        name="s5",
    )(u_t, tt, wt, vt, sc, dcol)


def _merge_kernel(x_ref, attn_ref, y_ref, ga_ref, gb_ref, lng_ref, lnb_ref, wglu_ref, bglu_ref,
                  wsb_ref, wab_ref, wo_ref, g1_ref, b1_ref, o_ref):
    h = _layer_norm(x_ref[...], lng_ref[...], lnb_ref[...])
    nc = x_ref.shape[0]
    y = y_ref[...].reshape(-1, nc).T
    yg = (y.astype(F32) * jax.nn.sigmoid(_dot(y, wglu_ref[...]) + bglu_ref[...])).astype(BF16)
    branch_a = _dot(attn_ref[...], wab_ref[...])
    branch_b = _dot(yg, wsb_ref[...])
    merged = ga_ref[...].astype(F32) * branch_a + gb_ref[...].astype(F32) * branch_b
    r = DEEPNORM_ALPHA * h + _dot(merged.astype(BF16), wo_ref[...])
    o_ref[...] = _layer_norm(r, g1_ref[...], b1_ref[...])


def _merge(x2, attn2, y_t, ga2, gb2, ln_g, ln_b, w_glu, b_glu, w_ssm_br, w_attn_br, w_o, ln1_g, ln1_b):
    groups, cw, nc = y_t.shape
    steps = cw // SSM_GROUP
    d = x2.shape[1] // steps
    col = lambda a: pl.BlockSpec((nc, a.shape[1] // steps), lambda i: (0, i))
    vec = lambda v: v[None]
    ws = [w_glu.astype(BF16), vec(b_glu), w_ssm_br.astype(BF16), w_attn_br.astype(BF16),
          w_o.astype(BF16), vec(ln1_g), vec(ln1_b)]
    return pl.pallas_call(
        _merge_kernel,
        grid=(steps,),
        in_specs=[col(x2), col(attn2), pl.BlockSpec((groups, SSM_GROUP, nc), lambda i: (0, i, 0)),
                  col(ga2), col(gb2), _const_spec((1, d)), _const_spec((1, d))]
                 + [_const_spec(w.shape) for w in ws],
        out_specs=col(x2),
        out_shape=jax.ShapeDtypeStruct(x2.shape, F32),
        compiler_params=pltpu.CompilerParams(dimension_semantics=("parallel",),
                                             vmem_limit_bytes=VMEM_LIMIT),
        name="merge",
    )(x2, attn2, y_t, ga2, gb2, vec(ln_g), vec(ln_b), *ws)


def _ffn_kernel(h_ref, p_ref, wup_ref, wdn_ref, g2_ref, b2_ref, wpg_ref, bpg_ref, wple_ref,
                g3_ref, b3_ref, o_ref, *, ff_chunk):
    h = h_ref[...]
    hb = h.astype(BF16)
    ff = jnp.zeros_like(h)
    for c in range(wup_ref.shape[1] // ff_chunk):
        cs = slice(c * ff_chunk, (c + 1) * ff_chunk)
        up = jnp.maximum(_dot(hb, wup_ref[:, cs]), 0.0)
        ff += _dot((up * up).astype(BF16), wdn_ref[cs, :])
    h2 = _layer_norm(DEEPNORM_ALPHA * h + ff, g2_ref[...], b2_ref[...])
    gate = jax.nn.sigmoid(_dot(h2.astype(BF16), wpg_ref[...]) + bpg_ref[...])
    ple = gate * _dot(p_ref[...].astype(BF16), wple_ref[...])
    o_ref[...] = _layer_norm(DEEPNORM_ALPHA * h2 + ple, g3_ref[...], b3_ref[...])


def _ffn(h1, p2, w_up, w_down, ln2_g, ln2_b, w_pg, b_pg, w_ple, ln3_g, ln3_b, tm):
    t, d = h1.shape
    row = lambda w: pl.BlockSpec((tm, w), lambda i: (i, 0))
    vec = lambda v: v[None]
    ws = [w_up.astype(BF16), w_down.astype(BF16), vec(ln2_g), vec(ln2_b), w_pg.astype(BF16), vec(b_pg),
          w_ple.astype(BF16), vec(ln3_g), vec(ln3_b)]
    return pl.pallas_call(
        functools.partial(_ffn_kernel, ff_chunk=1024),
        grid=(t // tm,),
        in_specs=[row(d), row(p2.shape[1])] + [_const_spec(w.shape) for w in ws],
        out_specs=row(d),
        out_shape=jax.ShapeDtypeStruct((t, d), F32),
        compiler_params=pltpu.CompilerParams(dimension_semantics=("parallel",),
                                             vmem_limit_bytes=VMEM_LIMIT),
        name="ffn",
    )(h1, p2, *ws)


def kernel(x, p, positions, ln_in_g, ln_in_b, w_in, b_gate, q_norm_g, w_uq, kv_norm_g, w_ukv, w_attn_br, a_re, a_im, log_dt, b_re, b_im, c_re, c_im, d_skip, w_glu, b_glu, w_ssm_br, w_o, ln1_g, ln1_b, w_up, w_down, ln2_g, ln2_b, w_ple_gate, b_ple_gate, w_ple, ln3_g, ln3_b):
    assert w_in.shape[0] == DEPTH
    b, s, d = x.shape
    t = b * s
    chunk = SSM_CHUNK
    seq_chunks = s // chunk
    nc = b * seq_chunks
    tm = min(512, s)
    tq = min(512, s)
    x2 = x.reshape(nc, chunk * d)
    pos3 = positions.reshape(nc, chunk).T[:, :, None]

    q, k, v, u_t, ga, gb = _in_proj(x2, pos3, ln_in_g, ln_in_b, w_in[0], b_gate[0], q_norm_g[0], w_uq[0],
                                    kv_norm_g[0], w_ukv[0])
    attn = _attention(q.reshape(b, s, -1), k.reshape(b, s, -1), v.reshape(b, s, -1), tq)
    y_t = _s5(u_t, seq_chunks, a_re[0], a_im[0], log_dt[0], b_re[0], b_im[0], c_re[0], c_im[0], d_skip[0])
    h1 = _merge(x2, attn.reshape(nc, -1), y_t, ga, gb, ln_in_g, ln_in_b, w_glu[0], b_glu[0], w_ssm_br[0],
                w_attn_br[0], w_o[0], ln1_g[0], ln1_b[0])
    out = _ffn(h1.reshape(t, d), p[0].reshape(t, -1), w_up[0], w_down[0], ln2_g[0], ln2_b[0], w_ple_gate[0],
               b_ple_gate[0], w_ple[0], ln3_g[0], ln3_b[0], tm)
    return out.reshape(b, s, d)
```

```python
import functools
import math

import jax
import jax.numpy as jnp
from jax import lax
from jax.experimental import pallas as pl
from jax.experimental.pallas import tpu as pltpu

F32 = jnp.float32
BF16 = jnp.bfloat16

MLA_HEADS = 8
QK_NOPE_DIM = 64
QK_ROPE_DIM = 32
V_HEAD_DIM = 64
ROPE_THETA = 10000.0
SSM_GROUP = 16
LN_EPS = 1e-5
RMS_EPS = 1e-6
DEPTH = 1
DEEPNORM_ALPHA = (2.0 * DEPTH) ** 0.25

LANES = 128
MXU_DIM = 256
HEAD_PAD = LANES
SSM_SUB = MXU_DIM // SSM_GROUP
SSM_CHUNK = 64
SCAN_STEPS = 8
CHUNK_BLOCK = LANES
VMEM_LIMIT = 56 * 1024 * 1024


def _const_spec(shape):
    nd = len(shape)
    return pl.BlockSpec(shape, lambda *_: (0,) * nd, pipeline_mode=pl.Buffered(1))


def _layer_norm(x, g, b):
    mu = jnp.mean(x, axis=-1, keepdims=True)
    xc = x - mu
    var = jnp.mean(xc * xc, axis=-1, keepdims=True)
    return xc * lax.rsqrt(var + LN_EPS) * g + b


def _rms_norm(x, g):
    return x * lax.rsqrt(jnp.mean(x * x, axis=-1, keepdims=True) + RMS_EPS) * g


def _dot(a, b):
    return jnp.dot(a, b, preferred_element_type=F32)


def _dot_t(a, b):
    return lax.dot_general(a, b, (((1,), (1,)), ((), ())), preferred_element_type=F32)


def _in_proj_kernel(x_ref, pos_ref, lng_ref, lnb_ref, wa_ref, bg_ref, qg_ref, wq_ref, kvg_ref,
                    wk_ref, wv_ref, wu_ref, invf_ref,
                    q_ref, k_ref, v_ref, ut_ref, ga_ref, gb_ref, u_buf, *, cols, scale):
    ncb, sub, d_in = x_ref.shape
    rows = ncb * sub
    tile = lambda v: v.reshape(ncb, sub, v.shape[-1])
    h = _layer_norm(x_ref[...].reshape(rows, d_in), lng_ref[...], lnb_ref[...]).astype(BF16)

    def proj(name):
        a, b = cols[name]
        return _dot(h, wa_ref[:, a:b])

    ang = pos_ref[...].reshape(rows, 1).astype(F32) * invf_ref[...]
    cos = jnp.cos(ang)
    sin = jnp.sin(ang)
    cos2 = jnp.concatenate([cos, cos], axis=1)
    sin2 = jnp.concatenate([sin, sin], axis=1)

    cqn = _rms_norm(proj("cq"), qg_ref[...]).astype(BF16)
    ckvn = _rms_norm(proj("ckv"), kvg_ref[...]).astype(BF16)
    k_rope = proj("kr") * cos + proj("kr_sw") * sin
    k_rope2 = jnp.concatenate([k_rope, k_rope], axis=1)

    width = wk_ref.shape[1]
    for c in range(width // MXU_DIM):
        sl = slice(c * MXU_DIM, (c + 1) * MXU_DIM)
        q = _dot(cqn, wq_ref[:, sl])
        q_sw = _dot(cqn, wq_ref[:, width + c * MXU_DIM: width + (c + 1) * MXU_DIM])
        q_ref[:, :, sl] = tile(((q * cos2 + q_sw * sin2) * scale).astype(BF16))
        k_ref[:, :, sl] = tile((_dot(ckvn, wk_ref[:, sl]) + k_rope2).astype(BF16))
    v_ref[...] = tile(_dot(ckvn, wv_ref[...]).astype(BF16))
    u = _dot(h, wu_ref[...])
    n = ut_ref.shape[1] // sub
    gl = LANES // n
    for c in range(u_buf.shape[0]):
        u_buf[c] = u[:, c * LANES:(c + 1) * LANES]
    for j in range(sub):
        for c in range(u_buf.shape[0]):
            u_j = u_buf[c, pl.ds(j, ncb, stride=sub), :].T
            ut_ref[c * gl:(c + 1) * gl, j * n:(j + 1) * n, :] = u_j.astype(BF16).reshape(gl, n, ncb)
    d = ga_ref.shape[-1]
    ga_ref[...] = tile(jax.nn.sigmoid(proj("ga") + bg_ref[:, :d]).astype(BF16))
    gb_ref[...] = tile(jax.nn.sigmoid(proj("gb") + bg_ref[:, d:]).astype(BF16))


def _rope_swap(w):
    half = w.shape[-1] // 2
    return jnp.concatenate([-w[..., half:], w[..., :half]], axis=-1)


def _head_slot(nope, rope):
    k, h, _ = rope.shape
    parts = [nope if nope is not None else jnp.zeros((k, h, QK_NOPE_DIM), rope.dtype), rope]
    used = QK_NOPE_DIM + QK_ROPE_DIM
    parts.append(jnp.zeros((k, h, HEAD_PAD - used), rope.dtype))
    return jnp.concatenate(parts, axis=-1).reshape(k, h * HEAD_PAD)


def _in_proj(x4, pos4, ln_g, ln_b, w_in, b_gate, q_norm_g, w_uq, kv_norm_g, w_ukv):
    nc, slabs, sub, d = x4.shape
    ncb = CHUNK_BLOCK
    q_lora = q_norm_g.shape[0]
    kv_lora = kv_norm_g.shape[0]
    ssm_w = w_in.shape[1] - (q_lora + kv_lora + QK_ROPE_DIM + 2 * d)
    heads = MLA_HEADS

    o = 0
    w_cq = w_in[:, o:o + q_lora]; o += q_lora
    w_ckv = w_in[:, o:o + kv_lora]; o += kv_lora
    w_kr = w_in[:, o:o + QK_ROPE_DIM]; o += QK_ROPE_DIM
    w_u = w_in[:, o:o + ssm_w]; o += ssm_w
    w_ga = w_in[:, o:o + d]; o += d
    w_gb = w_in[:, o:o + d]
    kr_slot = _head_slot(None, w_kr[:, None, :])
    kr_sw_slot = _head_slot(None, _rope_swap(w_kr)[:, None, :])
    pieces = [("cq", w_cq), ("ckv", w_ckv), ("kr", kr_slot), ("kr_sw", kr_sw_slot),
              ("ga", w_ga), ("gb", w_gb)]
    cols, o = {}, 0
    for name, w in pieces:
        cols[name] = (o, o + w.shape[1])
        o += w.shape[1]
    w_a = jnp.concatenate([w for _, w in pieces], axis=1).astype(BF16)

    wq = w_uq.reshape(q_lora, heads, QK_NOPE_DIM + QK_ROPE_DIM)
    wq_nope, wq_rope = wq[..., :QK_NOPE_DIM], wq[..., QK_NOPE_DIM:]
    w_q = jnp.concatenate([_head_slot(wq_nope, wq_rope),
                           _head_slot(jnp.zeros_like(wq_nope), _rope_swap(wq_rope))], axis=1).astype(BF16)
    wkv = w_ukv.reshape(kv_lora, heads, QK_NOPE_DIM + V_HEAD_DIM)
    w_k = _head_slot(wkv[..., :QK_NOPE_DIM], jnp.zeros((kv_lora, heads, QK_ROPE_DIM), F32)).astype(BF16)
    w_v = wkv[..., QK_NOPE_DIM:].reshape(kv_lora, heads * V_HEAD_DIM).astype(BF16)
    w_u = w_u.astype(BF16)

    inv_freq = ROPE_THETA ** (-jnp.arange(0, QK_ROPE_DIM, 2, dtype=F32) / QK_ROPE_DIM)
    invf = _head_slot(None, jnp.concatenate([inv_freq, inv_freq])[None, None, :])

    hw = heads * HEAD_PAD
    hv = heads * V_HEAD_DIM
    groups = ssm_w // SSM_GROUP
    slab = lambda w: pl.BlockSpec((ncb, None, sub, w), lambda i, c: (c, i, 0, 0))
    scale = (QK_NOPE_DIM + QK_ROPE_DIM) ** -0.5 * math.log2(math.e)
    nat = lambda w: jax.ShapeDtypeStruct((nc, slabs, sub, w), BF16)
    return pl.pallas_call(
        functools.partial(_in_proj_kernel, cols=cols, scale=scale),
        grid=(slabs, nc // ncb),
        in_specs=[slab(d), slab(1),
                  _const_spec((1, d)), _const_spec((1, d)), _const_spec(w_a.shape),
                  _const_spec((1, 2 * d)), _const_spec((1, q_lora)), _const_spec(w_q.shape),
                  _const_spec((1, kv_lora)), _const_spec(w_k.shape), _const_spec(w_v.shape),
                  _const_spec(w_u.shape), _const_spec((1, HEAD_PAD))],
        out_specs=[slab(hw), slab(hw), slab(hv),
                   pl.BlockSpec((groups, sub * SSM_GROUP, ncb), lambda i, c: (0, i, c)), slab(d), slab(d)],
        out_shape=[nat(hw), nat(hw), nat(hv),
                   jax.ShapeDtypeStruct((groups, slabs * sub * SSM_GROUP, nc), BF16), nat(d), nat(d)],
        scratch_shapes=[pltpu.VMEM((ssm_w // LANES, ncb * sub, LANES), F32)],
        compiler_params=pltpu.CompilerParams(dimension_semantics=("parallel", "parallel"),
                                             vmem_limit_bytes=VMEM_LIMIT),
        name="in_proj",
    )(x4, pos4, ln_g[None], ln_b[None], w_a, b_gate[None], q_norm_g[None], w_q, kv_norm_g[None],
      w_k, w_v, w_u, invf)


def _attn_kernel(q_ref, k_ref, v_ref, o_ref, s_buf, vt_buf, *, tq, tk):
    i = pl.program_id(2)

    @pl.when(i == 0)
    def _():
        for j in range(vt_buf.shape[0]):
            vt_buf[j] = v_ref[0, j * tk:(j + 1) * tk, :].T

    key = lax.broadcasted_iota(jnp.int32, (tk, tq), 0)
    qry = lax.broadcasted_iota(jnp.int32, (tk, tq), 1)
    heads = range(2)
    qs = [q_ref[0, :, hh * HEAD_PAD:(hh + 1) * HEAD_PAD] for hh in heads]

    def scores(j, slot):
        start = pl.multiple_of(j * tk, tk)
        for hh in heads:
            s_buf[slot, hh] = _dot_t(k_ref[0, pl.ds(start, tk), hh * HEAD_PAD:(hh + 1) * HEAD_PAD], qs[hh])

    def update(j, slot, state, mask_offset):
        ps, stats = [], []
        for hh in heads:
            m, l, _ = state[hh]
            s = s_buf[slot, hh]
            if mask_offset is not None:
                s = jnp.where(key + mask_offset <= qry, s, -1e30)
            m_new = jnp.maximum(m, jnp.max(s, axis=0, keepdims=True))
            a = jnp.exp2(m - m_new)
            p = jnp.exp2(s - m_new)
            stats.append((m_new, a, a * l + jnp.sum(p, axis=0, keepdims=True)))
            ps.append(p.astype(BF16))
        out = []
        for hh in heads:
            m_new, a, l_new = stats[hh]
            v_t = vt_buf[j, hh * V_HEAD_DIM:(hh + 1) * V_HEAD_DIM, :]
            out.append((m_new, l_new, a * state[hh][2] + _dot(v_t, ps[hh])))
        return tuple(out)

    def body(pair, state):
        blk = 2 * pair
        scores(blk + 1, 1)
        state = update(blk, 0, state, None)
        scores(blk + 2, 0)
        return update(blk + 1, 1, state, None)

    init = tuple((jnp.full((1, tq), -1e30, F32), jnp.zeros((1, tq), F32),
                  jnp.zeros((V_HEAD_DIM, tq), F32)) for _ in heads)
    scores(0, 0)
    state = lax.fori_loop(0, i, body, init)
    diag = 2 * i
    scores(diag + 1, 1)
    state = update(diag, 0, state, 0)
    state = update(diag + 1, 1, state, tk)
    o_t = jnp.concatenate([acc / l for _, l, acc in state], axis=0)
    o_ref[0] = o_t.T.astype(BF16)


def _attention(q, k, v, tq):
    b, s, _ = q.shape
    pairs = MLA_HEADS // 2
    tk = tq // 2
    nkv = s // tk
    return pl.pallas_call(
        functools.partial(_attn_kernel, tq=tq, tk=tk),
        grid=(b, pairs, s // tq),
        in_specs=[pl.BlockSpec((1, tq, 2 * HEAD_PAD), lambda bi, hp, i: (bi, i, hp)),
                  pl.BlockSpec((1, s, 2 * HEAD_PAD), lambda bi, hp, i: (bi, 0, hp)),
                  pl.BlockSpec((1, s, 2 * V_HEAD_DIM), lambda bi, hp, i: (bi, 0, hp))],
        out_specs=pl.BlockSpec((1, tq, 2 * V_HEAD_DIM), lambda bi, hp, i: (bi, i, hp)),
        out_shape=jax.ShapeDtypeStruct((b, s, MLA_HEADS * V_HEAD_DIM), BF16),
        scratch_shapes=[pltpu.VMEM((2, 2, tk, tq), F32), pltpu.VMEM((nkv, 2 * V_HEAD_DIM, tk), BF16)],
        compiler_params=pltpu.CompilerParams(
            dimension_semantics=("parallel", "parallel", "arbitrary"), vmem_limit_bytes=VMEM_LIMIT),
        name="attention",
    )(q, k, v)


def _s5_kernel(ut_ref, tt_ref, wt_ref, vt_ref, sc_ref, d_ref, y_ref, *, seq_chunks):
    nc = ut_ref.shape[2]
    z = _dot(wt_ref[0], ut_ref[0])
    half = z.shape[0] // 2
    pos = lax.broadcasted_iota(jnp.int32, (1, nc), 1) % seq_chunks

    x = z
    for k in range(seq_chunks.bit_length() - 1):
        sh = 1 << k
        a = sc_ref[0, :, k:k + 1]
        a_re, a_im = a[:half], a[half:]
        xs = jnp.where(pos >= sh, pltpu.roll(x, sh, axis=1), 0.0)
        xs_re, xs_im = xs[:half], xs[half:]
        x = x + jnp.concatenate([a_re * xs_re - a_im * xs_im, a_re * xs_im + a_im * xs_re], axis=0)
    x_in = jnp.where(pos >= 1, pltpu.roll(x, 1, axis=1), 0.0).astype(BF16)

    for t in range(tt_ref.shape[1]):
        ts = slice(t * MXU_DIM, (t + 1) * MXU_DIM)
        acc = _dot(vt_ref[0, ts, :], x_in)
        for s in range(t + 1):
            acc += _dot(tt_ref[0, t - s], ut_ref[0, s * MXU_DIM:(s + 1) * MXU_DIM, :])
        y = acc + d_ref[0, ts, :] * ut_ref[0, ts, :].astype(F32)
        y_ref[0, ts, :] = jax.nn.gelu(y).astype(BF16)


def _s5_param_kernel(are_ref, aim_ref, ldt_ref, btr_ref, bti_ref, cr_ref, ci_ref,
                     tt_ref, wt_ref, vt_ref, sc_ref, *, chunk):
    n, p = cr_ref.shape[1:]
    hi = lax.Precision.HIGHEST
    dt = jnp.exp(ldt_ref[0])
    lam_re = jnp.minimum(are_ref[0], -1e-4)
    lam_im = aim_ref[0]
    twice = lambda x: jnp.concatenate([x, x], axis=1)
    rate, freq = twice(lam_re) * dt, twice(lam_im) * dt
    first = lax.broadcasted_iota(jnp.int32, (1, 2 * p), 1) < p

    def powers(j):
        ang = freq * j
        return jnp.exp(rate * j) * jnp.where(first, jnp.cos(ang), jnp.sin(ang))

    j0 = lax.broadcasted_iota(jnp.int32, (chunk, 1), 0).astype(F32)
    abar = powers(jnp.ones((1, 1), F32))
    k_idx = lax.broadcasted_iota(jnp.int32, (sc_ref.shape[2], 1), 0).astype(F32)
    sc_ref[0] = powers(jnp.exp2(k_idx) * chunk).T
    nr, ni = abar[:, :p] - 1.0, abar[:, p:]
    den = lam_re * lam_re + lam_im * lam_im
    f_re = (nr * lam_re + ni * lam_im) / den
    f_im = (ni * lam_re - nr * lam_im) / den
    bt_re = f_re * btr_ref[0] - f_im * bti_ref[0]
    bt_im = f_re * bti_ref[0] + f_im * btr_ref[0]

    rows = chunk * n
    r_idx = lax.broadcasted_iota(jnp.int32, (rows, chunk), 0)
    j_idx = lax.broadcasted_iota(jnp.int32, (rows, chunk), 1)
    rep = (r_idx // n == j_idx).astype(F32)

    def per_step(pw):
        x = jnp.dot(rep, pw, precision=hi, preferred_element_type=F32)
        return x, pltpu.roll(x, p, axis=1)

    def per_chan(x):
        return jnp.broadcast_to(twice(x)[None], (chunk, n, 2 * p)).reshape(rows, 2 * p)

    x, x_sw = per_step(powers(j0 + 1.0))
    a, b = per_chan(cr_ref[0]) * x, per_chan(ci_ref[0]) * x_sw
    vt_ref[0] = jnp.where(first, a - b, -(a + b)).astype(BF16)
    x, x_sw = per_step(powers(chunk - 1.0 - j0))
    a, b = x * per_chan(bt_re), x_sw * per_chan(bt_im)
    w = jnp.where(first, a - b, a + b)
    wt_ref[0] = w.T.astype(BF16)

    c_cat = jnp.concatenate([cr_ref[0], -ci_ref[0]], axis=1)
    k_rev = lax.dot_general(c_cat, w, (((1,), (1,)), ((), ())), precision=hi, preferred_element_type=F32)
    sub = SSM_SUB
    k_rev = jnp.concatenate([k_rev, jnp.zeros((n, (sub - 1) * n), F32)], axis=1)
    for d in range(chunk // sub):
        for t in range(sub):
            off = (chunk - 1 - sub * d - t) * n
            tt_ref[0, d, t * n:(t + 1) * n, :] = k_rev[:, off:off + sub * n].astype(BF16)


def _s5_weights(a_re, a_im, log_dt, b_re, b_im, c_re, c_im, chunk):
    g, p = a_re.shape
    n = b_re.shape[-1]
    tiles, tw, cw = chunk // SSM_SUB, SSM_SUB * n, chunk * n
    grp = lambda shape: pl.BlockSpec((1,) + shape, lambda i: (i,) + (0,) * len(shape))
    return pl.pallas_call(
        functools.partial(_s5_param_kernel, chunk=chunk),
        grid=(g,),
        in_specs=[grp((1, p)), grp((1, p)), grp((1, 1)), grp((n, p)), grp((n, p)), grp((n, p)), grp((n, p))],
        out_specs=[grp((tiles, tw, tw)), grp((2 * p, cw)), grp((cw, 2 * p)), grp((2 * p, SCAN_STEPS))],
        out_shape=[jax.ShapeDtypeStruct((g, tiles, tw, tw), BF16), jax.ShapeDtypeStruct((g, 2 * p, cw), BF16),
                   jax.ShapeDtypeStruct((g, cw, 2 * p), BF16), jax.ShapeDtypeStruct((g, 2 * p, SCAN_STEPS), F32)],
        compiler_params=pltpu.CompilerParams(dimension_semantics=("parallel",)),
        name="s5_params",
    )(a_re[:, None], a_im[:, None], log_dt[:, None, None], jnp.swapaxes(b_re, 1, 2),
      jnp.swapaxes(b_im, 1, 2), c_re, c_im)


def _s5(u_t, seq_chunks, a_re, a_im, log_dt, b_re, b_im, c_re, c_im, d_skip):
    g, cw, nc = u_t.shape
    n = SSM_GROUP
    chunk = cw // n
    assert seq_chunks & (seq_chunks - 1) == 0 and seq_chunks <= 1 << SCAN_STEPS
    tt, wt, vt, sc = _s5_weights(a_re, a_im, log_dt, b_re, b_im, c_re, c_im, chunk)
    dcol = jnp.tile(d_skip.reshape(g, 1, n), (1, chunk, 1)).reshape(g, cw, 1)
    grp = lambda shape: pl.BlockSpec((1,) + shape, lambda i: (i,) + (0,) * len(shape))
    return pl.pallas_call(
        functools.partial(_s5_kernel, seq_chunks=seq_chunks),
        grid=(g,),
        in_specs=[grp((cw, nc)), grp(tt.shape[1:]), grp(wt.shape[1:]), grp(vt.shape[1:]), grp(sc.shape[1:]),
                  grp((cw, 1))],
        out_specs=grp((cw, nc)),
        out_shape=jax.ShapeDtypeStruct((g, cw, nc), BF16),
        compiler_params=pltpu.CompilerParams(dimension_semantics=("parallel",),
                                             vmem_limit_bytes=VMEM_LIMIT),
        name="s5",
    )(u_t, tt, wt, vt, sc, dcol)


def _merge_kernel(x_ref, attn_ref, y_ref, ga_ref, gb_ref, lng_ref, lnb_ref, wglu_ref, bglu_ref,
                  wsb_ref, wab_ref, wo_ref, g1_ref, b1_ref, o_ref, y_buf):
    ncb, sub, d = x_ref.shape
    rows = ncb * sub
    flat = lambda ref: ref[...].reshape(rows, ref.shape[-1])
    h = _layer_norm(flat(x_ref), lng_ref[...], lnb_ref[...])
    n = y_ref.shape[1] // sub
    gl = LANES // n
    for j in range(sub):
        for c in range(y_buf.shape[0]):
            y_j = y_ref[c * gl:(c + 1) * gl, j * n:(j + 1) * n, :].reshape(LANES, ncb).astype(F32)
            y_buf[c, pl.ds(j, ncb, stride=sub), :] = y_j.T
    y = jnp.concatenate([y_buf[c] for c in range(y_buf.shape[0])], axis=1)
    yg = (y * jax.nn.sigmoid(_dot(y.astype(BF16), wglu_ref[...]) + bglu_ref[...])).astype(BF16)
    branch_a = _dot(flat(attn_ref), wab_ref[...])
    branch_b = _dot(yg, wsb_ref[...])
    merged = flat(ga_ref).astype(F32) * branch_a + flat(gb_ref).astype(F32) * branch_b
    r = DEEPNORM_ALPHA * h + _dot(merged.astype(BF16), wo_ref[...])
    o_ref[...] = _layer_norm(r, g1_ref[...], b1_ref[...]).reshape(ncb, sub, d)


def _merge(x4, attn4, y_t, ga4, gb4, ln_g, ln_b, w_glu, b_glu, w_ssm_br, w_attn_br, w_o, ln1_g, ln1_b):
    nc, slabs, sub, d = x4.shape
    groups, cw, _ = y_t.shape
    ncb = CHUNK_BLOCK
    slab = lambda a: pl.BlockSpec((ncb, None, sub, a.shape[-1]), lambda i, c: (c, i, 0, 0))
    vec = lambda v: v[None]
    ws = [w_glu.astype(BF16), vec(b_glu), w_ssm_br.astype(BF16), w_attn_br.astype(BF16),
          w_o.astype(BF16), vec(ln1_g), vec(ln1_b)]
    return pl.pallas_call(
        _merge_kernel,
        grid=(slabs, nc // ncb),
        in_specs=[slab(x4), slab(attn4), pl.BlockSpec((groups, cw // slabs, ncb), lambda i, c: (0, i, c)),
                  slab(ga4), slab(gb4), _const_spec((1, d)), _const_spec((1, d))]
                 + [_const_spec(w.shape) for w in ws],
        out_specs=slab(x4),
        out_shape=jax.ShapeDtypeStruct(x4.shape, F32),
        scratch_shapes=[pltpu.VMEM((groups * SSM_GROUP // LANES, ncb * sub, LANES), F32)],
        compiler_params=pltpu.CompilerParams(dimension_semantics=("parallel", "parallel"),
                                             vmem_limit_bytes=VMEM_LIMIT),
        name="merge",
    )(x4, attn4, y_t, ga4, gb4, vec(ln_g), vec(ln_b), *ws)


def _ffn_kernel(h_ref, p_ref, wup_ref, wdn_ref, g2_ref, b2_ref, wpg_ref, bpg_ref, wple_ref,
                g3_ref, b3_ref, o_ref, *, ff_chunk):
    h = h_ref[...]
    hb = h.astype(BF16)
    ff = jnp.zeros_like(h)
    for c in range(wup_ref.shape[1] // ff_chunk):
        cs = slice(c * ff_chunk, (c + 1) * ff_chunk)
        up = jnp.maximum(_dot(hb, wup_ref[:, cs]), 0.0)
        ff += _dot((up * up).astype(BF16), wdn_ref[cs, :])
    h2 = _layer_norm(DEEPNORM_ALPHA * h + ff, g2_ref[...], b2_ref[...])
    gate = jax.nn.sigmoid(_dot(h2.astype(BF16), wpg_ref[...]) + bpg_ref[...])
    ple = gate * _dot(p_ref[...].astype(BF16), wple_ref[...])
    o_ref[...] = _layer_norm(DEEPNORM_ALPHA * h2 + ple, g3_ref[...], b3_ref[...])


def _ffn(h1, p2, w_up, w_down, ln2_g, ln2_b, w_pg, b_pg, w_ple, ln3_g, ln3_b, tm):
    t, d = h1.shape
    row = lambda w: pl.BlockSpec((tm, w), lambda i: (i, 0))
    vec = lambda v: v[None]
    ws = [w_up.astype(BF16), w_down.astype(BF16), vec(ln2_g), vec(ln2_b), w_pg.astype(BF16), vec(b_pg),
          w_ple.astype(BF16), vec(ln3_g), vec(ln3_b)]
    return pl.pallas_call(
        functools.partial(_ffn_kernel, ff_chunk=1024),
        grid=(t // tm,),
        in_specs=[row(d), row(p2.shape[1])] + [_const_spec(w.shape) for w in ws],
        out_specs=row(d),
        out_shape=jax.ShapeDtypeStruct((t, d), F32),
        compiler_params=pltpu.CompilerParams(dimension_semantics=("parallel",),
                                             vmem_limit_bytes=VMEM_LIMIT),
        name="ffn",
    )(h1, p2, *ws)


def kernel(x, p, positions, ln_in_g, ln_in_b, w_in, b_gate, q_norm_g, w_uq, kv_norm_g, w_ukv, w_attn_br, a_re, a_im, log_dt, b_re, b_im, c_re, c_im, d_skip, w_glu, b_glu, w_ssm_br, w_o, ln1_g, ln1_b, w_up, w_down, ln2_g, ln2_b, w_ple_gate, b_ple_gate, w_ple, ln3_g, ln3_b):
    assert w_in.shape[0] == DEPTH
    b, s, d = x.shape
    t = b * s
    chunk = SSM_CHUNK
    seq_chunks = s // chunk
    nc = b * seq_chunks
    tm = min(512, s)
    tq = min(512, s)
    sub = 8
    slabs = chunk // sub
    view = lambda a: a.reshape(nc, slabs, sub, a.shape[-1])
    x4 = view(x)

    q, k, v, u_t, ga, gb = _in_proj(x4, view(positions[..., None]), ln_in_g, ln_in_b, w_in[0], b_gate[0],
                                    q_norm_g[0], w_uq[0], kv_norm_g[0], w_ukv[0])
    attn = _attention(q.reshape(b, s, -1), k.reshape(b, s, -1), v.reshape(b, s, -1), tq)
    y_t = _s5(u_t, seq_chunks, a_re[0], a_im[0], log_dt[0], b_re[0], b_im[0], c_re[0], c_im[0], d_skip[0])
    h1 = _merge(x4, view(attn), y_t, ga, gb, ln_in_g, ln_in_b, w_glu[0], b_glu[0], w_ssm_br[0],
                w_attn_br[0], w_o[0], ln1_g[0], ln1_b[0])
    out = _ffn(h1.reshape(t, d), p[0].reshape(t, -1), w_up[0], w_down[0], ln2_g[0], ln2_b[0], w_ple_gate[0],
               b_ple_gate[0], w_ple[0], ln3_g[0], ln3_b[0], tm)
    return out.reshape(b, s, d)
```

```python
import functools
import math

import jax
import jax.numpy as jnp
from jax import lax
from jax.experimental import pallas as pl
from jax.experimental.pallas import tpu as pltpu

F32 = jnp.float32
BF16 = jnp.bfloat16

MLA_HEADS = 8
QK_NOPE_DIM = 64
QK_ROPE_DIM = 32
V_HEAD_DIM = 64
ROPE_THETA = 10000.0
SSM_GROUP = 16
LN_EPS = 1e-5
RMS_EPS = 1e-6
DEPTH = 1
DEEPNORM_ALPHA = (2.0 * DEPTH) ** 0.25

LANES = 128
MXU_DIM = 256
HEAD_PAD = LANES
BF16_ROWS = 16
V_AUG = V_HEAD_DIM + BF16_ROWS
SSM_SUB = MXU_DIM // SSM_GROUP
SSM_CHUNK = 64
SCAN_STEPS = 8
CHUNK_BLOCK = LANES
VMEM_LIMIT = 56 * 1024 * 1024


def _const_spec(shape):
    nd = len(shape)
    return pl.BlockSpec(shape, lambda *_: (0,) * nd, pipeline_mode=pl.Buffered(1))


def _layer_norm(x, g, b):
    mu = jnp.mean(x, axis=-1, keepdims=True)
    xc = x - mu
    var = jnp.mean(xc * xc, axis=-1, keepdims=True)
    return xc * lax.rsqrt(var + LN_EPS) * g + b


def _rms_norm(x, g):
    return x * lax.rsqrt(jnp.mean(x * x, axis=-1, keepdims=True) + RMS_EPS) * g


def _dot(a, b):
    return jnp.dot(a, b, preferred_element_type=F32)


def _dot_t(a, b):
    return lax.dot_general(a, b, (((1,), (1,)), ((), ())), preferred_element_type=F32)


def _in_proj_kernel(x_ref, pos_ref, lng_ref, lnb_ref, wa_ref, bg_ref, qg_ref, wq_ref, kvg_ref,
                    wk_ref, wv_ref, wu_ref, invf_ref,
                    q_ref, k_ref, v_ref, ut_ref, ga_ref, gb_ref, u_buf, *, cols, scale):
    ncb, sub, d_in = x_ref.shape
    hb = ncb // 2
    rows = hb * sub
    parts = range(2)
    cs = lambda p: slice(p * hb, (p + 1) * hb)
    tile = lambda v: v.reshape(hb, sub, v.shape[-1])
    h = [_layer_norm(x_ref[cs(p)].reshape(rows, d_in), lng_ref[...], lnb_ref[...]).astype(BF16) for p in parts]

    def proj(name):
        a, b = cols[name]
        return [_dot(h[p], wa_ref[:, a:b]) for p in parts]

    cq, ckv, kr = proj("cq"), proj("ckv"), proj("kr")

    one = jnp.ones((rows, QK_NOPE_DIM), F32)
    pad = HEAD_PAD - QK_NOPE_DIM - QK_ROPE_DIM
    cos2, sin2, k_rope2 = [], [], []
    for p in parts:
        ang_t = invf_ref[...] * pos_ref[:, p * rows:(p + 1) * rows].astype(F32)
        cos_f, sin_f = jnp.cos(ang_t).T, jnp.sin(ang_t).T
        cos = jnp.concatenate([one, cos_f, cos_f, one[:, :pad]], axis=1)
        sin = jnp.concatenate([one * 0.0, sin_f, sin_f, one[:, :pad] * 0.0], axis=1)
        cos2.append(jnp.concatenate([cos, cos], axis=1))
        sin2.append(jnp.concatenate([sin, sin], axis=1))
        k_rope = kr[p][:, :HEAD_PAD] * cos + kr[p][:, HEAD_PAD:] * sin
        k_rope2.append(jnp.concatenate([k_rope, k_rope], axis=1))

    cqn = [_rms_norm(cq[p], qg_ref[...]).astype(BF16) for p in parts]
    ckvn = [_rms_norm(ckv[p], kvg_ref[...]).astype(BF16) for p in parts]
    width = wk_ref.shape[1]
    for c in range(width // MXU_DIM):
        sl = slice(c * MXU_DIM, (c + 1) * MXU_DIM)
        q = [_dot(cqn[p], wq_ref[:, sl]) for p in parts]
        q_sw = [_dot(cqn[p], wq_ref[:, width + c * MXU_DIM: width + (c + 1) * MXU_DIM]) for p in parts]
        k = [_dot(ckvn[p], wk_ref[:, sl]) for p in parts]
        for p in parts:
            q_ref[cs(p), :, sl] = tile(((q[p] * cos2[p] + q_sw[p] * sin2[p]) * scale).astype(BF16))
            k_ref[cs(p), :, sl] = tile((k[p] + k_rope2[p]).astype(BF16))
    v = [_dot(ckvn[p], wv_ref[...]) for p in parts]
    u = [_dot(h[p], wu_ref[...]) for p in parts]
    ga, gb = proj("ga"), proj("gb")
    d = ga_ref.shape[-1]
    for p in parts:
        v_ref[cs(p)] = tile(v[p].astype(BF16))
        for c in range(u_buf.shape[0]):
            u_buf[c, p * rows:(p + 1) * rows, :] = u[p][:, c * LANES:(c + 1) * LANES]
        ga_ref[cs(p)] = tile(jax.nn.sigmoid(ga[p] + bg_ref[:, :d]).astype(BF16))
        gb_ref[cs(p)] = tile(jax.nn.sigmoid(gb[p] + bg_ref[:, d:]).astype(BF16))
    n = ut_ref.shape[1] // sub
    gl = LANES // n
    for j in range(sub):
        for c in range(u_buf.shape[0]):
            u_j = u_buf[c, pl.ds(j, ncb, stride=sub), :].T
            ut_ref[c * gl:(c + 1) * gl, j * n:(j + 1) * n, :] = u_j.astype(BF16).reshape(gl, n, ncb)


def _rope_swap(w):
    half = w.shape[-1] // 2
    return jnp.concatenate([-w[..., half:], w[..., :half]], axis=-1)


def _head_slot(nope, rope):
    k, h, _ = rope.shape
    parts = [nope if nope is not None else jnp.zeros((k, h, QK_NOPE_DIM), rope.dtype), rope]
    used = QK_NOPE_DIM + QK_ROPE_DIM
    parts.append(jnp.zeros((k, h, HEAD_PAD - used), rope.dtype))
    return jnp.concatenate(parts, axis=-1).reshape(k, h * HEAD_PAD)


def _in_proj(x4, positions, ln_g, ln_b, w_in, b_gate, q_norm_g, w_uq, kv_norm_g, w_ukv):
    nc, slabs, sub, d = x4.shape
    ncb = CHUNK_BLOCK
    pos4 = positions.reshape(nc // ncb, ncb, slabs, sub).transpose(0, 2, 1, 3).reshape(nc // ncb, slabs, 1, ncb * sub)
    q_lora = q_norm_g.shape[0]
    kv_lora = kv_norm_g.shape[0]
    ssm_w = w_in.shape[1] - (q_lora + kv_lora + QK_ROPE_DIM + 2 * d)
    heads = MLA_HEADS

    o = 0
    w_cq = w_in[:, o:o + q_lora]; o += q_lora
    w_ckv = w_in[:, o:o + kv_lora]; o += kv_lora
    w_kr = w_in[:, o:o + QK_ROPE_DIM]; o += QK_ROPE_DIM
    w_u = w_in[:, o:o + ssm_w]; o += ssm_w
    w_ga = w_in[:, o:o + d]; o += d
    w_gb = w_in[:, o:o + d]
    kr_slot = _head_slot(None, w_kr[:, None, :])
    kr_sw_slot = _head_slot(None, _rope_swap(w_kr)[:, None, :])
    pieces = [("cq", w_cq), ("ckv", w_ckv), ("kr", jnp.concatenate([kr_slot, kr_sw_slot], axis=1)),
              ("ga", w_ga), ("gb", w_gb)]
    cols, o = {}, 0
    for name, w in pieces:
        cols[name] = (o, o + w.shape[1])
        o += w.shape[1]
    w_a = jnp.concatenate([w for _, w in pieces], axis=1).astype(BF16)

    wq = w_uq.reshape(q_lora, heads, QK_NOPE_DIM + QK_ROPE_DIM)
    wq_nope, wq_rope = wq[..., :QK_NOPE_DIM], wq[..., QK_NOPE_DIM:]
    w_q = jnp.concatenate([_head_slot(wq_nope, wq_rope),
                           _head_slot(jnp.zeros_like(wq_nope), _rope_swap(wq_rope))], axis=1).astype(BF16)
    wkv = w_ukv.reshape(kv_lora, heads, QK_NOPE_DIM + V_HEAD_DIM)
    w_k = _head_slot(wkv[..., :QK_NOPE_DIM], jnp.zeros((kv_lora, heads, QK_ROPE_DIM), F32)).astype(BF16)
    w_v = wkv[..., QK_NOPE_DIM:].reshape(kv_lora, heads * V_HEAD_DIM).astype(BF16)
    w_u = w_u.astype(BF16)

    inv_freq = ROPE_THETA ** (-jnp.arange(0, QK_ROPE_DIM, 2, dtype=F32) / QK_ROPE_DIM)
    invf = inv_freq[:, None]

    hw = heads * HEAD_PAD
    hv = heads * V_HEAD_DIM
    groups = ssm_w // SSM_GROUP
    slab = lambda w: pl.BlockSpec((ncb, None, sub, w), lambda i, c: (c, i, 0, 0))
    scale = (QK_NOPE_DIM + QK_ROPE_DIM) ** -0.5 * math.log2(math.e)
    nat = lambda w: jax.ShapeDtypeStruct((nc, slabs, sub, w), BF16)
    return pl.pallas_call(
        functools.partial(_in_proj_kernel, cols=cols, scale=scale),
        grid=(slabs, nc // ncb),
        in_specs=[slab(d), pl.BlockSpec((None, None, 1, ncb * sub), lambda i, c: (c, i, 0, 0)),
                  _const_spec((1, d)), _const_spec((1, d)), _const_spec(w_a.shape),
                  _const_spec((1, 2 * d)), _const_spec((1, q_lora)), _const_spec(w_q.shape),
                  _const_spec((1, kv_lora)), _const_spec(w_k.shape), _const_spec(w_v.shape),
                  _const_spec(w_u.shape), _const_spec(invf.shape)],
        out_specs=[slab(hw), slab(hw), slab(hv),
                   pl.BlockSpec((groups, sub * SSM_GROUP, ncb), lambda i, c: (0, i, c)), slab(d), slab(d)],
        out_shape=[nat(hw), nat(hw), nat(hv),
                   jax.ShapeDtypeStruct((groups, slabs * sub * SSM_GROUP, nc), BF16), nat(d), nat(d)],
        scratch_shapes=[pltpu.VMEM((ssm_w // LANES, ncb * sub, LANES), F32)],
        compiler_params=pltpu.CompilerParams(dimension_semantics=("parallel", "parallel"),
                                             vmem_limit_bytes=VMEM_LIMIT),
        name="in_proj",
    )(x4, pos4, ln_g[None], ln_b[None], w_a, b_gate[None], q_norm_g[None], w_q, kv_norm_g[None],
      w_k, w_v, w_u, invf)


def _attn_kernel(q_ref, k_ref, v_ref, o_ref, s_buf, vt_buf, *, tq, tk):
    i = pl.program_id(2)

    @pl.when(i == 0)
    def _():
        extra = (lax.broadcasted_iota(jnp.int32, (V_AUG - V_HEAD_DIM, tk), 0) == 0).astype(BF16)
        for j in range(vt_buf.shape[0]):
            v_t = v_ref[0, j * tk:(j + 1) * tk, :].T
            for hh in range(2):
                vt_buf[j, hh] = jnp.concatenate([v_t[hh * V_HEAD_DIM:(hh + 1) * V_HEAD_DIM], extra], axis=0)

    key = lax.broadcasted_iota(jnp.int32, (tk, tq), 0)
    qry = lax.broadcasted_iota(jnp.int32, (tk, tq), 1)
    heads = range(2)
    qs = [q_ref[0, :, hh * HEAD_PAD:(hh + 1) * HEAD_PAD] for hh in heads]

    def scores(j, slot):
        start = pl.multiple_of(j * tk, tk)
        for hh in heads:
            s_buf[slot, hh] = _dot_t(k_ref[0, pl.ds(start, tk), hh * HEAD_PAD:(hh + 1) * HEAD_PAD], qs[hh])

    def update(j, slot, state, mask_offset):
        ps, stats = [], []
        for hh in heads:
            m, _ = state[hh]
            s = s_buf[slot, hh]
            if mask_offset is not None:
                s = jnp.where(key + mask_offset <= qry, s, -1e30)
            m_new = jnp.maximum(m, jnp.max(s, axis=0, keepdims=True))
            stats.append((m_new, jnp.exp2(m - m_new)))
            ps.append(jnp.exp2(s - m_new).astype(BF16))
        out = []
        for hh in heads:
            m_new, a = stats[hh]
            out.append((m_new, a * state[hh][1] + _dot(vt_buf[j, hh], ps[hh])))
        return tuple(out)

    def body(pair, state):
        blk = 2 * pair
        scores(blk + 1, 1)
        state = update(blk, 0, state, None)
        scores(blk + 2, 0)
        return update(blk + 1, 1, state, None)

    init = tuple((jnp.full((1, tq), -1e30, F32), jnp.zeros((V_AUG, tq), F32)) for _ in heads)
    scores(0, 0)
    state = lax.fori_loop(0, i // 2, lambda qi, st: body(2 * qi + 1, body(2 * qi, st)), init)
    state = lax.cond(i % 2 == 1, lambda st: body(i - 1, st), lambda st: st, state)
    diag = 2 * i
    scores(diag + 1, 1)
    state = update(diag, 0, state, 0)
    state = update(diag + 1, 1, state, tk)
    o_t = jnp.concatenate([acc[:V_HEAD_DIM] / acc[V_HEAD_DIM:V_HEAD_DIM + 1] for _, acc in state], axis=0)
    o_ref[0] = o_t.T.astype(BF16)


def _attention(q, k, v, tq):
    b, s, _ = q.shape
    pairs = MLA_HEADS // 2
    tk = tq // 2
    nkv = s // tk
    return pl.pallas_call(
        functools.partial(_attn_kernel, tq=tq, tk=tk),
        grid=(b, pairs, s // tq),
        in_specs=[pl.BlockSpec((1, tq, 2 * HEAD_PAD), lambda bi, hp, i: (bi, i, hp)),
                  pl.BlockSpec((1, s, 2 * HEAD_PAD), lambda bi, hp, i: (bi, 0, hp)),
                  pl.BlockSpec((1, s, 2 * V_HEAD_DIM), lambda bi, hp, i: (bi, 0, hp))],
        out_specs=pl.BlockSpec((1, tq, 2 * V_HEAD_DIM), lambda bi, hp, i: (bi, i, hp)),
        out_shape=jax.ShapeDtypeStruct((b, s, MLA_HEADS * V_HEAD_DIM), BF16),
        scratch_shapes=[pltpu.VMEM((2, 2, tk, tq), F32), pltpu.VMEM((nkv, 2, V_AUG, tk), BF16)],
        compiler_params=pltpu.CompilerParams(
            dimension_semantics=("parallel", "parallel", "arbitrary"), vmem_limit_bytes=VMEM_LIMIT),
        name="attention",
    )(q, k, v)


def _s5_kernel(ut_ref, tt_ref, wt_ref, vt_ref, sc_ref, d_ref, y_ref, *, seq_chunks):
    nc = ut_ref.shape[2]
    z = _dot(wt_ref[0], ut_ref[0])
    half = z.shape[0] // 2
    pos = lax.broadcasted_iota(jnp.int32, (1, nc), 1) % seq_chunks

    x = z
    for k in range(seq_chunks.bit_length() - 1):
        sh = 1 << k
        a = sc_ref[0, :, k:k + 1]
        a_re, a_im = a[:half], a[half:]
        xs = jnp.where(pos >= sh, pltpu.roll(x, sh, axis=1), 0.0)
        xs_re, xs_im = xs[:half], xs[half:]
        x = x + jnp.concatenate([a_re * xs_re - a_im * xs_im, a_re * xs_im + a_im * xs_re], axis=0)
    x_in = jnp.where(pos >= 1, pltpu.roll(x, 1, axis=1), 0.0).astype(BF16)

    for t in range(tt_ref.shape[1]):
        ts = slice(t * MXU_DIM, (t + 1) * MXU_DIM)
        acc = _dot(vt_ref[0, ts, :], x_in)
        for s in range(t + 1):
            acc += _dot(tt_ref[0, t - s], ut_ref[0, s * MXU_DIM:(s + 1) * MXU_DIM, :])
        y = acc + d_ref[0, ts, :] * ut_ref[0, ts, :].astype(F32)
        y_ref[0, ts, :] = jax.nn.gelu(y).astype(BF16)


def _s5_param_kernel(are_ref, aim_ref, ldt_ref, btr_ref, bti_ref, cr_ref, ci_ref,
                     tt_ref, wt_ref, vt_ref, sc_ref, *, chunk):
    n, p = cr_ref.shape[1:]
    hi = lax.Precision.HIGHEST
    dt = jnp.exp(ldt_ref[0])
    lam_re = jnp.minimum(are_ref[0], -1e-4)
    lam_im = aim_ref[0]
    twice = lambda x: jnp.concatenate([x, x], axis=1)
    rate, freq = twice(lam_re) * dt, twice(lam_im) * dt
    first = lax.broadcasted_iota(jnp.int32, (1, 2 * p), 1) < p

    def powers(j):
        ang = freq * j
        return jnp.exp(rate * j) * jnp.where(first, jnp.cos(ang), jnp.sin(ang))

    j0 = lax.broadcasted_iota(jnp.int32, (chunk, 1), 0).astype(F32)
    abar = powers(jnp.ones((1, 1), F32))
    k_idx = lax.broadcasted_iota(jnp.int32, (sc_ref.shape[2], 1), 0).astype(F32)
    sc_ref[0] = powers(jnp.exp2(k_idx) * chunk).T
    nr, ni = abar[:, :p] - 1.0, abar[:, p:]
    den = lam_re * lam_re + lam_im * lam_im
    f_re = (nr * lam_re + ni * lam_im) / den
    f_im = (ni * lam_re - nr * lam_im) / den
    bt_re = f_re * btr_ref[0] - f_im * bti_ref[0]
    bt_im = f_re * bti_ref[0] + f_im * btr_ref[0]

    rows = chunk * n

    def per_step(pw):
        x = jnp.broadcast_to(pw[:, None, :], (chunk, n, 2 * p)).reshape(rows, 2 * p)
        return x, pltpu.roll(x, p, axis=1)

    def per_chan(x):
        return jnp.broadcast_to(twice(x)[None], (chunk, n, 2 * p)).reshape(rows, 2 * p)

    x, x_sw = per_step(powers(j0 + 1.0))
    a, b = per_chan(cr_ref[0]) * x, per_chan(ci_ref[0]) * x_sw
    vt_ref[0] = jnp.where(first, a - b, -(a + b)).astype(BF16)
    x, x_sw = per_step(powers(chunk - 1.0 - j0))
    a, b = x * per_chan(bt_re), x_sw * per_chan(bt_im)
    w = jnp.where(first, a - b, a + b)
    wt_ref[0] = w.T.astype(BF16)

    c_cat = jnp.concatenate([cr_ref[0], -ci_ref[0]], axis=1)
    k_rev = lax.dot_general(c_cat, w, (((1,), (1,)), ((), ())), precision=hi, preferred_element_type=F32)
    sub = SSM_SUB
    k_rev = jnp.concatenate([k_rev, jnp.zeros((n, (sub - 1) * n), F32)], axis=1)
    for d in range(chunk // sub):
        for t in range(sub):
            off = (chunk - 1 - sub * d - t) * n
            tt_ref[0, d, t * n:(t + 1) * n, :] = k_rev[:, off:off + sub * n].astype(BF16)


def _s5_weights(a_re, a_im, log_dt, b_re, b_im, c_re, c_im, chunk):
    g, p = a_re.shape
    n = b_re.shape[-1]
    tiles, tw, cw = chunk // SSM_SUB, SSM_SUB * n, chunk * n
    grp = lambda shape: pl.BlockSpec((1,) + shape, lambda i: (i,) + (0,) * len(shape))
    return pl.pallas_call(
        functools.partial(_s5_param_kernel, chunk=chunk),
        grid=(g,),
        in_specs=[grp((1, p)), grp((1, p)), grp((1, 1)), grp((n, p)), grp((n, p)), grp((n, p)), grp((n, p))],
        out_specs=[grp((tiles, tw, tw)), grp((2 * p, cw)), grp((cw, 2 * p)), grp((2 * p, SCAN_STEPS))],
        out_shape=[jax.ShapeDtypeStruct((g, tiles, tw, tw), BF16), jax.ShapeDtypeStruct((g, 2 * p, cw), BF16),
                   jax.ShapeDtypeStruct((g, cw, 2 * p), BF16), jax.ShapeDtypeStruct((g, 2 * p, SCAN_STEPS), F32)],
        compiler_params=pltpu.CompilerParams(dimension_semantics=("parallel",)),
        name="s5_params",
    )(a_re[:, None], a_im[:, None], log_dt[:, None, None], jnp.swapaxes(b_re, 1, 2),
      jnp.swapaxes(b_im, 1, 2), c_re, c_im)


def _s5(u_t, seq_chunks, a_re, a_im, log_dt, b_re, b_im, c_re, c_im, d_skip):
    g, cw, nc = u_t.shape
    n = SSM_GROUP
    chunk = cw // n
    assert seq_chunks & (seq_chunks - 1) == 0 and seq_chunks <= 1 << SCAN_STEPS
    tt, wt, vt, sc = _s5_weights(a_re, a_im, log_dt, b_re, b_im, c_re, c_im, chunk)
    dcol = jnp.tile(d_skip.reshape(g, 1, n), (1, chunk, 1)).reshape(g, cw, 1)
    grp = lambda shape: pl.BlockSpec((1,) + shape, lambda i: (i,) + (0,) * len(shape))
    return pl.pallas_call(
        functools.partial(_s5_kernel, seq_chunks=seq_chunks),
        grid=(g,),
        in_specs=[grp((cw, nc)), grp(tt.shape[1:]), grp(wt.shape[1:]), grp(vt.shape[1:]), grp(sc.shape[1:]),
                  grp((cw, 1))],
        out_specs=grp((cw, nc)),
        out_shape=jax.ShapeDtypeStruct((g, cw, nc), BF16),
        compiler_params=pltpu.CompilerParams(dimension_semantics=("parallel",),
                                             vmem_limit_bytes=VMEM_LIMIT),
        name="s5",
    )(u_t, tt, wt, vt, sc, dcol)


def _merge_kernel(x_ref, attn_ref, y_ref, ga_ref, gb_ref, lng_ref, lnb_ref, wglu_ref, bglu_ref,
                  wsb_ref, wab_ref, wo_ref, g1_ref, b1_ref, o_ref, y_buf):
    ncb, sub, d = x_ref.shape
    n = y_ref.shape[1] // sub
    gl = LANES // n
    for j in range(sub):
        for c in range(y_buf.shape[0]):
            y_j = y_ref[c * gl:(c + 1) * gl, j * n:(j + 1) * n, :].reshape(LANES, ncb).astype(F32)
            y_buf[c, pl.ds(j, ncb, stride=sub), :] = y_j.T
    hb = ncb // 2
    parts = range(2)
    flat = lambda ref, p: ref[p * hb:(p + 1) * hb].reshape(hb * sub, ref.shape[-1])
    ys = [jnp.concatenate([y_buf[c, p * hb * sub:(p + 1) * hb * sub, :] for c in range(y_buf.shape[0])], axis=1)
          for p in parts]
    gate = [_dot(ys[p].astype(BF16), wglu_ref[...]) for p in parts]
    branch_a = [_dot(flat(attn_ref, p), wab_ref[...]) for p in parts]
    yg = [(ys[p] * jax.nn.sigmoid(gate[p] + bglu_ref[...])).astype(BF16) for p in parts]
    branch_b = [_dot(yg[p], wsb_ref[...]) for p in parts]
    merged = [(flat(ga_ref, p).astype(F32) * branch_a[p] + flat(gb_ref, p).astype(F32) * branch_b[p]).astype(BF16)
              for p in parts]
    mixed = [_dot(merged[p], wo_ref[...]) for p in parts]
    for p in parts:
        h = _layer_norm(flat(x_ref, p), lng_ref[...], lnb_ref[...])
        r = DEEPNORM_ALPHA * h + mixed[p]
        o_ref[p * hb:(p + 1) * hb] = _layer_norm(r, g1_ref[...], b1_ref[...]).reshape(hb, sub, d)


def _merge(x4, attn4, y_t, ga4, gb4, ln_g, ln_b, w_glu, b_glu, w_ssm_br, w_attn_br, w_o, ln1_g, ln1_b):
    nc, slabs, sub, d = x4.shape
    groups, cw, _ = y_t.shape
    ncb = CHUNK_BLOCK
    slab = lambda a: pl.BlockSpec((ncb, None, sub, a.shape[-1]), lambda i, c: (c, i, 0, 0))
    vec = lambda v: v[None]
    ws = [w_glu.astype(BF16), vec(b_glu), w_ssm_br.astype(BF16), w_attn_br.astype(BF16),
          w_o.astype(BF16), vec(ln1_g), vec(ln1_b)]
    return pl.pallas_call(
        _merge_kernel,
        grid=(slabs, nc // ncb),
        in_specs=[slab(x4), slab(attn4), pl.BlockSpec((groups, cw // slabs, ncb), lambda i, c: (0, i, c)),
                  slab(ga4), slab(gb4), _const_spec((1, d)), _const_spec((1, d))]
                 + [_const_spec(w.shape) for w in ws],
        out_specs=slab(x4),
        out_shape=jax.ShapeDtypeStruct(x4.shape, F32),
        scratch_shapes=[pltpu.VMEM((groups * SSM_GROUP // LANES, ncb * sub, LANES), F32)],
        compiler_params=pltpu.CompilerParams(dimension_semantics=("parallel", "parallel"),
                                             vmem_limit_bytes=VMEM_LIMIT),
        name="merge",
    )(x4, attn4, y_t, ga4, gb4, vec(ln_g), vec(ln_b), *ws)


def _ffn_kernel(h_ref, p_ref, wup_ref, wdn_ref, g2_ref, b2_ref, wpg_ref, bpg_ref, wple_ref,
                g3_ref, b3_ref, o_ref, *, ff_chunk):
    half = h_ref.shape[0] // 2
    parts = range(2)
    rs = lambda p: slice(p * half, (p + 1) * half)
    h = [h_ref[rs(p), :] for p in parts]
    hb = [h[p].astype(BF16) for p in parts]
    ff = [jnp.zeros_like(h[p]) for p in parts]
    for c in range(wup_ref.shape[1] // ff_chunk):
        cs = slice(c * ff_chunk, (c + 1) * ff_chunk)
        up = [jnp.maximum(_dot(hb[p], wup_ref[:, cs]), 0.0) for p in parts]
        ff = [ff[p] + _dot((up[p] * up[p]).astype(BF16), wdn_ref[cs, :]) for p in parts]
    ple = [_dot(p_ref[rs(p), :].astype(BF16), wple_ref[...]) for p in parts]
    h2 = [_layer_norm(DEEPNORM_ALPHA * h[p] + ff[p], g2_ref[...], b2_ref[...]) for p in parts]
    gate = [_dot(h2[p].astype(BF16), wpg_ref[...]) for p in parts]
    for p in parts:
        r = DEEPNORM_ALPHA * h2[p] + jax.nn.sigmoid(gate[p] + bpg_ref[...]) * ple[p]
        o_ref[rs(p), :] = _layer_norm(r, g3_ref[...], b3_ref[...])


def _ffn(h1, p2, w_up, w_down, ln2_g, ln2_b, w_pg, b_pg, w_ple, ln3_g, ln3_b, tm):
    t, d = h1.shape
    row = lambda w: pl.BlockSpec((tm, w), lambda i: (i, 0))
    vec = lambda v: v[None]
    ws = [w_up.astype(BF16), w_down.astype(BF16), vec(ln2_g), vec(ln2_b), w_pg.astype(BF16), vec(b_pg),
          w_ple.astype(BF16), vec(ln3_g), vec(ln3_b)]
    return pl.pallas_call(
        functools.partial(_ffn_kernel, ff_chunk=1024),
        grid=(t // tm,),
        in_specs=[row(d), row(p2.shape[1])] + [_const_spec(w.shape) for w in ws],
        out_specs=row(d),
        out_shape=jax.ShapeDtypeStruct((t, d), F32),
        compiler_params=pltpu.CompilerParams(dimension_semantics=("parallel",),
                                             vmem_limit_bytes=VMEM_LIMIT),
        name="ffn",
    )(h1, p2, *ws)


def kernel(x, p, positions, ln_in_g, ln_in_b, w_in, b_gate, q_norm_g, w_uq, kv_norm_g, w_ukv, w_attn_br, a_re, a_im, log_dt, b_re, b_im, c_re, c_im, d_skip, w_glu, b_glu, w_ssm_br, w_o, ln1_g, ln1_b, w_up, w_down, ln2_g, ln2_b, w_ple_gate, b_ple_gate, w_ple, ln3_g, ln3_b):
    assert w_in.shape[0] == DEPTH
    b, s, d = x.shape
    t = b * s
    chunk = SSM_CHUNK
    seq_chunks = s // chunk
    nc = b * seq_chunks
    tm = min(512, s)
    tq = min(512, s)
    sub = 8
    slabs = chunk // sub
    view = lambda a: a.reshape(nc, slabs, sub, a.shape[-1])
    x4 = view(x)

    q, k, v, u_t, ga, gb = _in_proj(x4, positions, ln_in_g, ln_in_b, w_in[0], b_gate[0],
                                    q_norm_g[0], w_uq[0], kv_norm_g[0], w_ukv[0])
    attn = _attention(q.reshape(b, s, -1), k.reshape(b, s, -1), v.reshape(b, s, -1), tq)
    y_t = _s5(u_t, seq_chunks, a_re[0], a_im[0], log_dt[0], b_re[0], b_im[0], c_re[0], c_im[0], d_skip[0])
    h1 = _merge(x4, view(attn), y_t, ga, gb, ln_in_g, ln_in_b, w_glu[0], b_glu[0], w_ssm_br[0],
                w_attn_br[0], w_o[0], ln1_g[0], ln1_b[0])
    out = _ffn(h1.reshape(t, d), p[0].reshape(t, -1), w_up[0], w_down[0], ln2_g[0], ln2_b[0], w_ple_gate[0],
               b_ple_gate[0], w_ple[0], ln3_g[0], ln3_b[0], tm)
    return out.reshape(b, s, d)
```

```python
import functools
import math

import jax
import jax.numpy as jnp
from jax import lax
from jax.experimental import pallas as pl
from jax.experimental.pallas import tpu as pltpu

F32 = jnp.float32
BF16 = jnp.bfloat16

MLA_HEADS = 8
QK_NOPE_DIM = 64
QK_ROPE_DIM = 32
V_HEAD_DIM = 64
ROPE_THETA = 10000.0
SSM_GROUP = 16
LN_EPS = 1e-5
RMS_EPS = 1e-6
DEPTH = 1
DEEPNORM_ALPHA = (2.0 * DEPTH) ** 0.25

LANES = 128
MXU_DIM = 256
HEAD_PAD = LANES
BF16_ROWS = 16
V_AUG = V_HEAD_DIM + BF16_ROWS
SSM_SUB = MXU_DIM // SSM_GROUP
SSM_CHUNK = 64
SCAN_STEPS = 8
CHUNK_BLOCK = LANES
VMEM_LIMIT = 56 * 1024 * 1024


def _const_spec(shape):
    nd = len(shape)
    return pl.BlockSpec(shape, lambda *_: (0,) * nd, pipeline_mode=pl.Buffered(1))


def _layer_norm(x, g, b):
    mu = jnp.mean(x, axis=-1, keepdims=True)
    xc = x - mu
    var = jnp.mean(xc * xc, axis=-1, keepdims=True)
    return xc * lax.rsqrt(var + LN_EPS) * g + b


def _rms_norm(x, g):
    return x * lax.rsqrt(jnp.mean(x * x, axis=-1, keepdims=True) + RMS_EPS) * g


def _dot(a, b):
    return jnp.dot(a, b, preferred_element_type=F32)


def _dot_t(a, b):
    return lax.dot_general(a, b, (((1,), (1,)), ((), ())), preferred_element_type=F32)


def _in_proj_kernel(x_ref, pos_ref, lng_ref, lnb_ref, wa_ref, bg_ref, qg_ref, wq_ref, kvg_ref,
                    wk_ref, wv_ref, wu_ref, invf_ref,
                    q_ref, k_ref, v_ref, ut_ref, ga_ref, gb_ref, u_buf, *, cols, scale):
    ncb, sub, d_in = x_ref.shape
    hb = ncb // 2
    rows = hb * sub
    parts = range(2)
    cs = lambda p: slice(p * hb, (p + 1) * hb)
    tile = lambda v: v.reshape(hb, sub, v.shape[-1])
    h = [_layer_norm(x_ref[cs(p)].reshape(rows, d_in), lng_ref[...], lnb_ref[...]).astype(BF16) for p in parts]

    def proj(name):
        a, b = cols[name]
        return [_dot(h[p], wa_ref[:, a:b]) for p in parts]

    cq, ckv, kr = proj("cq"), proj("ckv"), proj("kr")

    one = jnp.ones((rows, QK_NOPE_DIM), F32)
    pad = HEAD_PAD - QK_NOPE_DIM - QK_ROPE_DIM
    half = QK_ROPE_DIM // 2
    zero = one * 0.0
    cos2, sin_lo, sin_hi, k_rope2 = [], [], [], []
    for p in parts:
        ang_t = invf_ref[...] * pos_ref[:, p * rows:(p + 1) * rows].astype(F32)
        cos_f, sin_f = jnp.cos(ang_t).T, jnp.sin(ang_t).T
        cos = jnp.concatenate([one, cos_f, cos_f, one[:, :pad]], axis=1)
        sin = jnp.concatenate([one * 0.0, sin_f, sin_f, one[:, :pad] * 0.0], axis=1)
        cos2.append(jnp.concatenate([cos, cos], axis=1))
        sin_lo.append(jnp.concatenate([zero, -sin_f, zero[:, :half + pad]], axis=1))
        sin_hi.append(jnp.concatenate([zero, zero[:, :half], sin_f, zero[:, :pad]], axis=1))
        k_rope = kr[p][:, :HEAD_PAD] * cos + kr[p][:, HEAD_PAD:] * sin
        k_rope2.append(jnp.concatenate([k_rope, k_rope], axis=1))

    cqn = [_rms_norm(cq[p], qg_ref[...]).astype(BF16) for p in parts]
    ckvn = [_rms_norm(ckv[p], kvg_ref[...]).astype(BF16) for p in parts]
    width = wk_ref.shape[1]
    for c in range(width // MXU_DIM):
        sl = slice(c * MXU_DIM, (c + 1) * MXU_DIM)
        q = [_dot(cqn[p], wq_ref[:, sl]) for p in parts]
        k = [_dot(ckvn[p], wk_ref[:, sl]) for p in parts]
        for p in parts:
            q_rot = jnp.concatenate(
                [pltpu.roll(q[p][:, hh * HEAD_PAD:(hh + 1) * HEAD_PAD], HEAD_PAD - half, axis=1) * sin_lo[p]
                 + pltpu.roll(q[p][:, hh * HEAD_PAD:(hh + 1) * HEAD_PAD], half, axis=1) * sin_hi[p]
                 for hh in range(MXU_DIM // HEAD_PAD)], axis=1)
            q_ref[cs(p), :, sl] = tile(((q[p] * cos2[p] + q_rot) * scale).astype(BF16))
            k_ref[cs(p), :, sl] = tile((k[p] + k_rope2[p]).astype(BF16))
    v = [_dot(ckvn[p], wv_ref[...]) for p in parts]
    u = [_dot(h[p], wu_ref[...]) for p in parts]
    ga, gb = proj("ga"), proj("gb")
    d = ga_ref.shape[-1]
    for p in parts:
        v_ref[cs(p)] = tile(v[p].astype(BF16))
        for c in range(u_buf.shape[0]):
            u_buf[c, p * rows:(p + 1) * rows, :] = u[p][:, c * LANES:(c + 1) * LANES]
        ga_ref[cs(p)] = tile(jax.nn.sigmoid(ga[p] + bg_ref[:, :d]).astype(BF16))
        gb_ref[cs(p)] = tile(jax.nn.sigmoid(gb[p] + bg_ref[:, d:]).astype(BF16))
    n = ut_ref.shape[1] // sub
    gl = LANES // n
    for j in range(sub):
        for c in range(u_buf.shape[0]):
            u_j = u_buf[c, pl.ds(j, ncb, stride=sub), :].T
            ut_ref[c * gl:(c + 1) * gl, j * n:(j + 1) * n, :] = u_j.astype(BF16).reshape(gl, n, ncb)


def _rope_swap(w):
    half = w.shape[-1] // 2
    return jnp.concatenate([-w[..., half:], w[..., :half]], axis=-1)


def _head_slot(nope, rope):
    k, h, _ = rope.shape
    parts = [nope if nope is not None else jnp.zeros((k, h, QK_NOPE_DIM), rope.dtype), rope]
    used = QK_NOPE_DIM + QK_ROPE_DIM
    parts.append(jnp.zeros((k, h, HEAD_PAD - used), rope.dtype))
    return jnp.concatenate(parts, axis=-1).reshape(k, h * HEAD_PAD)


def _in_proj(x4, positions, ln_g, ln_b, w_in, b_gate, q_norm_g, w_uq, kv_norm_g, w_ukv):
    nc, slabs, sub, d = x4.shape
    ncb = CHUNK_BLOCK
    pos4 = positions.reshape(nc // ncb, ncb, slabs, sub).transpose(0, 2, 1, 3).reshape(nc // ncb, slabs, 1, ncb * sub)
    q_lora = q_norm_g.shape[0]
    kv_lora = kv_norm_g.shape[0]
    ssm_w = w_in.shape[1] - (q_lora + kv_lora + QK_ROPE_DIM + 2 * d)
    heads = MLA_HEADS

    o = 0
    w_cq = w_in[:, o:o + q_lora]; o += q_lora
    w_ckv = w_in[:, o:o + kv_lora]; o += kv_lora
    w_kr = w_in[:, o:o + QK_ROPE_DIM]; o += QK_ROPE_DIM
    w_u = w_in[:, o:o + ssm_w]; o += ssm_w
    w_ga = w_in[:, o:o + d]; o += d
    w_gb = w_in[:, o:o + d]
    kr_slot = _head_slot(None, w_kr[:, None, :])
    kr_sw_slot = _head_slot(None, _rope_swap(w_kr)[:, None, :])
    pieces = [("cq", w_cq), ("ckv", w_ckv), ("kr", jnp.concatenate([kr_slot, kr_sw_slot], axis=1)),
              ("ga", w_ga), ("gb", w_gb)]
    cols, o = {}, 0
    for name, w in pieces:
        cols[name] = (o, o + w.shape[1])
        o += w.shape[1]
    w_a = jnp.concatenate([w for _, w in pieces], axis=1).astype(BF16)

    wq = w_uq.reshape(q_lora, heads, QK_NOPE_DIM + QK_ROPE_DIM)
    wq_nope, wq_rope = wq[..., :QK_NOPE_DIM], wq[..., QK_NOPE_DIM:]
    w_q = _head_slot(wq_nope, wq_rope).astype(BF16)
    wkv = w_ukv.reshape(kv_lora, heads, QK_NOPE_DIM + V_HEAD_DIM)
    w_k = _head_slot(wkv[..., :QK_NOPE_DIM], jnp.zeros((kv_lora, heads, QK_ROPE_DIM), F32)).astype(BF16)
    w_v = wkv[..., QK_NOPE_DIM:].reshape(kv_lora, heads * V_HEAD_DIM).astype(BF16)
    w_u = w_u.astype(BF16)

    inv_freq = ROPE_THETA ** (-jnp.arange(0, QK_ROPE_DIM, 2, dtype=F32) / QK_ROPE_DIM)
    invf = inv_freq[:, None]

    hw = heads * HEAD_PAD
    hv = heads * V_HEAD_DIM
    groups = ssm_w // SSM_GROUP
    slab = lambda w: pl.BlockSpec((ncb, None, sub, w), lambda i, c: (c, i, 0, 0))
    scale = (QK_NOPE_DIM + QK_ROPE_DIM) ** -0.5 * math.log2(math.e)
    nat = lambda w: jax.ShapeDtypeStruct((nc, slabs, sub, w), BF16)
    return pl.pallas_call(
        functools.partial(_in_proj_kernel, cols=cols, scale=scale),
        grid=(slabs, nc // ncb),
        in_specs=[slab(d), pl.BlockSpec((None, None, 1, ncb * sub), lambda i, c: (c, i, 0, 0)),
                  _const_spec((1, d)), _const_spec((1, d)), _const_spec(w_a.shape),
                  _const_spec((1, 2 * d)), _const_spec((1, q_lora)), _const_spec(w_q.shape),
                  _const_spec((1, kv_lora)), _const_spec(w_k.shape), _const_spec(w_v.shape),
                  _const_spec(w_u.shape), _const_spec(invf.shape)],
        out_specs=[slab(hw), slab(hw), slab(hv),
                   pl.BlockSpec((groups, sub * SSM_GROUP, ncb), lambda i, c: (0, i, c)), slab(d), slab(d)],
        out_shape=[nat(hw), nat(hw), nat(hv),
                   jax.ShapeDtypeStruct((groups, slabs * sub * SSM_GROUP, nc), BF16), nat(d), nat(d)],
        scratch_shapes=[pltpu.VMEM((ssm_w // LANES, ncb * sub, LANES), F32)],
        compiler_params=pltpu.CompilerParams(dimension_semantics=("parallel", "parallel"),
                                             vmem_limit_bytes=VMEM_LIMIT),
        name="in_proj",
    )(x4, pos4, ln_g[None], ln_b[None], w_a, b_gate[None], q_norm_g[None], w_q, kv_norm_g[None],
      w_k, w_v, w_u, invf)


def _attn_kernel(q_ref, k_ref, v_ref, o_ref, s_buf, vt_buf, qt_buf, *, tq, tk):
    i = pl.program_id(2)

    @pl.when(i == 0)
    def _():
        extra = (lax.broadcasted_iota(jnp.int32, (V_AUG - V_HEAD_DIM, tk), 0) == 0).astype(BF16)
        for j in range(vt_buf.shape[0]):
            v_t = v_ref[0, j * tk:(j + 1) * tk, :].T
            for hh in range(2):
                vt_buf[j, hh] = jnp.concatenate([v_t[hh * V_HEAD_DIM:(hh + 1) * V_HEAD_DIM], extra], axis=0)

    key = lax.broadcasted_iota(jnp.int32, (tk, tq), 0)
    qry = lax.broadcasted_iota(jnp.int32, (tk, tq), 1)
    heads = range(2)
    for hh in heads:
        qt_buf[hh] = q_ref[0, :, hh * HEAD_PAD:(hh + 1) * HEAD_PAD].T

    def scores(j, slot):
        start = pl.multiple_of(j * tk, tk)
        for hh in heads:
            s_buf[slot, hh] = _dot(k_ref[0, pl.ds(start, tk), hh * HEAD_PAD:(hh + 1) * HEAD_PAD], qt_buf[hh])

    def update(j, slot, state, mask_offset):
        ps, stats = [], []
        for hh in heads:
            m, _ = state[hh]
            s = s_buf[slot, hh]
            if mask_offset is not None:
                s = jnp.where(key + mask_offset <= qry, s, -1e30)
            m_new = jnp.maximum(m, jnp.max(s, axis=0, keepdims=True))
            stats.append((m_new, jnp.exp2(m - m_new)))
            ps.append(jnp.exp2(s - m_new).astype(BF16))
        out = []
        for hh in heads:
            m_new, a = stats[hh]
            out.append((m_new, a * state[hh][1] + _dot(vt_buf[j, hh], ps[hh])))
        return tuple(out)

    def body(pair, state):
        blk = 2 * pair
        scores(blk + 1, 1)
        state = update(blk, 0, state, None)
        scores(blk + 2, 0)
        return update(blk + 1, 1, state, None)

    init = tuple((jnp.full((1, tq), -1e30, F32), jnp.zeros((V_AUG, tq), F32)) for _ in heads)
    scores(0, 0)
    state = lax.fori_loop(0, i // 2, lambda qi, st: body(2 * qi + 1, body(2 * qi, st)), init)
    state = lax.cond(i % 2 == 1, lambda st: body(i - 1, st), lambda st: st, state)
    diag = 2 * i
    scores(diag + 1, 1)
    state = update(diag, 0, state, 0)
    state = update(diag + 1, 1, state, tk)
    o_t = jnp.concatenate([acc[:V_HEAD_DIM] / acc[V_HEAD_DIM:V_HEAD_DIM + 1] for _, acc in state], axis=0)
    o_ref[0] = o_t.T.astype(BF16)


def _attention(q, k, v, tq):
    b, s, _ = q.shape
    pairs = MLA_HEADS // 2
    tk = tq // 2
    nkv = s // tk
    return pl.pallas_call(
        functools.partial(_attn_kernel, tq=tq, tk=tk),
        grid=(b, pairs, s // tq),
        in_specs=[pl.BlockSpec((1, tq, 2 * HEAD_PAD), lambda bi, hp, i: (bi, i, hp)),
                  pl.BlockSpec((1, s, 2 * HEAD_PAD), lambda bi, hp, i: (bi, 0, hp)),
                  pl.BlockSpec((1, s, 2 * V_HEAD_DIM), lambda bi, hp, i: (bi, 0, hp))],
        out_specs=pl.BlockSpec((1, tq, 2 * V_HEAD_DIM), lambda bi, hp, i: (bi, i, hp)),
        out_shape=jax.ShapeDtypeStruct((b, s, MLA_HEADS * V_HEAD_DIM), BF16),
        scratch_shapes=[pltpu.VMEM((2, 2, tk, tq), F32), pltpu.VMEM((nkv, 2, V_AUG, tk), BF16),
                        pltpu.VMEM((2, HEAD_PAD, tq), BF16)],
        compiler_params=pltpu.CompilerParams(
            dimension_semantics=("parallel", "parallel", "arbitrary"), vmem_limit_bytes=VMEM_LIMIT),
        name="attention",
    )(q, k, v)


def _gelu_tanh(x):
    k = -2.0 * math.sqrt(2.0 / math.pi) * math.log2(math.e)
    return x / (1.0 + jnp.exp2(x * (k + (k * 0.044715) * (x * x))))


def _s5_kernel(ut_ref, tt_ref, wt_ref, vt_ref, sc_ref, d_ref, y_ref, *, seq_chunks):
    nc = ut_ref.shape[2]
    tiles = range(tt_ref.shape[1])
    tile = lambda t: slice(t * MXU_DIM, (t + 1) * MXU_DIM)
    z = _dot(wt_ref[0], ut_ref[0])
    local = []
    for t in tiles:
        acc = _dot(tt_ref[0, t], ut_ref[0, tile(0), :])
        for s in range(1, t + 1):
            acc += _dot(tt_ref[0, t - s], ut_ref[0, tile(s), :])
        local.append(acc)

    half = z.shape[0] // 2
    pos = lax.broadcasted_iota(jnp.int32, (1, nc), 1) % seq_chunks
    x = z
    for k in range(seq_chunks.bit_length() - 1):
        sh = 1 << k
        a = sc_ref[0, :, k:k + 1]
        a_re, a_im = a[:half], a[half:]
        xs = jnp.where(pos >= sh, pltpu.roll(x, sh, axis=1), 0.0)
        xs_re, xs_im = xs[:half], xs[half:]
        x = x + jnp.concatenate([a_re * xs_re - a_im * xs_im, a_re * xs_im + a_im * xs_re], axis=0)
    x_in = jnp.where(pos >= 1, pltpu.roll(x, 1, axis=1), 0.0).astype(BF16)

    carried = [_dot(vt_ref[0, tile(t), :], x_in) for t in tiles]
    for t in tiles:
        y = local[t] + carried[t] + d_ref[0, tile(t), :] * ut_ref[0, tile(t), :].astype(F32)
        y_ref[0, tile(t), :] = _gelu_tanh(y).astype(BF16)


def _s5_param_kernel(are_ref, aim_ref, ldt_ref, btr_ref, bti_ref, cr_ref, ci_ref,
                     tt_ref, wt_ref, vt_ref, sc_ref, *, chunk):
    n, p = cr_ref.shape[1:]
    hi = lax.Precision.HIGHEST
    dt = jnp.exp(ldt_ref[0])
    lam_re = jnp.minimum(are_ref[0], -1e-4)
    lam_im = aim_ref[0]
    twice = lambda x: jnp.concatenate([x, x], axis=1)
    rate, freq = twice(lam_re) * dt, twice(lam_im) * dt
    first = lax.broadcasted_iota(jnp.int32, (1, 2 * p), 1) < p

    def powers(j):
        ang = freq * j
        return jnp.exp(rate * j) * jnp.where(first, jnp.cos(ang), jnp.sin(ang))

    j0 = lax.broadcasted_iota(jnp.int32, (chunk, 1), 0).astype(F32)
    abar = powers(jnp.ones((1, 1), F32))
    k_idx = lax.broadcasted_iota(jnp.int32, (sc_ref.shape[2], 1), 0).astype(F32)
    sc_ref[0] = powers(jnp.exp2(k_idx) * chunk).T
    nr, ni = abar[:, :p] - 1.0, abar[:, p:]
    den = lam_re * lam_re + lam_im * lam_im
    f_re = (nr * lam_re + ni * lam_im) / den
    f_im = (ni * lam_re - nr * lam_im) / den
    bt_re = f_re * btr_ref[0] - f_im * bti_ref[0]
    bt_im = f_re * bti_ref[0] + f_im * btr_ref[0]

    rows = chunk * n

    def per_step(pw):
        x = jnp.broadcast_to(pw[:, None, :], (chunk, n, 2 * p)).reshape(rows, 2 * p)
        return x, pltpu.roll(x, p, axis=1)

    def per_chan(x):
        return jnp.broadcast_to(twice(x)[None], (chunk, n, 2 * p)).reshape(rows, 2 * p)

    x, x_sw = per_step(powers(j0 + 1.0))
    a, b = per_chan(cr_ref[0]) * x, per_chan(ci_ref[0]) * x_sw
    vt_ref[0] = jnp.where(first, a - b, -(a + b)).astype(BF16)
    x, x_sw = per_step(powers(chunk - 1.0 - j0))
    a, b = x * per_chan(bt_re), x_sw * per_chan(bt_im)
    w = jnp.where(first, a - b, a + b)
    wt_ref[0] = w.T.astype(BF16)

    c_cat = jnp.concatenate([cr_ref[0], -ci_ref[0]], axis=1)
    k_rev = lax.dot_general(c_cat, w, (((1,), (1,)), ((), ())), precision=hi, preferred_element_type=F32)
    sub = SSM_SUB
    k_rev = jnp.concatenate([k_rev, jnp.zeros((n, (sub - 1) * n), F32)], axis=1)
    for d in range(chunk // sub):
        for t in range(sub):
            off = (chunk - 1 - sub * d - t) * n
            tt_ref[0, d, t * n:(t + 1) * n, :] = k_rev[:, off:off + sub * n].astype(BF16)


def _s5_weights(a_re, a_im, log_dt, b_re, b_im, c_re, c_im, chunk):
    g, p = a_re.shape
    n = b_re.shape[-1]
    tiles, tw, cw = chunk // SSM_SUB, SSM_SUB * n, chunk * n
    grp = lambda shape: pl.BlockSpec((1,) + shape, lambda i: (i,) + (0,) * len(shape))
    return pl.pallas_call(
        functools.partial(_s5_param_kernel, chunk=chunk),
        grid=(g,),
        in_specs=[grp((1, p)), grp((1, p)), grp((1, 1)), grp((n, p)), grp((n, p)), grp((n, p)), grp((n, p))],
        out_specs=[grp((tiles, tw, tw)), grp((2 * p, cw)), grp((cw, 2 * p)), grp((2 * p, SCAN_STEPS))],
        out_shape=[jax.ShapeDtypeStruct((g, tiles, tw, tw), BF16), jax.ShapeDtypeStruct((g, 2 * p, cw), BF16),
                   jax.ShapeDtypeStruct((g, cw, 2 * p), BF16), jax.ShapeDtypeStruct((g, 2 * p, SCAN_STEPS), F32)],
        compiler_params=pltpu.CompilerParams(dimension_semantics=("parallel",)),
        name="s5_params",
    )(a_re[:, None], a_im[:, None], log_dt[:, None, None], jnp.swapaxes(b_re, 1, 2),
      jnp.swapaxes(b_im, 1, 2), c_re, c_im)


def _s5(u_t, seq_chunks, a_re, a_im, log_dt, b_re, b_im, c_re, c_im, d_skip):
    g, cw, nc = u_t.shape
    n = SSM_GROUP
    chunk = cw // n
    assert seq_chunks & (seq_chunks - 1) == 0 and seq_chunks <= 1 << SCAN_STEPS
    tt, wt, vt, sc = _s5_weights(a_re, a_im, log_dt, b_re, b_im, c_re, c_im, chunk)
    dcol = jnp.tile(d_skip.reshape(g, 1, n), (1, chunk, 1)).reshape(g, cw, 1)
    grp = lambda shape: pl.BlockSpec((1,) + shape, lambda i: (i,) + (0,) * len(shape))
    return pl.pallas_call(
        functools.partial(_s5_kernel, seq_chunks=seq_chunks),
        grid=(g,),
        in_specs=[grp((cw, nc)), grp(tt.shape[1:]), grp(wt.shape[1:]), grp(vt.shape[1:]), grp(sc.shape[1:]),
                  grp((cw, 1))],
        out_specs=grp((cw, nc)),
        out_shape=jax.ShapeDtypeStruct((g, cw, nc), BF16),
        compiler_params=pltpu.CompilerParams(dimension_semantics=("parallel",),
                                             vmem_limit_bytes=VMEM_LIMIT),
        name="s5",
    )(u_t, tt, wt, vt, sc, dcol)


def _merge_kernel(x_ref, attn_ref, y_ref, ga_ref, gb_ref, lng_ref, lnb_ref, wglu_ref, bglu_ref,
                  wsb_ref, wab_ref, wo_ref, g1_ref, b1_ref, o_ref, y_buf):
    ncb, sub, d = x_ref.shape
    n = y_ref.shape[1] // sub
    gl = LANES // n
    for j in range(sub):
        for c in range(y_buf.shape[0]):
            y_j = y_ref[c * gl:(c + 1) * gl, j * n:(j + 1) * n, :].reshape(LANES, ncb).astype(F32)
            y_buf[c, pl.ds(j, ncb, stride=sub), :] = y_j.T
    hb = ncb // 2
    parts = range(2)
    flat = lambda ref, p: ref[p * hb:(p + 1) * hb].reshape(hb * sub, ref.shape[-1])
    ys = [jnp.concatenate([y_buf[c, p * hb * sub:(p + 1) * hb * sub, :] for c in range(y_buf.shape[0])], axis=1)
          for p in parts]
    gate = [_dot(ys[p].astype(BF16), wglu_ref[...]) for p in parts]
    branch_a = [_dot(flat(attn_ref, p), wab_ref[...]) for p in parts]
    yg = [(ys[p] * jax.nn.sigmoid(gate[p] + bglu_ref[...])).astype(BF16) for p in parts]
    branch_b = [_dot(yg[p], wsb_ref[...]) for p in parts]
    merged = [(flat(ga_ref, p).astype(F32) * branch_a[p] + flat(gb_ref, p).astype(F32) * branch_b[p]).astype(BF16)
              for p in parts]
    mixed = [_dot(merged[p], wo_ref[...]) for p in parts]
    for p in parts:
        h = _layer_norm(flat(x_ref, p), lng_ref[...], lnb_ref[...])
        r = DEEPNORM_ALPHA * h + mixed[p]
        o_ref[p * hb:(p + 1) * hb] = _layer_norm(r, g1_ref[...], b1_ref[...]).reshape(hb, sub, d)


def _merge(x4, attn4, y_t, ga4, gb4, ln_g, ln_b, w_glu, b_glu, w_ssm_br, w_attn_br, w_o, ln1_g, ln1_b):
    nc, slabs, sub, d = x4.shape
    groups, cw, _ = y_t.shape
    ncb = CHUNK_BLOCK
    slab = lambda a: pl.BlockSpec((ncb, None, sub, a.shape[-1]), lambda i, c: (c, i, 0, 0))
    vec = lambda v: v[None]
    ws = [w_glu.astype(BF16), vec(b_glu), w_ssm_br.astype(BF16), w_attn_br.astype(BF16),
          w_o.astype(BF16), vec(ln1_g), vec(ln1_b)]
    return pl.pallas_call(
        _merge_kernel,
        grid=(slabs, nc // ncb),
        in_specs=[slab(x4), slab(attn4), pl.BlockSpec((groups, cw // slabs, ncb), lambda i, c: (0, i, c)),
                  slab(ga4), slab(gb4), _const_spec((1, d)), _const_spec((1, d))]
                 + [_const_spec(w.shape) for w in ws],
        out_specs=slab(x4),
        out_shape=jax.ShapeDtypeStruct(x4.shape, F32),
        scratch_shapes=[pltpu.VMEM((groups * SSM_GROUP // LANES, ncb * sub, LANES), F32)],
        compiler_params=pltpu.CompilerParams(dimension_semantics=("parallel", "parallel"),
                                             vmem_limit_bytes=VMEM_LIMIT),
        name="merge",
    )(x4, attn4, y_t, ga4, gb4, vec(ln_g), vec(ln_b), *ws)


def _ffn_kernel(h_ref, p_ref, wup_ref, wdn_ref, g2_ref, b2_ref, wpg_ref, bpg_ref, wple_ref,
                g3_ref, b3_ref, o_ref, *, ff_chunk):
    half = h_ref.shape[0] // 2
    parts = range(2)
    rs = lambda p: slice(p * half, (p + 1) * half)
    h = [h_ref[rs(p), :] for p in parts]
    hb = [h[p].astype(BF16) for p in parts]
    ff = [jnp.zeros_like(h[p]) for p in parts]
    for c in range(wup_ref.shape[1] // ff_chunk):
        cs = slice(c * ff_chunk, (c + 1) * ff_chunk)
        up = [jnp.maximum(_dot(hb[p], wup_ref[:, cs]), 0.0) for p in parts]
        ff = [ff[p] + _dot((up[p] * up[p]).astype(BF16), wdn_ref[cs, :]) for p in parts]
    ple = [_dot(p_ref[rs(p), :].astype(BF16), wple_ref[...]) for p in parts]
    h2 = [_layer_norm(DEEPNORM_ALPHA * h[p] + ff[p], g2_ref[...], b2_ref[...]) for p in parts]
    gate = [_dot(h2[p].astype(BF16), wpg_ref[...]) for p in parts]
    for p in parts:
        r = DEEPNORM_ALPHA * h2[p] + jax.nn.sigmoid(gate[p] + bpg_ref[...]) * ple[p]
        o_ref[rs(p), :] = _layer_norm(r, g3_ref[...], b3_ref[...])


def _ffn(h1, p2, w_up, w_down, ln2_g, ln2_b, w_pg, b_pg, w_ple, ln3_g, ln3_b, tm):
    t, d = h1.shape
    row = lambda w: pl.BlockSpec((tm, w), lambda i: (i, 0))
    vec = lambda v: v[None]
    ws = [w_up.astype(BF16), w_down.astype(BF16), vec(ln2_g), vec(ln2_b), w_pg.astype(BF16), vec(b_pg),
          w_ple.astype(BF16), vec(ln3_g), vec(ln3_b)]
    return pl.pallas_call(
        functools.partial(_ffn_kernel, ff_chunk=1024),
        grid=(t // tm,),
        in_specs=[row(d), row(p2.shape[1])] + [_const_spec(w.shape) for w in ws],
        out_specs=row(d),
        out_shape=jax.ShapeDtypeStruct((t, d), F32),
        compiler_params=pltpu.CompilerParams(dimension_semantics=("parallel",),
                                             vmem_limit_bytes=VMEM_LIMIT),
        name="ffn",
    )(h1, p2, *ws)


def kernel(x, p, positions, ln_in_g, ln_in_b, w_in, b_gate, q_norm_g, w_uq, kv_norm_g, w_ukv, w_attn_br, a_re, a_im, log_dt, b_re, b_im, c_re, c_im, d_skip, w_glu, b_glu, w_ssm_br, w_o, ln1_g, ln1_b, w_up, w_down, ln2_g, ln2_b, w_ple_gate, b_ple_gate, w_ple, ln3_g, ln3_b):
    assert w_in.shape[0] == DEPTH
    b, s, d = x.shape
    t = b * s
    chunk = SSM_CHUNK
    seq_chunks = s // chunk
    nc = b * seq_chunks
    tm = min(512, s)
    tq = min(512, s)
    sub = 8
    slabs = chunk // sub
    view = lambda a: a.reshape(nc, slabs, sub, a.shape[-1])
    x4 = view(x)

    q, k, v, u_t, ga, gb = _in_proj(x4, positions, ln_in_g, ln_in_b, w_in[0], b_gate[0],
                                    q_norm_g[0], w_uq[0], kv_norm_g[0], w_ukv[0])
    attn = _attention(q.reshape(b, s, -1), k.reshape(b, s, -1), v.reshape(b, s, -1), tq)
    y_t = _s5(u_t, seq_chunks, a_re[0], a_im[0], log_dt[0], b_re[0], b_im[0], c_re[0], c_im[0], d_skip[0])
    h1 = _merge(x4, view(attn), y_t, ga, gb, ln_in_g, ln_in_b, w_glu[0], b_glu[0], w_ssm_br[0],
                w_attn_br[0], w_o[0], ln1_g[0], ln1_b[0])
    out = _ffn(h1.reshape(t, d), p[0].reshape(t, -1), w_up[0], w_down[0], ln2_g[0], ln2_b[0], w_ple_gate[0],
               b_ple_gate[0], w_ple[0], ln3_g[0], ln3_b[0], tm)
    return out.reshape(b, s, d)
```

```python
import functools
import math

import jax
import jax.numpy as jnp
from jax import lax
from jax.experimental import pallas as pl
from jax.experimental.pallas import tpu as pltpu

F32 = jnp.float32
BF16 = jnp.bfloat16

MLA_HEADS = 8
QK_NOPE_DIM = 64
QK_ROPE_DIM = 32
V_HEAD_DIM = 64
ROPE_THETA = 10000.0
SSM_GROUP = 16
LN_EPS = 1e-5
RMS_EPS = 1e-6
DEPTH = 1
DEEPNORM_ALPHA = (2.0 * DEPTH) ** 0.25

LANES = 128
MXU_DIM = 256
HEAD_PAD = LANES
BF16_ROWS = 16
V_AUG = V_HEAD_DIM + BF16_ROWS
SSM_SUB = MXU_DIM // SSM_GROUP
SSM_CHUNK = 64
SCAN_STEPS = 8
CHUNK_BLOCK = LANES
VMEM_LIMIT = 56 * 1024 * 1024


def _const_spec(shape):
    nd = len(shape)
    return pl.BlockSpec(shape, lambda *_: (0,) * nd, pipeline_mode=pl.Buffered(1))


def _layer_norm(x, g, b):
    mu = jnp.mean(x, axis=-1, keepdims=True)
    xc = x - mu
    var = jnp.mean(xc * xc, axis=-1, keepdims=True)
    return xc * lax.rsqrt(var + LN_EPS) * g + b


def _rms_norm(x, g):
    return x * lax.rsqrt(jnp.mean(x * x, axis=-1, keepdims=True) + RMS_EPS) * g


def _dot(a, b):
    return jnp.dot(a, b, preferred_element_type=F32)


def _dot_t(a, b):
    return lax.dot_general(a, b, (((1,), (1,)), ((), ())), preferred_element_type=F32)


def _in_proj_kernel(x_ref, pos_ref, lng_ref, lnb_ref, wa_ref, bg_ref, qg_ref, wq_ref, kvg_ref,
                    wk_ref, wv_ref, wu_ref, invf_ref,
                    q_ref, k_ref, v_ref, ut_ref, ga_ref, gb_ref, u_buf, *, cols, scale):
    ncb, sub, d_in = x_ref.shape
    hb = ncb // 2
    rows = hb * sub
    parts = range(2)
    cs = lambda p: slice(p * hb, (p + 1) * hb)
    tile = lambda v: v.reshape(hb, sub, v.shape[-1])
    h = [_layer_norm(x_ref[cs(p)].reshape(rows, d_in), lng_ref[...], lnb_ref[...]).astype(BF16) for p in parts]

    def proj(name):
        a, b = cols[name]
        return [_dot(h[p], wa_ref[:, a:b]) for p in parts]

    cq, ckv, kr = proj("cq"), proj("ckv"), proj("kr")

    one = jnp.ones((rows, QK_NOPE_DIM), F32)
    pad = HEAD_PAD - QK_NOPE_DIM - QK_ROPE_DIM
    half = QK_ROPE_DIM // 2
    zero = one * 0.0
    cos2, sin_lo, sin_hi, k_rope2 = [], [], [], []
    for p in parts:
        ang_t = invf_ref[...] * pos_ref[:, p * rows:(p + 1) * rows].astype(F32)
        cos_f, sin_f = jnp.cos(ang_t).T, jnp.sin(ang_t).T
        cos = jnp.concatenate([one, cos_f, cos_f, one[:, :pad]], axis=1)
        sin = jnp.concatenate([one * 0.0, sin_f, sin_f, one[:, :pad] * 0.0], axis=1)
        cos2.append(jnp.concatenate([cos, cos], axis=1))
        sin_lo.append(jnp.concatenate([zero, -sin_f, zero[:, :half + pad]], axis=1))
        sin_hi.append(jnp.concatenate([zero, zero[:, :half], sin_f, zero[:, :pad]], axis=1))
        k_rope = kr[p][:, :HEAD_PAD] * cos + kr[p][:, HEAD_PAD:] * sin
        k_rope2.append(jnp.concatenate([k_rope, k_rope], axis=1))

    cqn = [_rms_norm(cq[p], qg_ref[...]).astype(BF16) for p in parts]
    ckvn = [_rms_norm(ckv[p], kvg_ref[...]).astype(BF16) for p in parts]
    width = wk_ref.shape[1]
    for c in range(width // MXU_DIM):
        sl = slice(c * MXU_DIM, (c + 1) * MXU_DIM)
        q = [_dot(cqn[p], wq_ref[:, sl]) for p in parts]
        k = [_dot(ckvn[p], wk_ref[:, sl]) for p in parts]
        for p in parts:
            q_rot = jnp.concatenate(
                [pltpu.roll(q[p][:, hh * HEAD_PAD:(hh + 1) * HEAD_PAD], HEAD_PAD - half, axis=1) * sin_lo[p]
                 + pltpu.roll(q[p][:, hh * HEAD_PAD:(hh + 1) * HEAD_PAD], half, axis=1) * sin_hi[p]
                 for hh in range(MXU_DIM // HEAD_PAD)], axis=1)
            q_ref[cs(p), :, sl] = tile(((q[p] * cos2[p] + q_rot) * scale).astype(BF16))
            k_ref[cs(p), :, sl] = tile((k[p] + k_rope2[p]).astype(BF16))
    v = [_dot(ckvn[p], wv_ref[...]) for p in parts]
    u = [_dot(h[p], wu_ref[...]) for p in parts]
    ga, gb = proj("ga"), proj("gb")
    d = ga_ref.shape[-1]
    for p in parts:
        v_ref[cs(p)] = tile(v[p].astype(BF16))
        for c in range(u_buf.shape[0]):
            u_buf[c, p * rows:(p + 1) * rows, :] = u[p][:, c * LANES:(c + 1) * LANES]
        ga_ref[cs(p)] = tile(jax.nn.sigmoid(ga[p] + bg_ref[:, :d]).astype(BF16))
        gb_ref[cs(p)] = tile(jax.nn.sigmoid(gb[p] + bg_ref[:, d:]).astype(BF16))
    n = ut_ref.shape[1] // sub
    gl = LANES // n
    for j in range(sub):
        for c in range(u_buf.shape[0]):
            u_j = u_buf[c, pl.ds(j, ncb, stride=sub), :].T
            ut_ref[c * gl:(c + 1) * gl, j * n:(j + 1) * n, :] = u_j.astype(BF16).reshape(gl, n, ncb)


def _rope_swap(w):
    half = w.shape[-1] // 2
    return jnp.concatenate([-w[..., half:], w[..., :half]], axis=-1)


def _head_slot(nope, rope):
    k, h, _ = rope.shape
    parts = [nope if nope is not None else jnp.zeros((k, h, QK_NOPE_DIM), rope.dtype), rope]
    used = QK_NOPE_DIM + QK_ROPE_DIM
    parts.append(jnp.zeros((k, h, HEAD_PAD - used), rope.dtype))
    return jnp.concatenate(parts, axis=-1).reshape(k, h * HEAD_PAD)


def _in_proj(x4, positions, ln_g, ln_b, w_in, b_gate, q_norm_g, w_uq, kv_norm_g, w_ukv):
    nc, slabs, sub, d = x4.shape
    ncb = CHUNK_BLOCK
    pos4 = positions.reshape(nc // ncb, ncb, slabs, sub).transpose(0, 2, 1, 3).reshape(nc // ncb, slabs, 1, ncb * sub)
    q_lora = q_norm_g.shape[0]
    kv_lora = kv_norm_g.shape[0]
    ssm_w = w_in.shape[1] - (q_lora + kv_lora + QK_ROPE_DIM + 2 * d)
    heads = MLA_HEADS

    o = 0
    w_cq = w_in[:, o:o + q_lora]; o += q_lora
    w_ckv = w_in[:, o:o + kv_lora]; o += kv_lora
    w_kr = w_in[:, o:o + QK_ROPE_DIM]; o += QK_ROPE_DIM
    w_u = w_in[:, o:o + ssm_w]; o += ssm_w
    w_ga = w_in[:, o:o + d]; o += d
    w_gb = w_in[:, o:o + d]
    kr_slot = _head_slot(None, w_kr[:, None, :])
    kr_sw_slot = _head_slot(None, _rope_swap(w_kr)[:, None, :])
    pieces = [("cq", w_cq), ("ckv", w_ckv), ("kr", jnp.concatenate([kr_slot, kr_sw_slot], axis=1)),
              ("ga", w_ga), ("gb", w_gb)]
    cols, o = {}, 0
    for name, w in pieces:
        cols[name] = (o, o + w.shape[1])
        o += w.shape[1]
    w_a = jnp.concatenate([w for _, w in pieces], axis=1).astype(BF16)

    wq = w_uq.reshape(q_lora, heads, QK_NOPE_DIM + QK_ROPE_DIM)
    wq_nope, wq_rope = wq[..., :QK_NOPE_DIM], wq[..., QK_NOPE_DIM:]
    w_q = _head_slot(wq_nope, wq_rope).astype(BF16)
    wkv = w_ukv.reshape(kv_lora, heads, QK_NOPE_DIM + V_HEAD_DIM)
    w_k = _head_slot(wkv[..., :QK_NOPE_DIM], jnp.zeros((kv_lora, heads, QK_ROPE_DIM), F32)).astype(BF16)
    w_v = wkv[..., QK_NOPE_DIM:].reshape(kv_lora, heads * V_HEAD_DIM).astype(BF16)
    w_u = w_u.astype(BF16)

    inv_freq = ROPE_THETA ** (-jnp.arange(0, QK_ROPE_DIM, 2, dtype=F32) / QK_ROPE_DIM)
    invf = inv_freq[:, None]

    hw = heads * HEAD_PAD
    hv = heads * V_HEAD_DIM
    groups = ssm_w // SSM_GROUP
    slab = lambda w: pl.BlockSpec((ncb, None, sub, w), lambda i, c: (c, i, 0, 0))
    scale = (QK_NOPE_DIM + QK_ROPE_DIM) ** -0.5 * math.log2(math.e)
    nat = lambda w: jax.ShapeDtypeStruct((nc, slabs, sub, w), BF16)
    return pl.pallas_call(
        functools.partial(_in_proj_kernel, cols=cols, scale=scale),
        grid=(slabs, nc // ncb),
        in_specs=[slab(d), pl.BlockSpec((None, None, 1, ncb * sub), lambda i, c: (c, i, 0, 0)),
                  _const_spec((1, d)), _const_spec((1, d)), _const_spec(w_a.shape),
                  _const_spec((1, 2 * d)), _const_spec((1, q_lora)), _const_spec(w_q.shape),
                  _const_spec((1, kv_lora)), _const_spec(w_k.shape), _const_spec(w_v.shape),
                  _const_spec(w_u.shape), _const_spec(invf.shape)],
        out_specs=[slab(hw), slab(hw), slab(hv),
                   pl.BlockSpec((groups, sub * SSM_GROUP, ncb), lambda i, c: (0, i, c)), slab(d), slab(d)],
        out_shape=[nat(hw), nat(hw), nat(hv),
                   jax.ShapeDtypeStruct((groups, slabs * sub * SSM_GROUP, nc), BF16), nat(d), nat(d)],
        scratch_shapes=[pltpu.VMEM((ssm_w // LANES, ncb * sub, LANES), F32)],
        compiler_params=pltpu.CompilerParams(dimension_semantics=("parallel", "parallel"),
                                             vmem_limit_bytes=VMEM_LIMIT),
        name="in_proj",
    )(x4, pos4, ln_g[None], ln_b[None], w_a, b_gate[None], q_norm_g[None], w_q, kv_norm_g[None],
      w_k, w_v, w_u, invf)


def _attn_kernel(q_ref, k_ref, v_ref, o_ref, s_buf, vt_buf, qt_buf, *, tq, tk):
    step = pl.program_id(2)

    @pl.when(step == 0)
    def _():
        extra = (lax.broadcasted_iota(jnp.int32, (V_AUG - V_HEAD_DIM, tk), 0) == 0).astype(BF16)
        for j in range(vt_buf.shape[0]):
            v_t = v_ref[0, j * tk:(j + 1) * tk, :].T
            for hh in range(2):
                vt_buf[j, hh] = jnp.concatenate([v_t[hh * V_HEAD_DIM:(hh + 1) * V_HEAD_DIM], extra], axis=0)

    key = lax.broadcasted_iota(jnp.int32, (tk, tq), 0)
    qry = lax.broadcasted_iota(jnp.int32, (tk, tq), 1)
    heads = range(2)

    class QueryBlock:
        def __init__(self, side):
            self.side, self.i = side, 2 * step + side
            self.rows = slice(side * tq, (side + 1) * tq)

        def setup(self):
            for hh in heads:
                qt_buf[self.side, hh] = q_ref[0, self.rows, hh * HEAD_PAD:(hh + 1) * HEAD_PAD].T

        def scores(self, j, slot):
            start = pl.multiple_of(j * tk, tk)
            for hh in heads:
                s_buf[self.side, slot, hh] = _dot(k_ref[0, pl.ds(start, tk), hh * HEAD_PAD:(hh + 1) * HEAD_PAD],
                                                  qt_buf[self.side, hh])

        def update(self, j, slot, state, mask_offset):
            ps, stats = [], []
            for hh in heads:
                m, _ = state[hh]
                s = s_buf[self.side, slot, hh]
                if mask_offset is not None:
                    s = jnp.where(key + mask_offset <= qry, s, -1e30)
                m_new = jnp.maximum(m, jnp.max(s, axis=0, keepdims=True))
                stats.append((m_new, jnp.exp2(m - m_new)))
                ps.append(jnp.exp2(s - m_new).astype(BF16))
            out = []
            for hh in heads:
                m_new, a = stats[hh]
                out.append((m_new, a * state[hh][1] + _dot(vt_buf[j, hh], ps[hh])))
            return tuple(out)

        def pair(self, p, state):
            blk = 2 * p
            self.scores(blk + 1, 1)
            state = self.update(blk, 0, state, None)
            self.scores(blk + 2, 0)
            return self.update(blk + 1, 1, state, None)

        def unmasked(self):
            i = self.i
            init = tuple((jnp.full((1, tq), -1e30, F32), jnp.zeros((V_AUG, tq), F32)) for _ in heads)
            state = lax.fori_loop(0, i // 2, lambda qi, st: self.pair(2 * qi + 1, self.pair(2 * qi, st)), init)
            return lax.cond(i % 2 == 1, lambda st: self.pair(i - 1, st), lambda st: st, state)

        def finish(self, state):
            o_t = jnp.concatenate([acc[:V_HEAD_DIM] / acc[V_HEAD_DIM:V_HEAD_DIM + 1] for _, acc in state], axis=0)
            o_ref[0, self.rows, :] = o_t.T.astype(BF16)

    first, second = QueryBlock(0), QueryBlock(1)
    first.setup()
    first.scores(0, 0)
    second.setup()
    state = first.unmasked()
    diag = 2 * first.i
    first.scores(diag + 1, 1)
    state = first.update(diag, 0, state, 0)
    second.scores(0, 0)
    first.finish(first.update(diag + 1, 1, state, tk))
    state = second.unmasked()
    diag = 2 * second.i
    second.scores(diag + 1, 1)
    state = second.update(diag, 0, state, 0)
    second.finish(second.update(diag + 1, 1, state, tk))


def _attention(q, k, v, tq):
    b, s, _ = q.shape
    pairs = MLA_HEADS // 2
    tk = tq // 2
    nkv = s // tk
    return pl.pallas_call(
        functools.partial(_attn_kernel, tq=tq, tk=tk),
        grid=(b, pairs, s // (2 * tq)),
        in_specs=[pl.BlockSpec((1, 2 * tq, 2 * HEAD_PAD), lambda bi, hp, i: (bi, i, hp)),
                  pl.BlockSpec((1, s, 2 * HEAD_PAD), lambda bi, hp, i: (bi, 0, hp)),
                  pl.BlockSpec((1, s, 2 * V_HEAD_DIM), lambda bi, hp, i: (bi, 0, hp))],
        out_specs=pl.BlockSpec((1, 2 * tq, 2 * V_HEAD_DIM), lambda bi, hp, i: (bi, i, hp)),
        out_shape=jax.ShapeDtypeStruct((b, s, MLA_HEADS * V_HEAD_DIM), BF16),
        scratch_shapes=[pltpu.VMEM((2, 2, 2, tk, tq), F32), pltpu.VMEM((nkv, 2, V_AUG, tk), BF16),
                        pltpu.VMEM((2, 2, HEAD_PAD, tq), BF16)],
        compiler_params=pltpu.CompilerParams(
            dimension_semantics=("parallel", "parallel", "arbitrary"), vmem_limit_bytes=VMEM_LIMIT),
        name="attention",
    )(q, k, v)


def _gelu_tanh(x):
    k = -2.0 * math.sqrt(2.0 / math.pi) * math.log2(math.e)
    return x / (1.0 + jnp.exp2(x * (k + (k * 0.044715) * (x * x))))


def _s5_kernel(ut_ref, tt_ref, wt_ref, vt_ref, sc_ref, d_ref, y_ref, *, seq_chunks):
    nc = ut_ref.shape[2]
    tiles = range(tt_ref.shape[1])
    tile = lambda t: slice(t * MXU_DIM, (t + 1) * MXU_DIM)
    z = _dot(wt_ref[0], ut_ref[0])
    local = []
    for t in tiles:
        acc = _dot(tt_ref[0, t], ut_ref[0, tile(0), :])
        for s in range(1, t + 1):
            acc += _dot(tt_ref[0, t - s], ut_ref[0, tile(s), :])
        local.append(acc)

    half = z.shape[0] // 2
    pos = lax.broadcasted_iota(jnp.int32, (1, nc), 1) % seq_chunks
    x = z
    for k in range(seq_chunks.bit_length() - 1):
        sh = 1 << k
        a = sc_ref[0, :, k:k + 1]
        a_re, a_im = a[:half], a[half:]
        xs = jnp.where(pos >= sh, pltpu.roll(x, sh, axis=1), 0.0)
        xs_re, xs_im = xs[:half], xs[half:]
        x = x + jnp.concatenate([a_re * xs_re - a_im * xs_im, a_re * xs_im + a_im * xs_re], axis=0)
    x_in = jnp.where(pos >= 1, pltpu.roll(x, 1, axis=1), 0.0).astype(BF16)

    carried = [_dot(vt_ref[0, tile(t), :], x_in) for t in tiles]
    for t in tiles:
        y = local[t] + carried[t] + d_ref[0, tile(t), :] * ut_ref[0, tile(t), :].astype(F32)
        y_ref[0, tile(t), :] = _gelu_tanh(y).astype(BF16)


def _s5_param_kernel(are_ref, aim_ref, ldt_ref, btr_ref, bti_ref, cr_ref, ci_ref,
                     tt_ref, wt_ref, vt_ref, sc_ref, *, chunk):
    n, p = cr_ref.shape[1:]
    hi = lax.Precision.HIGHEST
    dt = jnp.exp(ldt_ref[0])
    lam_re = jnp.minimum(are_ref[0], -1e-4)
    lam_im = aim_ref[0]
    twice = lambda x: jnp.concatenate([x, x], axis=1)
    rate, freq = twice(lam_re) * dt, twice(lam_im) * dt
    first = lax.broadcasted_iota(jnp.int32, (1, 2 * p), 1) < p

    def powers(j):
        ang = freq * j
        return jnp.exp(rate * j) * jnp.where(first, jnp.cos(ang), jnp.sin(ang))

    j0 = lax.broadcasted_iota(jnp.int32, (chunk, 1), 0).astype(F32)
    abar = powers(jnp.ones((1, 1), F32))
    k_idx = lax.broadcasted_iota(jnp.int32, (sc_ref.shape[2], 1), 0).astype(F32)
    sc_ref[0] = powers(jnp.exp2(k_idx) * chunk).T
    nr, ni = abar[:, :p] - 1.0, abar[:, p:]
    den = lam_re * lam_re + lam_im * lam_im
    f_re = (nr * lam_re + ni * lam_im) / den
    f_im = (ni * lam_re - nr * lam_im) / den
    bt_re = f_re * btr_ref[0] - f_im * bti_ref[0]
    bt_im = f_re * bti_ref[0] + f_im * btr_ref[0]

    rows = chunk * n

    def per_step(pw):
        x = jnp.broadcast_to(pw[:, None, :], (chunk, n, 2 * p)).reshape(rows, 2 * p)
        return x, pltpu.roll(x, p, axis=1)

    def per_chan(x):
        return jnp.broadcast_to(twice(x)[None], (chunk, n, 2 * p)).reshape(rows, 2 * p)

    x, x_sw = per_step(powers(j0 + 1.0))
    a, b = per_chan(cr_ref[0]) * x, per_chan(ci_ref[0]) * x_sw
    vt_ref[0] = jnp.where(first, a - b, -(a + b)).astype(BF16)
    x, x_sw = per_step(powers(chunk - 1.0 - j0))
    a, b = x * per_chan(bt_re), x_sw * per_chan(bt_im)
    w = jnp.where(first, a - b, a + b)
    wt_ref[0] = w.T.astype(BF16)

    c_cat = jnp.concatenate([cr_ref[0], -ci_ref[0]], axis=1)
    k_rev = lax.dot_general(c_cat, w, (((1,), (1,)), ((), ())), precision=hi, preferred_element_type=F32)
    sub = SSM_SUB
    k_rev = jnp.concatenate([k_rev, jnp.zeros((n, (sub - 1) * n), F32)], axis=1)
    for d in range(chunk // sub):
        for t in range(sub):
            off = (chunk - 1 - sub * d - t) * n
            tt_ref[0, d, t * n:(t + 1) * n, :] = k_rev[:, off:off + sub * n].astype(BF16)


def _s5_weights(a_re, a_im, log_dt, b_re, b_im, c_re, c_im, chunk):
    g, p = a_re.shape
    n = b_re.shape[-1]
    tiles, tw, cw = chunk // SSM_SUB, SSM_SUB * n, chunk * n
    grp = lambda shape: pl.BlockSpec((1,) + shape, lambda i: (i,) + (0,) * len(shape))
    return pl.pallas_call(
        functools.partial(_s5_param_kernel, chunk=chunk),
        grid=(g,),
        in_specs=[grp((1, p)), grp((1, p)), grp((1, 1)), grp((n, p)), grp((n, p)), grp((n, p)), grp((n, p))],
        out_specs=[grp((tiles, tw, tw)), grp((2 * p, cw)), grp((cw, 2 * p)), grp((2 * p, SCAN_STEPS))],
        out_shape=[jax.ShapeDtypeStruct((g, tiles, tw, tw), BF16), jax.ShapeDtypeStruct((g, 2 * p, cw), BF16),
                   jax.ShapeDtypeStruct((g, cw, 2 * p), BF16), jax.ShapeDtypeStruct((g, 2 * p, SCAN_STEPS), F32)],
        compiler_params=pltpu.CompilerParams(dimension_semantics=("parallel",)),
        name="s5_params",
    )(a_re[:, None], a_im[:, None], log_dt[:, None, None], jnp.swapaxes(b_re, 1, 2),
      jnp.swapaxes(b_im, 1, 2), c_re, c_im)


def _s5(u_t, seq_chunks, a_re, a_im, log_dt, b_re, b_im, c_re, c_im, d_skip):
    g, cw, nc = u_t.shape
    n = SSM_GROUP
    chunk = cw // n
    assert seq_chunks & (seq_chunks - 1) == 0 and seq_chunks <= 1 << SCAN_STEPS
    tt, wt, vt, sc = _s5_weights(a_re, a_im, log_dt, b_re, b_im, c_re, c_im, chunk)
    dcol = jnp.tile(d_skip.reshape(g, 1, n), (1, chunk, 1)).reshape(g, cw, 1)
    grp = lambda shape: pl.BlockSpec((1,) + shape, lambda i: (i,) + (0,) * len(shape))
    return pl.pallas_call(
        functools.partial(_s5_kernel, seq_chunks=seq_chunks),
        grid=(g,),
        in_specs=[grp((cw, nc)), grp(tt.shape[1:]), grp(wt.shape[1:]), grp(vt.shape[1:]), grp(sc.shape[1:]),
                  grp((cw, 1))],
        out_specs=grp((cw, nc)),
        out_shape=jax.ShapeDtypeStruct((g, cw, nc), BF16),
        compiler_params=pltpu.CompilerParams(dimension_semantics=("parallel",),
                                             vmem_limit_bytes=VMEM_LIMIT),
        name="s5",
    )(u_t, tt, wt, vt, sc, dcol)


def _merge_kernel(x_ref, attn_ref, y_ref, ga_ref, gb_ref, lng_ref, lnb_ref, wglu_ref, bglu_ref,
                  wsb_ref, wab_ref, wo_ref, g1_ref, b1_ref, o_ref, y_buf):
    ncb, sub, d = x_ref.shape
    n = y_ref.shape[1] // sub
    gl = LANES // n
    for j in range(sub):
        for c in range(y_buf.shape[0]):
            y_j = y_ref[c * gl:(c + 1) * gl, j * n:(j + 1) * n, :].reshape(LANES, ncb).astype(F32)
            y_buf[c, pl.ds(j, ncb, stride=sub), :] = y_j.T
    hb = ncb // 2
    parts = range(2)
    flat = lambda ref, p: ref[p * hb:(p + 1) * hb].reshape(hb * sub, ref.shape[-1])
    ys = [jnp.concatenate([y_buf[c, p * hb * sub:(p + 1) * hb * sub, :] for c in range(y_buf.shape[0])], axis=1)
          for p in parts]
    gate = [_dot(ys[p].astype(BF16), wglu_ref[...]) for p in parts]
    branch_a = [_dot(flat(attn_ref, p), wab_ref[...]) for p in parts]
    yg = [(ys[p] * jax.nn.sigmoid(gate[p] + bglu_ref[...])).astype(BF16) for p in parts]
    branch_b = [_dot(yg[p], wsb_ref[...]) for p in parts]
    merged = [(flat(ga_ref, p).astype(F32) * branch_a[p] + flat(gb_ref, p).astype(F32) * branch_b[p]).astype(BF16)
              for p in parts]
    mixed = [_dot(merged[p], wo_ref[...]) for p in parts]
    for p in parts:
        h = _layer_norm(flat(x_ref, p), lng_ref[...], lnb_ref[...])
        r = DEEPNORM_ALPHA * h + mixed[p]
        o_ref[p * hb:(p + 1) * hb] = _layer_norm(r, g1_ref[...], b1_ref[...]).reshape(hb, sub, d)


def _merge(x4, attn4, y_t, ga4, gb4, ln_g, ln_b, w_glu, b_glu, w_ssm_br, w_attn_br, w_o, ln1_g, ln1_b):
    nc, slabs, sub, d = x4.shape
    groups, cw, _ = y_t.shape
    ncb = CHUNK_BLOCK
    slab = lambda a: pl.BlockSpec((ncb, None, sub, a.shape[-1]), lambda i, c: (c, i, 0, 0))
    vec = lambda v: v[None]
    ws = [w_glu.astype(BF16), vec(b_glu), w_ssm_br.astype(BF16), w_attn_br.astype(BF16),
          w_o.astype(BF16), vec(ln1_g), vec(ln1_b)]
    return pl.pallas_call(
        _merge_kernel,
        grid=(slabs, nc // ncb),
        in_specs=[slab(x4), slab(attn4), pl.BlockSpec((groups, cw // slabs, ncb), lambda i, c: (0, i, c)),
                  slab(ga4), slab(gb4), _const_spec((1, d)), _const_spec((1, d))]
                 + [_const_spec(w.shape) for w in ws],
        out_specs=slab(x4),
        out_shape=jax.ShapeDtypeStruct(x4.shape, F32),
        scratch_shapes=[pltpu.VMEM((groups * SSM_GROUP // LANES, ncb * sub, LANES), F32)],
        compiler_params=pltpu.CompilerParams(dimension_semantics=("parallel", "parallel"),
                                             vmem_limit_bytes=VMEM_LIMIT),
        name="merge",
    )(x4, attn4, y_t, ga4, gb4, vec(ln_g), vec(ln_b), *ws)


def _ffn_kernel(h_ref, p_ref, wup_ref, wdn_ref, g2_ref, b2_ref, wpg_ref, bpg_ref, wple_ref,
                g3_ref, b3_ref, o_ref, *, ff_chunk):
    half = h_ref.shape[0] // 2
    parts = range(2)
    rs = lambda p: slice(p * half, (p + 1) * half)
    h = [h_ref[rs(p), :] for p in parts]
    hb = [h[p].astype(BF16) for p in parts]
    ff = [jnp.zeros_like(h[p]) for p in parts]
    for c in range(wup_ref.shape[1] // ff_chunk):
        cs = slice(c * ff_chunk, (c + 1) * ff_chunk)
        up = [jnp.maximum(_dot(hb[p], wup_ref[:, cs]), 0.0) for p in parts]
        ff = [ff[p] + _dot((up[p] * up[p]).astype(BF16), wdn_ref[cs, :]) for p in parts]
    ple = [_dot(p_ref[rs(p), :].astype(BF16), wple_ref[...]) for p in parts]
    h2 = [_layer_norm(DEEPNORM_ALPHA * h[p] + ff[p], g2_ref[...], b2_ref[...]) for p in parts]
    gate = [_dot(h2[p].astype(BF16), wpg_ref[...]) for p in parts]
    for p in parts:
        r = DEEPNORM_ALPHA * h2[p] + jax.nn.sigmoid(gate[p] + bpg_ref[...]) * ple[p]
        o_ref[rs(p), :] = _layer_norm(r, g3_ref[...], b3_ref[...])


def _ffn(h1, p2, w_up, w_down, ln2_g, ln2_b, w_pg, b_pg, w_ple, ln3_g, ln3_b, tm):
    t, d = h1.shape
    row = lambda w: pl.BlockSpec((tm, w), lambda i: (i, 0))
    vec = lambda v: v[None]
    ws = [w_up.astype(BF16), w_down.astype(BF16), vec(ln2_g), vec(ln2_b), w_pg.astype(BF16), vec(b_pg),
          w_ple.astype(BF16), vec(ln3_g), vec(ln3_b)]
    return pl.pallas_call(
        functools.partial(_ffn_kernel, ff_chunk=1024),
        grid=(t // tm,),
        in_specs=[row(d), row(p2.shape[1])] + [_const_spec(w.shape) for w in ws],
        out_specs=row(d),
        out_shape=jax.ShapeDtypeStruct((t, d), F32),
        compiler_params=pltpu.CompilerParams(dimension_semantics=("parallel",),
                                             vmem_limit_bytes=VMEM_LIMIT),
        name="ffn",
    )(h1, p2, *ws)


def kernel(x, p, positions, ln_in_g, ln_in_b, w_in, b_gate, q_norm_g, w_uq, kv_norm_g, w_ukv, w_attn_br, a_re, a_im, log_dt, b_re, b_im, c_re, c_im, d_skip, w_glu, b_glu, w_ssm_br, w_o, ln1_g, ln1_b, w_up, w_down, ln2_g, ln2_b, w_ple_gate, b_ple_gate, w_ple, ln3_g, ln3_b):
    assert w_in.shape[0] == DEPTH
    b, s, d = x.shape
    t = b * s
    chunk = SSM_CHUNK
    seq_chunks = s // chunk
    nc = b * seq_chunks
    tm = min(512, s)
    tq = min(512, s // 2)
    sub = 8
    slabs = chunk // sub
    view = lambda a: a.reshape(nc, slabs, sub, a.shape[-1])
    x4 = view(x)

    q, k, v, u_t, ga, gb = _in_proj(x4, positions, ln_in_g, ln_in_b, w_in[0], b_gate[0],
                                    q_norm_g[0], w_uq[0], kv_norm_g[0], w_ukv[0])
    attn = _attention(q.reshape(b, s, -1), k.reshape(b, s, -1), v.reshape(b, s, -1), tq)
    y_t = _s5(u_t, seq_chunks, a_re[0], a_im[0], log_dt[0], b_re[0], b_im[0], c_re[0], c_im[0], d_skip[0])
    h1 = _merge(x4, view(attn), y_t, ga, gb, ln_in_g, ln_in_b, w_glu[0], b_glu[0], w_ssm_br[0],
                w_attn_br[0], w_o[0], ln1_g[0], ln1_b[0])
    out = _ffn(h1.reshape(t, d), p[0].reshape(t, -1), w_up[0], w_down[0], ln2_g[0], ln2_b[0], w_ple_gate[0],
               b_ple_gate[0], w_ple[0], ln3_g[0], ln3_b[0], tm)
    return out.reshape(b, s, d)
```

```python
import functools
import math

import jax
import jax.numpy as jnp
from jax import lax
from jax.experimental import pallas as pl
from jax.experimental.pallas import tpu as pltpu

F32 = jnp.float32
BF16 = jnp.bfloat16

MLA_HEADS = 8
QK_NOPE_DIM = 64
QK_ROPE_DIM = 32
V_HEAD_DIM = 64
ROPE_THETA = 10000.0
SSM_GROUP = 16
LN_EPS = 1e-5
RMS_EPS = 1e-6
DEPTH = 1
DEEPNORM_ALPHA = (2.0 * DEPTH) ** 0.25

LANES = 128
MXU_DIM = 256
HEAD_PAD = LANES
BF16_ROWS = 16
V_AUG = V_HEAD_DIM + BF16_ROWS
SSM_SUB = MXU_DIM // SSM_GROUP
SSM_CHUNK = 64
SCAN_STEPS = 8
CHUNK_BLOCK = LANES
VMEM_LIMIT = 56 * 1024 * 1024


def _const_spec(shape):
    nd = len(shape)
    return pl.BlockSpec(shape, lambda *_: (0,) * nd, pipeline_mode=pl.Buffered(1))


def _layer_norm(x, g, b):
    mu = jnp.mean(x, axis=-1, keepdims=True)
    xc = x - mu
    var = jnp.mean(xc * xc, axis=-1, keepdims=True)
    return xc * lax.rsqrt(var + LN_EPS) * g + b


def _rms_norm(x, g):
    return x * lax.rsqrt(jnp.mean(x * x, axis=-1, keepdims=True) + RMS_EPS) * g


def _dot(a, b):
    return jnp.dot(a, b, preferred_element_type=F32)


def _dot_t(a, b):
    return lax.dot_general(a, b, (((1,), (1,)), ((), ())), preferred_element_type=F32)


def _in_proj_kernel(x_ref, pos_ref, lng_ref, lnb_ref, wa_ref, bg_ref, qg_ref, wq_ref, kvg_ref,
                    wk_ref, wv_ref, wu_ref, invf_ref,
                    q_ref, k_ref, v_ref, ut_ref, ga_ref, gb_ref, u_buf, *, cols, scale):
    ncb, sub, d_in = x_ref.shape
    hb = ncb // 2
    rows = hb * sub
    parts = range(2)
    cs = lambda p: slice(p * hb, (p + 1) * hb)
    tile = lambda v: v.reshape(hb, sub, v.shape[-1])
    h = [_layer_norm(x_ref[cs(p)].reshape(rows, d_in), lng_ref[...], lnb_ref[...]).astype(BF16) for p in parts]

    def proj(name):
        a, b = cols[name]
        return [_dot(h[p], wa_ref[:, a:b]) for p in parts]

    cq, ckv, kr = proj("cq"), proj("ckv"), proj("kr")

    one = jnp.ones((rows, QK_NOPE_DIM), F32)
    pad = HEAD_PAD - QK_NOPE_DIM - QK_ROPE_DIM
    half = QK_ROPE_DIM // 2
    zero = one * 0.0
    cos2, sin_lo, sin_hi, k_rope2 = [], [], [], []
    for p in parts:
        ang_t = invf_ref[...] * pos_ref[:, p * rows:(p + 1) * rows].astype(F32)
        cos_f, sin_f = jnp.cos(ang_t).T, jnp.sin(ang_t).T
        cos = jnp.concatenate([one, cos_f, cos_f, one[:, :pad]], axis=1)
        sin = jnp.concatenate([one * 0.0, sin_f, sin_f, one[:, :pad] * 0.0], axis=1)
        cos2.append(jnp.concatenate([cos, cos], axis=1))
        sin_lo.append(jnp.concatenate([zero, -sin_f, zero[:, :half + pad]], axis=1))
        sin_hi.append(jnp.concatenate([zero, zero[:, :half], sin_f, zero[:, :pad]], axis=1))
        k_rope = kr[p][:, :HEAD_PAD] * cos + kr[p][:, HEAD_PAD:] * sin
        k_rope2.append(jnp.concatenate([k_rope, k_rope], axis=1))

    cqn = [_rms_norm(cq[p], qg_ref[...]).astype(BF16) for p in parts]
    ckvn = [_rms_norm(ckv[p], kvg_ref[...]).astype(BF16) for p in parts]
    width = wk_ref.shape[1]
    for c in range(width // MXU_DIM):
        sl = slice(c * MXU_DIM, (c + 1) * MXU_DIM)
        q = [_dot(cqn[p], wq_ref[:, sl]) for p in parts]
        k = [_dot(ckvn[p], wk_ref[:, sl]) for p in parts]
        for p in parts:
            q_rot = jnp.concatenate(
                [pltpu.roll(q[p][:, hh * HEAD_PAD:(hh + 1) * HEAD_PAD], HEAD_PAD - half, axis=1) * sin_lo[p]
                 + pltpu.roll(q[p][:, hh * HEAD_PAD:(hh + 1) * HEAD_PAD], half, axis=1) * sin_hi[p]
                 for hh in range(MXU_DIM // HEAD_PAD)], axis=1)
            q_ref[cs(p), :, sl] = tile(((q[p] * cos2[p] + q_rot) * scale).astype(BF16))
            k_ref[cs(p), :, sl] = tile((k[p] + k_rope2[p]).astype(BF16))
    v = [_dot(ckvn[p], wv_ref[...]) for p in parts]
    u = [_dot(h[p], wu_ref[...]) for p in parts]
    ga, gb = proj("ga"), proj("gb")
    d = ga_ref.shape[-1]
    for p in parts:
        v_ref[cs(p)] = tile(v[p].astype(BF16))
        for c in range(u_buf.shape[0]):
            u_buf[c, p * rows:(p + 1) * rows, :] = u[p][:, c * LANES:(c + 1) * LANES]
        ga_ref[cs(p)] = tile(jax.nn.sigmoid(ga[p] + bg_ref[:, :d]).astype(BF16))
        gb_ref[cs(p)] = tile(jax.nn.sigmoid(gb[p] + bg_ref[:, d:]).astype(BF16))
    n = ut_ref.shape[1] // sub
    gl = LANES // n
    for j in range(sub):
        for c in range(u_buf.shape[0]):
            u_j = u_buf[c, pl.ds(j, ncb, stride=sub), :].T
            ut_ref[c * gl:(c + 1) * gl, j * n:(j + 1) * n, :] = u_j.astype(BF16).reshape(gl, n, ncb)


def _rope_swap(w):
    half = w.shape[-1] // 2
    return jnp.concatenate([-w[..., half:], w[..., :half]], axis=-1)


def _head_slot(nope, rope):
    k, h, _ = rope.shape
    parts = [nope if nope is not None else jnp.zeros((k, h, QK_NOPE_DIM), rope.dtype), rope]
    used = QK_NOPE_DIM + QK_ROPE_DIM
    parts.append(jnp.zeros((k, h, HEAD_PAD - used), rope.dtype))
    return jnp.concatenate(parts, axis=-1).reshape(k, h * HEAD_PAD)


def _in_proj(x4, positions, ln_g, ln_b, w_in, b_gate, q_norm_g, w_uq, kv_norm_g, w_ukv):
    nc, slabs, sub, d = x4.shape
    ncb = CHUNK_BLOCK
    pos4 = positions.reshape(nc // ncb, ncb, slabs, sub).transpose(0, 2, 1, 3).reshape(nc // ncb, slabs, 1, ncb * sub)
    q_lora = q_norm_g.shape[0]
    kv_lora = kv_norm_g.shape[0]
    ssm_w = w_in.shape[1] - (q_lora + kv_lora + QK_ROPE_DIM + 2 * d)
    heads = MLA_HEADS

    o = 0
    w_cq = w_in[:, o:o + q_lora]; o += q_lora
    w_ckv = w_in[:, o:o + kv_lora]; o += kv_lora
    w_kr = w_in[:, o:o + QK_ROPE_DIM]; o += QK_ROPE_DIM
    w_u = w_in[:, o:o + ssm_w]; o += ssm_w
    w_ga = w_in[:, o:o + d]; o += d
    w_gb = w_in[:, o:o + d]
    kr_slot = _head_slot(None, w_kr[:, None, :])
    kr_sw_slot = _head_slot(None, _rope_swap(w_kr)[:, None, :])
    pieces = [("cq", w_cq), ("ckv", w_ckv), ("kr", jnp.concatenate([kr_slot, kr_sw_slot], axis=1)),
              ("ga", w_ga), ("gb", w_gb)]
    cols, o = {}, 0
    for name, w in pieces:
        cols[name] = (o, o + w.shape[1])
        o += w.shape[1]
    w_a = jnp.concatenate([w for _, w in pieces], axis=1).astype(BF16)

    wq = w_uq.reshape(q_lora, heads, QK_NOPE_DIM + QK_ROPE_DIM)
    wq_nope, wq_rope = wq[..., :QK_NOPE_DIM], wq[..., QK_NOPE_DIM:]
    w_q = _head_slot(wq_nope, wq_rope).astype(BF16)
    wkv = w_ukv.reshape(kv_lora, heads, QK_NOPE_DIM + V_HEAD_DIM)
    w_k = _head_slot(wkv[..., :QK_NOPE_DIM], jnp.zeros((kv_lora, heads, QK_ROPE_DIM), F32)).astype(BF16)
    w_v = wkv[..., QK_NOPE_DIM:].reshape(kv_lora, heads * V_HEAD_DIM).astype(BF16)
    w_u = w_u.astype(BF16)

    inv_freq = ROPE_THETA ** (-jnp.arange(0, QK_ROPE_DIM, 2, dtype=F32) / QK_ROPE_DIM)
    invf = inv_freq[:, None]

    hw = heads * HEAD_PAD
    hv = heads * V_HEAD_DIM
    groups = ssm_w // SSM_GROUP
    slab = lambda w: pl.BlockSpec((ncb, None, sub, w), lambda i, c: (c, i, 0, 0))
    scale = (QK_NOPE_DIM + QK_ROPE_DIM) ** -0.5 * math.log2(math.e)
    nat = lambda w: jax.ShapeDtypeStruct((nc, slabs, sub, w), BF16)
    return pl.pallas_call(
        functools.partial(_in_proj_kernel, cols=cols, scale=scale),
        grid=(slabs, nc // ncb),
        in_specs=[slab(d), pl.BlockSpec((None, None, 1, ncb * sub), lambda i, c: (c, i, 0, 0)),
                  _const_spec((1, d)), _const_spec((1, d)), _const_spec(w_a.shape),
                  _const_spec((1, 2 * d)), _const_spec((1, q_lora)), _const_spec(w_q.shape),
                  _const_spec((1, kv_lora)), _const_spec(w_k.shape), _const_spec(w_v.shape),
                  _const_spec(w_u.shape), _const_spec(invf.shape)],
        out_specs=[slab(hw), slab(hw), slab(hv),
                   pl.BlockSpec((groups, sub * SSM_GROUP, ncb), lambda i, c: (0, i, c)), slab(d), slab(d)],
        out_shape=[nat(hw), nat(hw), nat(hv),
                   jax.ShapeDtypeStruct((groups, slabs * sub * SSM_GROUP, nc), BF16), nat(d), nat(d)],
        scratch_shapes=[pltpu.VMEM((ssm_w // LANES, ncb * sub, LANES), F32)],
        compiler_params=pltpu.CompilerParams(dimension_semantics=("parallel", "parallel"),
                                             vmem_limit_bytes=VMEM_LIMIT),
        name="in_proj",
    )(x4, pos4, ln_g[None], ln_b[None], w_a, b_gate[None], q_norm_g[None], w_q, kv_norm_g[None],
      w_k, w_v, w_u, invf)


def _attn_kernel(q_ref, k_ref, v_ref, o_ref, s_buf, vt_buf, qt_buf, *, tq, tk):
    extra = (lax.broadcasted_iota(jnp.int32, (V_AUG - V_HEAD_DIM, tk), 0) == 0).astype(BF16)
    for j in range(vt_buf.shape[0]):
        v_t = v_ref[0, j * tk:(j + 1) * tk, :].T
        for hh in range(2):
            vt_buf[j, hh] = jnp.concatenate([v_t[hh * V_HEAD_DIM:(hh + 1) * V_HEAD_DIM], extra], axis=0)

    key = lax.broadcasted_iota(jnp.int32, (tk, tq), 0)
    qry = lax.broadcasted_iota(jnp.int32, (tk, tq), 1)
    heads = range(2)

    class QueryBlock:
        def __init__(self, i):
            self.side, self.i = i % 2, i
            self.rows = slice(i * tq, (i + 1) * tq)

        def setup(self):
            for hh in heads:
                qt_buf[self.side, hh] = q_ref[0, self.rows, hh * HEAD_PAD:(hh + 1) * HEAD_PAD].T

        def scores(self, j, slot):
            start = pl.multiple_of(j * tk, tk)
            for hh in heads:
                s_buf[self.side, slot, hh] = _dot(k_ref[0, pl.ds(start, tk), hh * HEAD_PAD:(hh + 1) * HEAD_PAD],
                                                  qt_buf[self.side, hh])

        def update(self, j, slot, state, mask_offset):
            ps, stats = [], []
            for hh in heads:
                m, _ = state[hh]
                s = s_buf[self.side, slot, hh]
                if mask_offset is not None:
                    s = jnp.where(key + mask_offset <= qry, s, -1e30)
                m_new = jnp.maximum(m, jnp.max(s, axis=0, keepdims=True))
                stats.append((m_new, jnp.exp2(m - m_new)))
                ps.append(jnp.exp2(s - m_new).astype(BF16))
            out = []
            for hh in heads:
                m_new, a = stats[hh]
                out.append((m_new, a * state[hh][1] + _dot(vt_buf[j, hh], ps[hh])))
            return tuple(out)

        def pair(self, p, state):
            blk = 2 * p
            self.scores(blk + 1, 1)
            state = self.update(blk, 0, state, None)
            self.scores(blk + 2, 0)
            return self.update(blk + 1, 1, state, None)

        def unmasked(self):
            i = self.i
            init = tuple((jnp.full((1, tq), -1e30, F32), jnp.zeros((V_AUG, tq), F32)) for _ in heads)
            state = lax.fori_loop(0, i // 2, lambda qi, st: self.pair(2 * qi + 1, self.pair(2 * qi, st)), init)
            return self.pair(i - 1, state) if i % 2 else state

        def finish(self, state):
            o_t = jnp.concatenate([acc[:V_HEAD_DIM] / acc[V_HEAD_DIM:V_HEAD_DIM + 1] for _, acc in state], axis=0)
            o_ref[0, self.rows, :] = o_t.T.astype(BF16)

    blocks = [QueryBlock(i) for i in range(q_ref.shape[1] // tq)]
    blocks[0].setup()
    blocks[0].scores(0, 0)
    for blk, nxt in zip(blocks, blocks[1:] + [None]):
        if nxt is not None:
            nxt.setup()
        state = blk.unmasked()
        diag = 2 * blk.i
        blk.scores(diag + 1, 1)
        state = blk.update(diag, 0, state, 0)
        if nxt is not None:
            nxt.scores(0, 0)
        blk.finish(blk.update(diag + 1, 1, state, tk))


def _attention(q, k, v, tq):
    b, s, _ = q.shape
    pairs = MLA_HEADS // 2
    tk = tq // 2
    nkv = s // tk
    return pl.pallas_call(
        functools.partial(_attn_kernel, tq=tq, tk=tk),
        grid=(b, pairs),
        in_specs=[pl.BlockSpec((1, s, 2 * HEAD_PAD), lambda bi, hp: (bi, 0, hp)),
                  pl.BlockSpec((1, s, 2 * HEAD_PAD), lambda bi, hp: (bi, 0, hp)),
                  pl.BlockSpec((1, s, 2 * V_HEAD_DIM), lambda bi, hp: (bi, 0, hp))],
        out_specs=pl.BlockSpec((1, s, 2 * V_HEAD_DIM), lambda bi, hp: (bi, 0, hp)),
        out_shape=jax.ShapeDtypeStruct((b, s, MLA_HEADS * V_HEAD_DIM), BF16),
        scratch_shapes=[pltpu.VMEM((2, 2, 2, tk, tq), F32), pltpu.VMEM((nkv, 2, V_AUG, tk), BF16),
                        pltpu.VMEM((2, 2, HEAD_PAD, tq), BF16)],
        compiler_params=pltpu.CompilerParams(
            dimension_semantics=("parallel", "parallel"), vmem_limit_bytes=VMEM_LIMIT),
        name="attention",
    )(q, k, v)


def _gelu_tanh(x):
    k = -2.0 * math.sqrt(2.0 / math.pi) * math.log2(math.e)
    return x / (1.0 + jnp.exp2(x * (k + (k * 0.044715) * (x * x))))


def _s5_kernel(ut_ref, tt_ref, wt_ref, vt_ref, sc_ref, d_ref, y_ref, *, seq_chunks):
    nc = ut_ref.shape[2]
    tiles = range(tt_ref.shape[1])
    tile = lambda t: slice(t * MXU_DIM, (t + 1) * MXU_DIM)
    z = _dot(wt_ref[0], ut_ref[0])
    local = []
    for t in tiles:
        acc = _dot(tt_ref[0, t], ut_ref[0, tile(0), :])
        for s in range(1, t + 1):
            acc += _dot(tt_ref[0, t - s], ut_ref[0, tile(s), :])
        local.append(acc)

    half = z.shape[0] // 2
    pos = lax.broadcasted_iota(jnp.int32, (1, nc), 1) % seq_chunks
    x = z
    for k in range(seq_chunks.bit_length() - 1):
        sh = 1 << k
        a = sc_ref[0, :, k:k + 1]
        a_re, a_im = a[:half], a[half:]
        xs = jnp.where(pos >= sh, pltpu.roll(x, sh, axis=1), 0.0)
        xs_re, xs_im = xs[:half], xs[half:]
        x = x + jnp.concatenate([a_re * xs_re - a_im * xs_im, a_re * xs_im + a_im * xs_re], axis=0)
    x_in = jnp.where(pos >= 1, pltpu.roll(x, 1, axis=1), 0.0).astype(BF16)

    carried = [_dot(vt_ref[0, tile(t), :], x_in) for t in tiles]
    for t in tiles:
        y = local[t] + carried[t] + d_ref[0, tile(t), :] * ut_ref[0, tile(t), :].astype(F32)
        y_ref[0, tile(t), :] = _gelu_tanh(y).astype(BF16)


def _s5_param_kernel(are_ref, aim_ref, ldt_ref, btr_ref, bti_ref, cr_ref, ci_ref,
                     tt_ref, wt_ref, vt_ref, sc_ref, *, chunk):
    n, p = cr_ref.shape[1:]
    hi = lax.Precision.HIGHEST
    dt = jnp.exp(ldt_ref[0])
    lam_re = jnp.minimum(are_ref[0], -1e-4)
    lam_im = aim_ref[0]
    twice = lambda x: jnp.concatenate([x, x], axis=1)
    rate, freq = twice(lam_re) * dt, twice(lam_im) * dt
    first = lax.broadcasted_iota(jnp.int32, (1, 2 * p), 1) < p

    def powers(j):
        ang = freq * j
        return jnp.exp(rate * j) * jnp.where(first, jnp.cos(ang), jnp.sin(ang))

    j0 = lax.broadcasted_iota(jnp.int32, (chunk, 1), 0).astype(F32)
    abar = powers(jnp.ones((1, 1), F32))
    k_idx = lax.broadcasted_iota(jnp.int32, (sc_ref.shape[2], 1), 0).astype(F32)
    sc_ref[0] = powers(jnp.exp2(k_idx) * chunk).T
    nr, ni = abar[:, :p] - 1.0, abar[:, p:]
    den = lam_re * lam_re + lam_im * lam_im
    f_re = (nr * lam_re + ni * lam_im) / den
    f_im = (ni * lam_re - nr * lam_im) / den
    bt_re = f_re * btr_ref[0] - f_im * bti_ref[0]
    bt_im = f_re * bti_ref[0] + f_im * btr_ref[0]

    rows = chunk * n

    def per_step(pw):
        x = jnp.broadcast_to(pw[:, None, :], (chunk, n, 2 * p)).reshape(rows, 2 * p)
        return x, pltpu.roll(x, p, axis=1)

    def per_chan(x):
        return jnp.broadcast_to(twice(x)[None], (chunk, n, 2 * p)).reshape(rows, 2 * p)

    x, x_sw = per_step(powers(j0 + 1.0))
    a, b = per_chan(cr_ref[0]) * x, per_chan(ci_ref[0]) * x_sw
    vt_ref[0] = jnp.where(first, a - b, -(a + b)).astype(BF16)
    x, x_sw = per_step(powers(chunk - 1.0 - j0))
    a, b = x * per_chan(bt_re), x_sw * per_chan(bt_im)
    w = jnp.where(first, a - b, a + b)
    wt_ref[0] = w.T.astype(BF16)

    c_cat = jnp.concatenate([cr_ref[0], -ci_ref[0]], axis=1)
    k_rev = lax.dot_general(c_cat, w, (((1,), (1,)), ((), ())), precision=hi, preferred_element_type=F32)
    sub = SSM_SUB
    k_rev = jnp.concatenate([k_rev, jnp.zeros((n, (sub - 1) * n), F32)], axis=1)
    for d in range(chunk // sub):
        for t in range(sub):
            off = (chunk - 1 - sub * d - t) * n
            tt_ref[0, d, t * n:(t + 1) * n, :] = k_rev[:, off:off + sub * n].astype(BF16)


def _s5_weights(a_re, a_im, log_dt, b_re, b_im, c_re, c_im, chunk):
    g, p = a_re.shape
    n = b_re.shape[-1]
    tiles, tw, cw = chunk // SSM_SUB, SSM_SUB * n, chunk * n
    grp = lambda shape: pl.BlockSpec((1,) + shape, lambda i: (i,) + (0,) * len(shape))
    return pl.pallas_call(
        functools.partial(_s5_param_kernel, chunk=chunk),
        grid=(g,),
        in_specs=[grp((1, p)), grp((1, p)), grp((1, 1)), grp((n, p)), grp((n, p)), grp((n, p)), grp((n, p))],
        out_specs=[grp((tiles, tw, tw)), grp((2 * p, cw)), grp((cw, 2 * p)), grp((2 * p, SCAN_STEPS))],
        out_shape=[jax.ShapeDtypeStruct((g, tiles, tw, tw), BF16), jax.ShapeDtypeStruct((g, 2 * p, cw), BF16),
                   jax.ShapeDtypeStruct((g, cw, 2 * p), BF16), jax.ShapeDtypeStruct((g, 2 * p, SCAN_STEPS), F32)],
        compiler_params=pltpu.CompilerParams(dimension_semantics=("parallel",)),
        name="s5_params",
    )(a_re[:, None], a_im[:, None], log_dt[:, None, None], jnp.swapaxes(b_re, 1, 2),
      jnp.swapaxes(b_im, 1, 2), c_re, c_im)


def _s5(u_t, seq_chunks, a_re, a_im, log_dt, b_re, b_im, c_re, c_im, d_skip):
    g, cw, nc = u_t.shape
    n = SSM_GROUP
    chunk = cw // n
    assert seq_chunks & (seq_chunks - 1) == 0 and seq_chunks <= 1 << SCAN_STEPS
    tt, wt, vt, sc = _s5_weights(a_re, a_im, log_dt, b_re, b_im, c_re, c_im, chunk)
    dcol = jnp.tile(d_skip.reshape(g, 1, n), (1, chunk, 1)).reshape(g, cw, 1)
    grp = lambda shape: pl.BlockSpec((1,) + shape, lambda i: (i,) + (0,) * len(shape))
    return pl.pallas_call(
        functools.partial(_s5_kernel, seq_chunks=seq_chunks),
        grid=(g,),
        in_specs=[grp((cw, nc)), grp(tt.shape[1:]), grp(wt.shape[1:]), grp(vt.shape[1:]), grp(sc.shape[1:]),
                  grp((cw, 1))],
        out_specs=grp((cw, nc)),
        out_shape=jax.ShapeDtypeStruct((g, cw, nc), BF16),
        compiler_params=pltpu.CompilerParams(dimension_semantics=("parallel",),
                                             vmem_limit_bytes=VMEM_LIMIT),
        name="s5",
    )(u_t, tt, wt, vt, sc, dcol)


def _merge_kernel(x_ref, attn_ref, y_ref, ga_ref, gb_ref, lng_ref, lnb_ref, wglu_ref, bglu_ref,
                  wsb_ref, wab_ref, wo_ref, g1_ref, b1_ref, o_ref, y_buf):
    ncb, sub, d = x_ref.shape
    n = y_ref.shape[1] // sub
    gl = LANES // n
    for j in range(sub):
        for c in range(y_buf.shape[0]):
            y_j = y_ref[c * gl:(c + 1) * gl, j * n:(j + 1) * n, :].reshape(LANES, ncb).astype(F32)
            y_buf[c, pl.ds(j, ncb, stride=sub), :] = y_j.T
    hb = ncb // 2
    parts = range(2)
    flat = lambda ref, p: ref[p * hb:(p + 1) * hb].reshape(hb * sub, ref.shape[-1])
    ys = [jnp.concatenate([y_buf[c, p * hb * sub:(p + 1) * hb * sub, :] for c in range(y_buf.shape[0])], axis=1)
          for p in parts]
    gate = [_dot(ys[p].astype(BF16), wglu_ref[...]) for p in parts]
    branch_a = [_dot(flat(attn_ref, p), wab_ref[...]) for p in parts]
    yg = [(ys[p] * jax.nn.sigmoid(gate[p] + bglu_ref[...])).astype(BF16) for p in parts]
    branch_b = [_dot(yg[p], wsb_ref[...]) for p in parts]
    merged = [(flat(ga_ref, p).astype(F32) * branch_a[p] + flat(gb_ref, p).astype(F32) * branch_b[p]).astype(BF16)
              for p in parts]
    mixed = [_dot(merged[p], wo_ref[...]) for p in parts]
    for p in parts:
        h = _layer_norm(flat(x_ref, p), lng_ref[...], lnb_ref[...])
        r = DEEPNORM_ALPHA * h + mixed[p]
        o_ref[p * hb:(p + 1) * hb] = _layer_norm(r, g1_ref[...], b1_ref[...]).reshape(hb, sub, d)


def _merge(x4, attn4, y_t, ga4, gb4, ln_g, ln_b, w_glu, b_glu, w_ssm_br, w_attn_br, w_o, ln1_g, ln1_b):
    nc, slabs, sub, d = x4.shape
    groups, cw, _ = y_t.shape
    ncb = CHUNK_BLOCK
    slab = lambda a: pl.BlockSpec((ncb, None, sub, a.shape[-1]), lambda i, c: (c, i, 0, 0))
    vec = lambda v: v[None]
    ws = [w_glu.astype(BF16), vec(b_glu), w_ssm_br.astype(BF16), w_attn_br.astype(BF16),
          w_o.astype(BF16), vec(ln1_g), vec(ln1_b)]
    return pl.pallas_call(
        _merge_kernel,
        grid=(slabs, nc // ncb),
        in_specs=[slab(x4), slab(attn4), pl.BlockSpec((groups, cw // slabs, ncb), lambda i, c: (0, i, c)),
                  slab(ga4), slab(gb4), _const_spec((1, d)), _const_spec((1, d))]
                 + [_const_spec(w.shape) for w in ws],
        out_specs=slab(x4),
        out_shape=jax.ShapeDtypeStruct(x4.shape, F32),
        scratch_shapes=[pltpu.VMEM((groups * SSM_GROUP // LANES, ncb * sub, LANES), F32)],
        compiler_params=pltpu.CompilerParams(dimension_semantics=("parallel", "parallel"),
                                             vmem_limit_bytes=VMEM_LIMIT),
        name="merge",
    )(x4, attn4, y_t, ga4, gb4, vec(ln_g), vec(ln_b), *ws)


def _ffn_kernel(h_ref, p_ref, wup_ref, wdn_ref, g2_ref, b2_ref, wpg_ref, bpg_ref, wple_ref,
                g3_ref, b3_ref, o_ref, *, ff_chunk):
    half = h_ref.shape[0] // 2
    parts = range(2)
    rs = lambda p: slice(p * half, (p + 1) * half)
    h = [h_ref[rs(p), :] for p in parts]
    hb = [h[p].astype(BF16) for p in parts]
    ff = [jnp.zeros_like(h[p]) for p in parts]
    for c in range(wup_ref.shape[1] // ff_chunk):
        cs = slice(c * ff_chunk, (c + 1) * ff_chunk)
        up = [jnp.maximum(_dot(hb[p], wup_ref[:, cs]), 0.0) for p in parts]
        ff = [ff[p] + _dot((up[p] * up[p]).astype(BF16), wdn_ref[cs, :]) for p in parts]
    ple = [_dot(p_ref[rs(p), :].astype(BF16), wple_ref[...]) for p in parts]
    h2 = [_layer_norm(DEEPNORM_ALPHA * h[p] + ff[p], g2_ref[...], b2_ref[...]) for p in parts]
    gate = [_dot(h2[p].astype(BF16), wpg_ref[...]) for p in parts]
    for p in parts:
        r = DEEPNORM_ALPHA * h2[p] + jax.nn.sigmoid(gate[p] + bpg_ref[...]) * ple[p]
        o_ref[rs(p), :] = _layer_norm(r, g3_ref[...], b3_ref[...])


def _ffn(h1, p2, w_up, w_down, ln2_g, ln2_b, w_pg, b_pg, w_ple, ln3_g, ln3_b, tm):
    t, d = h1.shape
    row = lambda w: pl.BlockSpec((tm, w), lambda i: (i, 0))
    vec = lambda v: v[None]
    ws = [w_up.astype(BF16), w_down.astype(BF16), vec(ln2_g), vec(ln2_b), w_pg.astype(BF16), vec(b_pg),
          w_ple.astype(BF16), vec(ln3_g), vec(ln3_b)]
    return pl.pallas_call(
        functools.partial(_ffn_kernel, ff_chunk=1024),
        grid=(t // tm,),
        in_specs=[row(d), row(p2.shape[1])] + [_const_spec(w.shape) for w in ws],
        out_specs=row(d),
        out_shape=jax.ShapeDtypeStruct((t, d), F32),
        compiler_params=pltpu.CompilerParams(dimension_semantics=("parallel",),
                                             vmem_limit_bytes=VMEM_LIMIT),
        name="ffn",
    )(h1, p2, *ws)


def kernel(x, p, positions, ln_in_g, ln_in_b, w_in, b_gate, q_norm_g, w_uq, kv_norm_g, w_ukv, w_attn_br, a_re, a_im, log_dt, b_re, b_im, c_re, c_im, d_skip, w_glu, b_glu, w_ssm_br, w_o, ln1_g, ln1_b, w_up, w_down, ln2_g, ln2_b, w_ple_gate, b_ple_gate, w_ple, ln3_g, ln3_b):
    assert w_in.shape[0] == DEPTH
    b, s, d = x.shape
    t = b * s
    chunk = SSM_CHUNK
    seq_chunks = s // chunk
    nc = b * seq_chunks
    tm = min(512, s)
    tq = min(512, s)
    sub = 8
    slabs = chunk // sub
    view = lambda a: a.reshape(nc, slabs, sub, a.shape[-1])
    x4 = view(x)

    q, k, v, u_t, ga, gb = _in_proj(x4, positions, ln_in_g, ln_in_b, w_in[0], b_gate[0],
                                    q_norm_g[0], w_uq[0], kv_norm_g[0], w_ukv[0])
    attn = _attention(q.reshape(b, s, -1), k.reshape(b, s, -1), v.reshape(b, s, -1), tq)
    y_t = _s5(u_t, seq_chunks, a_re[0], a_im[0], log_dt[0], b_re[0], b_im[0], c_re[0], c_im[0], d_skip[0])
    h1 = _merge(x4, view(attn), y_t, ga, gb, ln_in_g, ln_in_b, w_glu[0], b_glu[0], w_ssm_br[0],
                w_attn_br[0], w_o[0], ln1_g[0], ln1_b[0])
    out = _ffn(h1.reshape(t, d), p[0].reshape(t, -1), w_up[0], w_down[0], ln2_g[0], ln2_b[0], w_ple_gate[0],
               b_ple_gate[0], w_ple[0], ln3_g[0], ln3_b[0], tm)
    return out.reshape(b, s, d)
```

```python
import functools
import math

import jax
import jax.numpy as jnp
from jax import lax
from jax.experimental import pallas as pl
from jax.experimental.pallas import tpu as pltpu

F32 = jnp.float32
BF16 = jnp.bfloat16

MLA_HEADS = 8
QK_NOPE_DIM = 64
QK_ROPE_DIM = 32
V_HEAD_DIM = 64
ROPE_THETA = 10000.0
SSM_GROUP = 16
LN_EPS = 1e-5
RMS_EPS = 1e-6
DEPTH = 1
DEEPNORM_ALPHA = (2.0 * DEPTH) ** 0.25

LANES = 128
MXU_DIM = 256
HEAD_PAD = LANES
BF16_ROWS = 16
V_AUG = V_HEAD_DIM + BF16_ROWS
SSM_SUB = MXU_DIM // SSM_GROUP
SSM_CHUNK = 64
SCAN_STEPS = 8
CHUNK_BLOCK = LANES
VMEM_LIMIT = 56 * 1024 * 1024


def _const_spec(shape):
    nd = len(shape)
    return pl.BlockSpec(shape, lambda *_: (0,) * nd, pipeline_mode=pl.Buffered(1))


def _layer_norm(x, g, b):
    mu = jnp.mean(x, axis=-1, keepdims=True)
    xc = x - mu
    var = jnp.mean(xc * xc, axis=-1, keepdims=True)
    return xc * lax.rsqrt(var + LN_EPS) * g + b


def _rms_norm(x, g):
    return x * lax.rsqrt(jnp.mean(x * x, axis=-1, keepdims=True) + RMS_EPS) * g


def _dot(a, b):
    return jnp.dot(a, b, preferred_element_type=F32)


def _dot_t(a, b):
    return lax.dot_general(a, b, (((1,), (1,)), ((), ())), preferred_element_type=F32)


def _in_proj_kernel(x_ref, pos_ref, lng_ref, lnb_ref, wa_ref, bg_ref, qg_ref, wq_ref, kvg_ref,
                    wk_ref, wv_ref, wu_ref, invf_ref,
                    q_ref, k_ref, v_ref, ut_ref, ga_ref, gb_ref, u_buf, *, cols, scale):
    ncb, sub, d_in = x_ref.shape
    hb = ncb // 2
    rows = hb * sub
    parts = range(2)
    cs = lambda p: slice(p * hb, (p + 1) * hb)
    tile = lambda v: v.reshape(hb, sub, v.shape[-1])
    h = [_layer_norm(x_ref[cs(p)].reshape(rows, d_in), lng_ref[...], lnb_ref[...]).astype(BF16) for p in parts]

    def proj(name):
        a, b = cols[name]
        return [_dot(h[p], wa_ref[:, a:b]) for p in parts]

    cq, ckv, kr = proj("cq"), proj("ckv"), proj("kr")

    one = jnp.ones((rows, QK_NOPE_DIM), F32)
    pad = HEAD_PAD - QK_NOPE_DIM - QK_ROPE_DIM
    half = QK_ROPE_DIM // 2
    zero = one * 0.0
    cos2, sin_lo, sin_hi, k_rope2 = [], [], [], []
    for p in parts:
        ang_t = invf_ref[...] * pos_ref[:, p * rows:(p + 1) * rows].astype(F32)
        cos_f, sin_f = jnp.cos(ang_t).T, jnp.sin(ang_t).T
        cos = jnp.concatenate([one, cos_f, cos_f, one[:, :pad]], axis=1)
        sin = jnp.concatenate([one * 0.0, sin_f, sin_f, one[:, :pad] * 0.0], axis=1)
        cos2.append(jnp.concatenate([cos, cos], axis=1))
        sin_lo.append(jnp.concatenate([zero, -sin_f, zero[:, :half + pad]], axis=1))
        sin_hi.append(jnp.concatenate([zero, zero[:, :half], sin_f, zero[:, :pad]], axis=1))
        k_rope = kr[p][:, :HEAD_PAD] * cos + kr[p][:, HEAD_PAD:] * sin
        k_rope2.append(jnp.concatenate([k_rope, k_rope], axis=1))

    cqn = [_rms_norm(cq[p], qg_ref[...]).astype(BF16) for p in parts]
    ckvn = [_rms_norm(ckv[p], kvg_ref[...]).astype(BF16) for p in parts]
    width = wk_ref.shape[1]
    for c in range(width // MXU_DIM):
        sl = slice(c * MXU_DIM, (c + 1) * MXU_DIM)
        q = [_dot(cqn[p], wq_ref[:, sl]) for p in parts]
        k = [_dot(ckvn[p], wk_ref[:, sl]) for p in parts]
        for p in parts:
            q_rot = jnp.concatenate(
                [pltpu.roll(q[p][:, hh * HEAD_PAD:(hh + 1) * HEAD_PAD], HEAD_PAD - half, axis=1) * sin_lo[p]
                 + pltpu.roll(q[p][:, hh * HEAD_PAD:(hh + 1) * HEAD_PAD], half, axis=1) * sin_hi[p]
                 for hh in range(MXU_DIM // HEAD_PAD)], axis=1)
            q_ref[cs(p), :, sl] = tile(((q[p] * cos2[p] + q_rot) * scale).astype(BF16))
            k_ref[cs(p), :, sl] = tile((k[p] + k_rope2[p]).astype(BF16))
    v = [_dot(ckvn[p], wv_ref[...]) for p in parts]
    u = [_dot(h[p], wu_ref[...]) for p in parts]
    ga, gb = proj("ga"), proj("gb")
    d = ga_ref.shape[-1]
    for p in parts:
        v_ref[cs(p)] = tile(v[p].astype(BF16))
        for c in range(u_buf.shape[0]):
            u_buf[c, p * rows:(p + 1) * rows, :] = u[p][:, c * LANES:(c + 1) * LANES]
        ga_ref[cs(p)] = tile(jax.nn.sigmoid(ga[p] + bg_ref[:, :d]).astype(BF16))
        gb_ref[cs(p)] = tile(jax.nn.sigmoid(gb[p] + bg_ref[:, d:]).astype(BF16))
    n = ut_ref.shape[1] // sub
    gl = LANES // n
    for j in range(sub):
        for c in range(u_buf.shape[0]):
            u_j = u_buf[c, pl.ds(j, ncb, stride=sub), :].T
            ut_ref[c * gl:(c + 1) * gl, j * n:(j + 1) * n, :] = u_j.astype(BF16).reshape(gl, n, ncb)


def _rope_swap(w):
    half = w.shape[-1] // 2
    return jnp.concatenate([-w[..., half:], w[..., :half]], axis=-1)


def _head_slot(nope, rope):
    k, h, _ = rope.shape
    parts = [nope if nope is not None else jnp.zeros((k, h, QK_NOPE_DIM), rope.dtype), rope]
    used = QK_NOPE_DIM + QK_ROPE_DIM
    parts.append(jnp.zeros((k, h, HEAD_PAD - used), rope.dtype))
    return jnp.concatenate(parts, axis=-1).reshape(k, h * HEAD_PAD)


def _in_proj(x4, positions, ln_g, ln_b, w_in, b_gate, q_norm_g, w_uq, kv_norm_g, w_ukv):
    nc, slabs, sub, d = x4.shape
    ncb = CHUNK_BLOCK
    pos4 = positions.reshape(nc // ncb, ncb, slabs, sub).transpose(0, 2, 1, 3).reshape(nc // ncb, slabs, 1, ncb * sub)
    q_lora = q_norm_g.shape[0]
    kv_lora = kv_norm_g.shape[0]
    ssm_w = w_in.shape[1] - (q_lora + kv_lora + QK_ROPE_DIM + 2 * d)
    heads = MLA_HEADS

    o = 0
    w_cq = w_in[:, o:o + q_lora]; o += q_lora
    w_ckv = w_in[:, o:o + kv_lora]; o += kv_lora
    w_kr = w_in[:, o:o + QK_ROPE_DIM]; o += QK_ROPE_DIM
    w_u = w_in[:, o:o + ssm_w]; o += ssm_w
    w_ga = w_in[:, o:o + d]; o += d
    w_gb = w_in[:, o:o + d]
    kr_slot = _head_slot(None, w_kr[:, None, :])
    kr_sw_slot = _head_slot(None, _rope_swap(w_kr)[:, None, :])
    pieces = [("cq", w_cq), ("ckv", w_ckv), ("kr", jnp.concatenate([kr_slot, kr_sw_slot], axis=1)),
              ("ga", w_ga), ("gb", w_gb)]
    cols, o = {}, 0
    for name, w in pieces:
        cols[name] = (o, o + w.shape[1])
        o += w.shape[1]
    w_a = jnp.concatenate([w for _, w in pieces], axis=1).astype(BF16)

    wq = w_uq.reshape(q_lora, heads, QK_NOPE_DIM + QK_ROPE_DIM)
    wq_nope, wq_rope = wq[..., :QK_NOPE_DIM], wq[..., QK_NOPE_DIM:]
    w_q = _head_slot(wq_nope, wq_rope).astype(BF16)
    wkv = w_ukv.reshape(kv_lora, heads, QK_NOPE_DIM + V_HEAD_DIM)
    w_k = _head_slot(wkv[..., :QK_NOPE_DIM], jnp.zeros((kv_lora, heads, QK_ROPE_DIM), F32)).astype(BF16)
    w_v = wkv[..., QK_NOPE_DIM:].reshape(kv_lora, heads * V_HEAD_DIM).astype(BF16)
    w_u = w_u.astype(BF16)

    inv_freq = ROPE_THETA ** (-jnp.arange(0, QK_ROPE_DIM, 2, dtype=F32) / QK_ROPE_DIM)
    invf = inv_freq[:, None]

    hw = heads * HEAD_PAD
    hv = heads * V_HEAD_DIM
    groups = ssm_w // SSM_GROUP
    slab = lambda w: pl.BlockSpec((ncb, None, sub, w), lambda i, c: (c, i, 0, 0))
    scale = (QK_NOPE_DIM + QK_ROPE_DIM) ** -0.5 * math.log2(math.e)
    nat = lambda w: jax.ShapeDtypeStruct((nc, slabs, sub, w), BF16)
    return pl.pallas_call(
        functools.partial(_in_proj_kernel, cols=cols, scale=scale),
        grid=(slabs, nc // ncb),
        in_specs=[slab(d), pl.BlockSpec((None, None, 1, ncb * sub), lambda i, c: (c, i, 0, 0)),
                  _const_spec((1, d)), _const_spec((1, d)), _const_spec(w_a.shape),
                  _const_spec((1, 2 * d)), _const_spec((1, q_lora)), _const_spec(w_q.shape),
                  _const_spec((1, kv_lora)), _const_spec(w_k.shape), _const_spec(w_v.shape),
                  _const_spec(w_u.shape), _const_spec(invf.shape)],
        out_specs=[slab(hw), slab(hw), slab(hv),
                   pl.BlockSpec((groups, sub * SSM_GROUP, ncb), lambda i, c: (0, i, c)), slab(d), slab(d)],
        out_shape=[nat(hw), nat(hw), nat(hv),
                   jax.ShapeDtypeStruct((groups, slabs * sub * SSM_GROUP, nc), BF16), nat(d), nat(d)],
        scratch_shapes=[pltpu.VMEM((ssm_w // LANES, ncb * sub, LANES), F32)],
        compiler_params=pltpu.CompilerParams(dimension_semantics=("parallel", "parallel"),
                                             vmem_limit_bytes=VMEM_LIMIT),
        name="in_proj",
    )(x4, pos4, ln_g[None], ln_b[None], w_a, b_gate[None], q_norm_g[None], w_q, kv_norm_g[None],
      w_k, w_v, w_u, invf)


def _attn_kernel(idx_ref, q_ref, k_ref, v_ref, o_ref, s_buf, vt_buf, qt_buf, *, tq, tk):
    extra = (lax.broadcasted_iota(jnp.int32, (V_AUG - V_HEAD_DIM, tk), 0) == 0).astype(BF16)
    for j in range(vt_buf.shape[0]):
        v_t = v_ref[0, j * tk:(j + 1) * tk, :].T
        for hh in range(2):
            vt_buf[j, hh] = jnp.concatenate([v_t[hh * V_HEAD_DIM:(hh + 1) * V_HEAD_DIM], extra], axis=0)

    key = lax.broadcasted_iota(jnp.int32, (tk, tq), 0)
    qry = lax.broadcasted_iota(jnp.int32, (tk, tq), 1)
    heads = range(2)

    class QueryBlock:
        def __init__(self, i):
            self.side, self.i, self.quads = i % 2, i, idx_ref[i] // 2
            self.rows = slice(i * tq, (i + 1) * tq)

        def setup(self):
            for hh in heads:
                qt_buf[self.side, hh] = q_ref[0, self.rows, hh * HEAD_PAD:(hh + 1) * HEAD_PAD].T

        def scores(self, j, slot):
            start = pl.multiple_of(j * tk, tk)
            for hh in heads:
                s_buf[self.side, slot, hh] = _dot(k_ref[0, pl.ds(start, tk), hh * HEAD_PAD:(hh + 1) * HEAD_PAD],
                                                  qt_buf[self.side, hh])

        def update(self, j, slot, state, mask_offset):
            ps, stats = [], []
            for hh in heads:
                m, _ = state[hh]
                s = s_buf[self.side, slot, hh]
                if mask_offset is not None:
                    s = jnp.where(key + mask_offset <= qry, s, -1e30)
                m_new = jnp.maximum(m, jnp.max(s, axis=0, keepdims=True))
                stats.append((m_new, jnp.exp2(m - m_new)))
                ps.append(jnp.exp2(s - m_new).astype(BF16))
            out = []
            for hh in heads:
                m_new, a = stats[hh]
                out.append((m_new, a * state[hh][1] + _dot(vt_buf[j, hh], ps[hh])))
            return tuple(out)

        def pair(self, p, state):
            blk = 2 * p
            self.scores(blk + 1, 1)
            state = self.update(blk, 0, state, None)
            self.scores(blk + 2, 0)
            return self.update(blk + 1, 1, state, None)

        def unmasked(self):
            i = self.i
            init = tuple((jnp.full((1, tq), -1e30, F32), jnp.zeros((V_AUG, tq), F32)) for _ in heads)
            state = lax.fori_loop(0, self.quads, lambda qi, st: self.pair(2 * qi + 1, self.pair(2 * qi, st)), init)
            return self.pair(i - 1, state) if i % 2 else state

        def finish(self, state):
            o_t = jnp.concatenate([acc[:V_HEAD_DIM] / acc[V_HEAD_DIM:V_HEAD_DIM + 1] for _, acc in state], axis=0)
            o_ref[0, self.rows, :] = o_t.T.astype(BF16)

    blocks = [QueryBlock(i) for i in range(q_ref.shape[1] // tq)]
    blocks[0].setup()
    blocks[0].scores(0, 0)
    for blk, nxt in zip(blocks, blocks[1:] + [None]):
        if nxt is not None:
            nxt.setup()
        state = blk.unmasked()
        diag = 2 * blk.i
        blk.scores(diag + 1, 1)
        state = blk.update(diag, 0, state, 0)
        if nxt is not None:
            nxt.scores(0, 0)
        blk.finish(blk.update(diag + 1, 1, state, tk))


def _attention(q, k, v, tq):
    b, s, _ = q.shape
    pairs = MLA_HEADS // 2
    tk = tq // 2
    nkv = s // tk
    grid_spec = pltpu.PrefetchScalarGridSpec(
        num_scalar_prefetch=1,
        grid=(b, pairs),
        in_specs=[pl.BlockSpec((1, s, 2 * HEAD_PAD), lambda bi, hp, idx: (bi, 0, hp)),
                  pl.BlockSpec((1, s, 2 * HEAD_PAD), lambda bi, hp, idx: (bi, 0, hp)),
                  pl.BlockSpec((1, s, 2 * V_HEAD_DIM), lambda bi, hp, idx: (bi, 0, hp))],
        out_specs=pl.BlockSpec((1, s, 2 * V_HEAD_DIM), lambda bi, hp, idx: (bi, 0, hp)),
        scratch_shapes=[pltpu.VMEM((2, 2, 2, tk, tq), F32), pltpu.VMEM((nkv, 2, V_AUG, tk), BF16),
                        pltpu.VMEM((2, 2, HEAD_PAD, tq), BF16)])
    return pl.pallas_call(
        functools.partial(_attn_kernel, tq=tq, tk=tk),
        grid_spec=grid_spec,
        out_shape=jax.ShapeDtypeStruct((b, s, MLA_HEADS * V_HEAD_DIM), BF16),
        compiler_params=pltpu.CompilerParams(
            dimension_semantics=("parallel", "parallel"), vmem_limit_bytes=VMEM_LIMIT),
        name="attention",
    )(jnp.arange(s // tq, dtype=jnp.int32), q, k, v)


def _gelu_tanh(x):
    k = -2.0 * math.sqrt(2.0 / math.pi) * math.log2(math.e)
    return x / (1.0 + jnp.exp2(x * (k + (k * 0.044715) * (x * x))))


def _s5_kernel(ut_ref, tt_ref, wt_ref, vt_ref, sc_ref, d_ref, y_ref, *, seq_chunks):
    nc = ut_ref.shape[2]
    tiles = range(tt_ref.shape[1])
    tile = lambda t: slice(t * MXU_DIM, (t + 1) * MXU_DIM)
    z = _dot(wt_ref[0], ut_ref[0])
    local = []
    for t in tiles:
        acc = _dot(tt_ref[0, t], ut_ref[0, tile(0), :])
        for s in range(1, t + 1):
            acc += _dot(tt_ref[0, t - s], ut_ref[0, tile(s), :])
        local.append(acc)

    half = z.shape[0] // 2
    pos = lax.broadcasted_iota(jnp.int32, (1, nc), 1) % seq_chunks
    x = z
    for k in range(seq_chunks.bit_length() - 1):
        sh = 1 << k
        a = sc_ref[0, :, k:k + 1]
        a_re, a_im = a[:half], a[half:]
        xs = jnp.where(pos >= sh, pltpu.roll(x, sh, axis=1), 0.0)
        xs_re, xs_im = xs[:half], xs[half:]
        x = x + jnp.concatenate([a_re * xs_re - a_im * xs_im, a_re * xs_im + a_im * xs_re], axis=0)
    x_in = jnp.where(pos >= 1, pltpu.roll(x, 1, axis=1), 0.0).astype(BF16)

    carried = [_dot(vt_ref[0, tile(t), :], x_in) for t in tiles]
    for t in tiles:
        y = local[t] + carried[t] + d_ref[0, tile(t), :] * ut_ref[0, tile(t), :].astype(F32)
        y_ref[0, tile(t), :] = _gelu_tanh(y).astype(BF16)


def _s5_param_kernel(are_ref, aim_ref, ldt_ref, btr_ref, bti_ref, cr_ref, ci_ref,
                     tt_ref, wt_ref, vt_ref, sc_ref, *, chunk):
    n, p = cr_ref.shape[1:]
    hi = lax.Precision.HIGHEST
    dt = jnp.exp(ldt_ref[0])
    lam_re = jnp.minimum(are_ref[0], -1e-4)
    lam_im = aim_ref[0]
    twice = lambda x: jnp.concatenate([x, x], axis=1)
    rate, freq = twice(lam_re) * dt, twice(lam_im) * dt
    first = lax.broadcasted_iota(jnp.int32, (1, 2 * p), 1) < p

    def powers(j):
        ang = freq * j
        return jnp.exp(rate * j) * jnp.where(first, jnp.cos(ang), jnp.sin(ang))

    j0 = lax.broadcasted_iota(jnp.int32, (chunk, 1), 0).astype(F32)
    abar = powers(jnp.ones((1, 1), F32))
    k_idx = lax.broadcasted_iota(jnp.int32, (sc_ref.shape[2], 1), 0).astype(F32)
    sc_ref[0] = powers(jnp.exp2(k_idx) * chunk).T
    nr, ni = abar[:, :p] - 1.0, abar[:, p:]
    den = lam_re * lam_re + lam_im * lam_im
    f_re = (nr * lam_re + ni * lam_im) / den
    f_im = (ni * lam_re - nr * lam_im) / den
    bt_re = f_re * btr_ref[0] - f_im * bti_ref[0]
    bt_im = f_re * bti_ref[0] + f_im * btr_ref[0]

    rows = chunk * n

    def per_step(pw):
        x = jnp.broadcast_to(pw[:, None, :], (chunk, n, 2 * p)).reshape(rows, 2 * p)
        return x, pltpu.roll(x, p, axis=1)

    def per_chan(x):
        return jnp.broadcast_to(twice(x)[None], (chunk, n, 2 * p)).reshape(rows, 2 * p)

    x, x_sw = per_step(powers(j0 + 1.0))
    a, b = per_chan(cr_ref[0]) * x, per_chan(ci_ref[0]) * x_sw
    vt_ref[0] = jnp.where(first, a - b, -(a + b)).astype(BF16)
    x, x_sw = per_step(powers(chunk - 1.0 - j0))
    a, b = x * per_chan(bt_re), x_sw * per_chan(bt_im)
    w = jnp.where(first, a - b, a + b)
    wt_ref[0] = w.T.astype(BF16)

    c_cat = jnp.concatenate([cr_ref[0], -ci_ref[0]], axis=1)
    k_rev = lax.dot_general(c_cat, w, (((1,), (1,)), ((), ())), precision=hi, preferred_element_type=F32)
    sub = SSM_SUB
    k_rev = jnp.concatenate([k_rev, jnp.zeros((n, (sub - 1) * n), F32)], axis=1)
    for d in range(chunk // sub):
        for t in range(sub):
            off = (chunk - 1 - sub * d - t) * n
            tt_ref[0, d, t * n:(t + 1) * n, :] = k_rev[:, off:off + sub * n].astype(BF16)


def _s5_weights(a_re, a_im, log_dt, b_re, b_im, c_re, c_im, chunk):
    g, p = a_re.shape
    n = b_re.shape[-1]
    tiles, tw, cw = chunk // SSM_SUB, SSM_SUB * n, chunk * n
    grp = lambda shape: pl.BlockSpec((1,) + shape, lambda i: (i,) + (0,) * len(shape))
    return pl.pallas_call(
        functools.partial(_s5_param_kernel, chunk=chunk),
        grid=(g,),
        in_specs=[grp((1, p)), grp((1, p)), grp((1, 1)), grp((n, p)), grp((n, p)), grp((n, p)), grp((n, p))],
        out_specs=[grp((tiles, tw, tw)), grp((2 * p, cw)), grp((cw, 2 * p)), grp((2 * p, SCAN_STEPS))],
        out_shape=[jax.ShapeDtypeStruct((g, tiles, tw, tw), BF16), jax.ShapeDtypeStruct((g, 2 * p, cw), BF16),
                   jax.ShapeDtypeStruct((g, cw, 2 * p), BF16), jax.ShapeDtypeStruct((g, 2 * p, SCAN_STEPS), F32)],
        compiler_params=pltpu.CompilerParams(dimension_semantics=("parallel",)),
        name="s5_params",
    )(a_re[:, None], a_im[:, None], log_dt[:, None, None], jnp.swapaxes(b_re, 1, 2),
      jnp.swapaxes(b_im, 1, 2), c_re, c_im)


def _s5(u_t, seq_chunks, a_re, a_im, log_dt, b_re, b_im, c_re, c_im, d_skip):
    g, cw, nc = u_t.shape
    n = SSM_GROUP
    chunk = cw // n
    assert seq_chunks & (seq_chunks - 1) == 0 and seq_chunks <= 1 << SCAN_STEPS
    tt, wt, vt, sc = _s5_weights(a_re, a_im, log_dt, b_re, b_im, c_re, c_im, chunk)
    dcol = jnp.tile(d_skip.reshape(g, 1, n), (1, chunk, 1)).reshape(g, cw, 1)
    grp = lambda shape: pl.BlockSpec((1,) + shape, lambda i: (i,) + (0,) * len(shape))
    return pl.pallas_call(
        functools.partial(_s5_kernel, seq_chunks=seq_chunks),
        grid=(g,),
        in_specs=[grp((cw, nc)), grp(tt.shape[1:]), grp(wt.shape[1:]), grp(vt.shape[1:]), grp(sc.shape[1:]),
                  grp((cw, 1))],
        out_specs=grp((cw, nc)),
        out_shape=jax.ShapeDtypeStruct((g, cw, nc), BF16),
        compiler_params=pltpu.CompilerParams(dimension_semantics=("parallel",),
                                             vmem_limit_bytes=VMEM_LIMIT),
        name="s5",
    )(u_t, tt, wt, vt, sc, dcol)


def _merge_kernel(x_ref, attn_ref, y_ref, ga_ref, gb_ref, lng_ref, lnb_ref, wglu_ref, bglu_ref,
                  wsb_ref, wab_ref, wo_ref, g1_ref, b1_ref, o_ref, y_buf):
    ncb, sub, d = x_ref.shape
    n = y_ref.shape[1] // sub
    gl = LANES // n
    for j in range(sub):
        for c in range(y_buf.shape[0]):
            y_j = y_ref[c * gl:(c + 1) * gl, j * n:(j + 1) * n, :].reshape(LANES, ncb).astype(F32)
            y_buf[c, pl.ds(j, ncb, stride=sub), :] = y_j.T
    hb = ncb // 2
    parts = range(2)
    flat = lambda ref, p: ref[p * hb:(p + 1) * hb].reshape(hb * sub, ref.shape[-1])
    ys = [jnp.concatenate([y_buf[c, p * hb * sub:(p + 1) * hb * sub, :] for c in range(y_buf.shape[0])], axis=1)
          for p in parts]
    gate = [_dot(ys[p].astype(BF16), wglu_ref[...]) for p in parts]
    branch_a = [_dot(flat(attn_ref, p), wab_ref[...]) for p in parts]
    yg = [(ys[p] * jax.nn.sigmoid(gate[p] + bglu_ref[...])).astype(BF16) for p in parts]
    branch_b = [_dot(yg[p], wsb_ref[...]) for p in parts]
    merged = [(flat(ga_ref, p).astype(F32) * branch_a[p] + flat(gb_ref, p).astype(F32) * branch_b[p]).astype(BF16)
              for p in parts]
    mixed = [_dot(merged[p], wo_ref[...]) for p in parts]
    for p in parts:
        h = _layer_norm(flat(x_ref, p), lng_ref[...], lnb_ref[...])
        r = DEEPNORM_ALPHA * h + mixed[p]
        o_ref[p * hb:(p + 1) * hb] = _layer_norm(r, g1_ref[...], b1_ref[...]).reshape(hb, sub, d)


def _merge(x4, attn4, y_t, ga4, gb4, ln_g, ln_b, w_glu, b_glu, w_ssm_br, w_attn_br, w_o, ln1_g, ln1_b):
    nc, slabs, sub, d = x4.shape
    groups, cw, _ = y_t.shape
    ncb = CHUNK_BLOCK
    slab = lambda a: pl.BlockSpec((ncb, None, sub, a.shape[-1]), lambda i, c: (c, i, 0, 0))
    vec = lambda v: v[None]
    ws = [w_glu.astype(BF16), vec(b_glu), w_ssm_br.astype(BF16), w_attn_br.astype(BF16),
          w_o.astype(BF16), vec(ln1_g), vec(ln1_b)]
    return pl.pallas_call(
        _merge_kernel,
        grid=(slabs, nc // ncb),
        in_specs=[slab(x4), slab(attn4), pl.BlockSpec((groups, cw // slabs, ncb), lambda i, c: (0, i, c)),
                  slab(ga4), slab(gb4), _const_spec((1, d)), _const_spec((1, d))]
                 + [_const_spec(w.shape) for w in ws],
        out_specs=slab(x4),
        out_shape=jax.ShapeDtypeStruct(x4.shape, F32),
        scratch_shapes=[pltpu.VMEM((groups * SSM_GROUP // LANES, ncb * sub, LANES), F32)],
        compiler_params=pltpu.CompilerParams(dimension_semantics=("parallel", "parallel"),
                                             vmem_limit_bytes=VMEM_LIMIT),
        name="merge",
    )(x4, attn4, y_t, ga4, gb4, vec(ln_g), vec(ln_b), *ws)


def _ffn_kernel(h_ref, p_ref, wup_ref, wdn_ref, g2_ref, b2_ref, wpg_ref, bpg_ref, wple_ref,
                g3_ref, b3_ref, o_ref, *, ff_chunk):
    half = h_ref.shape[0] // 2
    parts = range(2)
    rs = lambda p: slice(p * half, (p + 1) * half)
    h = [h_ref[rs(p), :] for p in parts]
    hb = [h[p].astype(BF16) for p in parts]
    ff = [jnp.zeros_like(h[p]) for p in parts]
    for c in range(wup_ref.shape[1] // ff_chunk):
        cs = slice(c * ff_chunk, (c + 1) * ff_chunk)
        up = [jnp.maximum(_dot(hb[p], wup_ref[:, cs]), 0.0) for p in parts]
        ff = [ff[p] + _dot((up[p] * up[p]).astype(BF16), wdn_ref[cs, :]) for p in parts]
    ple = [_dot(p_ref[rs(p), :].astype(BF16), wple_ref[...]) for p in parts]
    h2 = [_layer_norm(DEEPNORM_ALPHA * h[p] + ff[p], g2_ref[...], b2_ref[...]) for p in parts]
    gate = [_dot(h2[p].astype(BF16), wpg_ref[...]) for p in parts]
    for p in parts:
        r = DEEPNORM_ALPHA * h2[p] + jax.nn.sigmoid(gate[p] + bpg_ref[...]) * ple[p]
        o_ref[rs(p), :] = _layer_norm(r, g3_ref[...], b3_ref[...])


def _ffn(h1, p2, w_up, w_down, ln2_g, ln2_b, w_pg, b_pg, w_ple, ln3_g, ln3_b, tm):
    t, d = h1.shape
    row = lambda w: pl.BlockSpec((tm, w), lambda i: (i, 0))
    vec = lambda v: v[None]
    ws = [w_up.astype(BF16), w_down.astype(BF16), vec(ln2_g), vec(ln2_b), w_pg.astype(BF16), vec(b_pg),
          w_ple.astype(BF16), vec(ln3_g), vec(ln3_b)]
    return pl.pallas_call(
        functools.partial(_ffn_kernel, ff_chunk=1024),
        grid=(t // tm,),
        in_specs=[row(d), row(p2.shape[1])] + [_const_spec(w.shape) for w in ws],
        out_specs=row(d),
        out_shape=jax.ShapeDtypeStruct((t, d), F32),
        compiler_params=pltpu.CompilerParams(dimension_semantics=("parallel",),
                                             vmem_limit_bytes=VMEM_LIMIT),
        name="ffn",
    )(h1, p2, *ws)


def kernel(x, p, positions, ln_in_g, ln_in_b, w_in, b_gate, q_norm_g, w_uq, kv_norm_g, w_ukv, w_attn_br, a_re, a_im, log_dt, b_re, b_im, c_re, c_im, d_skip, w_glu, b_glu, w_ssm_br, w_o, ln1_g, ln1_b, w_up, w_down, ln2_g, ln2_b, w_ple_gate, b_ple_gate, w_ple, ln3_g, ln3_b):
    assert w_in.shape[0] == DEPTH
    b, s, d = x.shape
    t = b * s
    chunk = SSM_CHUNK
    seq_chunks = s // chunk
    nc = b * seq_chunks
    tm = min(512, s)
    tq = min(512, s)
    sub = 8
    slabs = chunk // sub
    view = lambda a: a.reshape(nc, slabs, sub, a.shape[-1])
    x4 = view(x)

    q, k, v, u_t, ga, gb = _in_proj(x4, positions, ln_in_g, ln_in_b, w_in[0], b_gate[0],
                                    q_norm_g[0], w_uq[0], kv_norm_g[0], w_ukv[0])
    attn = _attention(q.reshape(b, s, -1), k.reshape(b, s, -1), v.reshape(b, s, -1), tq)
    y_t = _s5(u_t, seq_chunks, a_re[0], a_im[0], log_dt[0], b_re[0], b_im[0], c_re[0], c_im[0], d_skip[0])
    h1 = _merge(x4, view(attn), y_t, ga, gb, ln_in_g, ln_in_b, w_glu[0], b_glu[0], w_ssm_br[0],
                w_attn_br[0], w_o[0], ln1_g[0], ln1_b[0])
    out = _ffn(h1.reshape(t, d), p[0].reshape(t, -1), w_up[0], w_down[0], ln2_g[0], ln2_b[0], w_ple_gate[0],
               b_ple_gate[0], w_ple[0], ln3_g[0], ln3_b[0], tm)
    return out.reshape(b, s, d)
```

```python
import functools
import math

import jax
import jax.numpy as jnp
from jax import lax
from jax.experimental import pallas as pl
from jax.experimental.pallas import tpu as pltpu

F32 = jnp.float32
BF16 = jnp.bfloat16

MLA_HEADS = 8
QK_NOPE_DIM = 64
QK_ROPE_DIM = 32
V_HEAD_DIM = 64
ROPE_THETA = 10000.0
SSM_GROUP = 16
LN_EPS = 1e-5
RMS_EPS = 1e-6
DEPTH = 1
DEEPNORM_ALPHA = (2.0 * DEPTH) ** 0.25

LANES = 128
MXU_DIM = 256
HEAD_PAD = LANES
BF16_ROWS = 16
V_AUG = V_HEAD_DIM + BF16_ROWS
SSM_SUB = MXU_DIM // SSM_GROUP
SSM_CHUNK = 64
SCAN_STEPS = 8
CHUNK_BLOCK = LANES
VMEM_LIMIT = 56 * 1024 * 1024


def _const_spec(shape):
    nd = len(shape)
    return pl.BlockSpec(shape, lambda *_: (0,) * nd, pipeline_mode=pl.Buffered(1))


def _layer_norm(x, g, b):
    mu = jnp.mean(x, axis=-1, keepdims=True)
    xc = x - mu
    var = jnp.mean(xc * xc, axis=-1, keepdims=True)
    return xc * lax.rsqrt(var + LN_EPS) * g + b


def _rms_norm(x, g):
    return x * lax.rsqrt(jnp.mean(x * x, axis=-1, keepdims=True) + RMS_EPS) * g


def _dot(a, b):
    return jnp.dot(a, b, preferred_element_type=F32)


def _dot_t(a, b):
    return lax.dot_general(a, b, (((1,), (1,)), ((), ())), preferred_element_type=F32)


def _in_proj_kernel(x_ref, pos_ref, lng_ref, lnb_ref, wa_ref, bg_ref, qg_ref, wq_ref, kvg_ref,
                    wk_ref, wv_ref, wu_ref, invf_ref,
                    q_ref, k_ref, v_ref, ut_ref, ga_ref, gb_ref, u_buf, *, cols, scale):
    ncb, sub, d_in = x_ref.shape
    hb = ncb // 2
    rows = hb * sub
    parts = range(2)
    cs = lambda p: slice(p * hb, (p + 1) * hb)
    tile = lambda v: v.reshape(hb, sub, v.shape[-1])
    h = [_layer_norm(x_ref[cs(p)].reshape(rows, d_in), lng_ref[...], lnb_ref[...]).astype(BF16) for p in parts]

    def proj(name):
        a, b = cols[name]
        return [_dot(h[p], wa_ref[:, a:b]) for p in parts]

    cq, ckv, kr = proj("cq"), proj("ckv"), proj("kr")

    one = jnp.ones((rows, QK_NOPE_DIM), F32)
    pad = HEAD_PAD - QK_NOPE_DIM - QK_ROPE_DIM
    half = QK_ROPE_DIM // 2
    zero = one * 0.0
    cos2, sin_lo, sin_hi, k_rope2 = [], [], [], []
    for p in parts:
        ang_t = invf_ref[...] * pos_ref[:, p * rows:(p + 1) * rows].astype(F32)
        cos_f, sin_f = jnp.cos(ang_t).T, jnp.sin(ang_t).T
        cos = jnp.concatenate([one, cos_f, cos_f, one[:, :pad]], axis=1)
        sin = jnp.concatenate([one * 0.0, sin_f, sin_f, one[:, :pad] * 0.0], axis=1)
        cos2.append(jnp.concatenate([cos, cos], axis=1))
        sin_lo.append(jnp.concatenate([zero, -sin_f, zero[:, :half + pad]], axis=1))
        sin_hi.append(jnp.concatenate([zero, zero[:, :half], sin_f, zero[:, :pad]], axis=1))
        k_rope = kr[p][:, :HEAD_PAD] * cos + kr[p][:, HEAD_PAD:] * sin
        k_rope2.append(jnp.concatenate([k_rope, k_rope], axis=1))

    cqn = [_rms_norm(cq[p], qg_ref[...]).astype(BF16) for p in parts]
    ckvn = [_rms_norm(ckv[p], kvg_ref[...]).astype(BF16) for p in parts]
    width = wk_ref.shape[1]
    for c in range(width // MXU_DIM):
        sl = slice(c * MXU_DIM, (c + 1) * MXU_DIM)
        q = [_dot(cqn[p], wq_ref[:, sl]) for p in parts]
        k = [_dot(ckvn[p], wk_ref[:, sl]) for p in parts]
        for p in parts:
            q_rot = jnp.concatenate(
                [pltpu.roll(q[p][:, hh * HEAD_PAD:(hh + 1) * HEAD_PAD], HEAD_PAD - half, axis=1) * sin_lo[p]
                 + pltpu.roll(q[p][:, hh * HEAD_PAD:(hh + 1) * HEAD_PAD], half, axis=1) * sin_hi[p]
                 for hh in range(MXU_DIM // HEAD_PAD)], axis=1)
            q_ref[cs(p), :, sl] = tile(((q[p] * cos2[p] + q_rot) * scale).astype(BF16))
            k_ref[cs(p), :, sl] = tile((k[p] + k_rope2[p]).astype(BF16))
    v = [_dot(ckvn[p], wv_ref[...]) for p in parts]
    u = [_dot(h[p], wu_ref[...]) for p in parts]
    ga, gb = proj("ga"), proj("gb")
    d = ga_ref.shape[-1]
    for p in parts:
        v_ref[cs(p)] = tile(v[p].astype(BF16))
        for c in range(u_buf.shape[0]):
            u_buf[c, p * rows:(p + 1) * rows, :] = u[p][:, c * LANES:(c + 1) * LANES]
        ga_ref[cs(p)] = tile(jax.nn.sigmoid(ga[p] + bg_ref[:, :d]).astype(BF16))
        gb_ref[cs(p)] = tile(jax.nn.sigmoid(gb[p] + bg_ref[:, d:]).astype(BF16))
    n = ut_ref.shape[1] // sub
    gl = LANES // n
    for j in range(sub):
        for c in range(u_buf.shape[0]):
            u_j = u_buf[c, pl.ds(j, ncb, stride=sub), :].T
            ut_ref[c * gl:(c + 1) * gl, j * n:(j + 1) * n, :] = u_j.astype(BF16).reshape(gl, n, ncb)


def _rope_swap(w):
    half = w.shape[-1] // 2
    return jnp.concatenate([-w[..., half:], w[..., :half]], axis=-1)


def _head_slot(nope, rope):
    k, h, _ = rope.shape
    parts = [nope if nope is not None else jnp.zeros((k, h, QK_NOPE_DIM), rope.dtype), rope]
    used = QK_NOPE_DIM + QK_ROPE_DIM
    parts.append(jnp.zeros((k, h, HEAD_PAD - used), rope.dtype))
    return jnp.concatenate(parts, axis=-1).reshape(k, h * HEAD_PAD)


def _in_proj(x4, positions, ln_g, ln_b, w_in, b_gate, q_norm_g, w_uq, kv_norm_g, w_ukv):
    nc, slabs, sub, d = x4.shape
    ncb = CHUNK_BLOCK
    pos4 = positions.reshape(nc // ncb, ncb, slabs, sub).transpose(0, 2, 1, 3).reshape(nc // ncb, slabs, 1, ncb * sub)
    q_lora = q_norm_g.shape[0]
    kv_lora = kv_norm_g.shape[0]
    ssm_w = w_in.shape[1] - (q_lora + kv_lora + QK_ROPE_DIM + 2 * d)
    heads = MLA_HEADS

    o = 0
    w_cq = w_in[:, o:o + q_lora]; o += q_lora
    w_ckv = w_in[:, o:o + kv_lora]; o += kv_lora
    w_kr = w_in[:, o:o + QK_ROPE_DIM]; o += QK_ROPE_DIM
    w_u = w_in[:, o:o + ssm_w]; o += ssm_w
    w_ga = w_in[:, o:o + d]; o += d
    w_gb = w_in[:, o:o + d]
    kr_slot = _head_slot(None, w_kr[:, None, :])
    kr_sw_slot = _head_slot(None, _rope_swap(w_kr)[:, None, :])
    pieces = [("cq", w_cq), ("ckv", w_ckv), ("kr", jnp.concatenate([kr_slot, kr_sw_slot], axis=1)),
              ("ga", w_ga), ("gb", w_gb)]
    cols, o = {}, 0
    for name, w in pieces:
        cols[name] = (o, o + w.shape[1])
        o += w.shape[1]
    w_a = jnp.concatenate([w for _, w in pieces], axis=1).astype(BF16)

    wq = w_uq.reshape(q_lora, heads, QK_NOPE_DIM + QK_ROPE_DIM)
    wq_nope, wq_rope = wq[..., :QK_NOPE_DIM], wq[..., QK_NOPE_DIM:]
    w_q = _head_slot(wq_nope, wq_rope).astype(BF16)
    wkv = w_ukv.reshape(kv_lora, heads, QK_NOPE_DIM + V_HEAD_DIM)
    w_k = _head_slot(wkv[..., :QK_NOPE_DIM], jnp.zeros((kv_lora, heads, QK_ROPE_DIM), F32)).astype(BF16)
    w_v = wkv[..., QK_NOPE_DIM:].reshape(kv_lora, heads * V_HEAD_DIM).astype(BF16)
    w_u = w_u.astype(BF16)

    inv_freq = ROPE_THETA ** (-jnp.arange(0, QK_ROPE_DIM, 2, dtype=F32) / QK_ROPE_DIM)
    invf = inv_freq[:, None]

    hw = heads * HEAD_PAD
    hv = heads * V_HEAD_DIM
    groups = ssm_w // SSM_GROUP
    slab = lambda w: pl.BlockSpec((ncb, None, sub, w), lambda i, c: (c, i, 0, 0))
    scale = (QK_NOPE_DIM + QK_ROPE_DIM) ** -0.5 * math.log2(math.e)
    nat = lambda w: jax.ShapeDtypeStruct((nc, slabs, sub, w), BF16)
    return pl.pallas_call(
        functools.partial(_in_proj_kernel, cols=cols, scale=scale),
        grid=(slabs, nc // ncb),
        in_specs=[slab(d), pl.BlockSpec((None, None, 1, ncb * sub), lambda i, c: (c, i, 0, 0)),
                  _const_spec((1, d)), _const_spec((1, d)), _const_spec(w_a.shape),
                  _const_spec((1, 2 * d)), _const_spec((1, q_lora)), _const_spec(w_q.shape),
                  _const_spec((1, kv_lora)), _const_spec(w_k.shape), _const_spec(w_v.shape),
                  _const_spec(w_u.shape), _const_spec(invf.shape)],
        out_specs=[slab(hw), slab(hw), slab(hv),
                   pl.BlockSpec((groups, sub * SSM_GROUP, ncb), lambda i, c: (0, i, c)), slab(d), slab(d)],
        out_shape=[nat(hw), nat(hw), nat(hv),
                   jax.ShapeDtypeStruct((groups, slabs * sub * SSM_GROUP, nc), BF16), nat(d), nat(d)],
        scratch_shapes=[pltpu.VMEM((ssm_w // LANES, ncb * sub, LANES), F32)],
        compiler_params=pltpu.CompilerParams(dimension_semantics=("parallel", "parallel"),
                                             vmem_limit_bytes=VMEM_LIMIT),
        name="in_proj",
    )(x4, pos4, ln_g[None], ln_b[None], w_a, b_gate[None], q_norm_g[None], w_q, kv_norm_g[None],
      w_k, w_v, w_u, invf)


def _attn_kernel(idx_ref, q_ref, k_ref, v_ref, o_ref, s_buf, vt_buf, qt_buf, *, tq, tk):
    extra = (lax.broadcasted_iota(jnp.int32, (V_AUG - V_HEAD_DIM, tk), 0) == 0).astype(BF16)
    for j in range(vt_buf.shape[0]):
        v_t = v_ref[0, j * tk:(j + 1) * tk, :].T
        for hh in range(2):
            vt_buf[j, hh] = jnp.concatenate([v_t[hh * V_HEAD_DIM:(hh + 1) * V_HEAD_DIM], extra], axis=0)

    key = lax.broadcasted_iota(jnp.int32, (tk, tq), 0)
    qry = lax.broadcasted_iota(jnp.int32, (tk, tq), 1)
    heads = range(2)

    class QueryBlock:
        def __init__(self, i):
            self.side, self.i, self.quads = i % 2, i, idx_ref[i] // 2
            self.rows = slice(i * tq, (i + 1) * tq)

        def setup(self):
            for hh in heads:
                qt_buf[self.side, hh] = q_ref[0, self.rows, hh * HEAD_PAD:(hh + 1) * HEAD_PAD].T

        def scores(self, j, slot):
            start = pl.multiple_of(j * tk, tk)
            for hh in heads:
                s_buf[self.side, slot, hh] = _dot(k_ref[0, pl.ds(start, tk), hh * HEAD_PAD:(hh + 1) * HEAD_PAD],
                                                  qt_buf[self.side, hh])

        def update(self, j, slot, state, mask_offset):
            ps, stats = [], []
            for hh in heads:
                m, _ = state[hh]
                s = s_buf[self.side, slot, hh]
                if mask_offset is not None:
                    s = jnp.where(key + mask_offset <= qry, s, -1e30)
                m_new = jnp.maximum(m, jnp.max(s, axis=0, keepdims=True))
                stats.append((m_new, jnp.exp2(m - m_new)))
                ps.append(jnp.exp2(s - m_new).astype(BF16))
            out = []
            for hh in heads:
                m_new, a = stats[hh]
                out.append((m_new, a * state[hh][1] + _dot(vt_buf[j, hh], ps[hh])))
            return tuple(out)

        def pair(self, p, state):
            blk = 2 * p
            self.scores(blk + 1, 1)
            state = self.update(blk, 0, state, None)
            self.scores(blk + 2, 0)
            return self.update(blk + 1, 1, state, None)

        def unmasked(self):
            i = self.i
            init = tuple((jnp.full((1, tq), -1e30, F32), jnp.zeros((V_AUG, tq), F32)) for _ in heads)
            state = lax.fori_loop(0, self.quads, lambda qi, st: self.pair(2 * qi + 1, self.pair(2 * qi, st)), init)
            return self.pair(i - 1, state) if i % 2 else state

        def finish(self, state):
            o_t = jnp.concatenate([acc[:V_HEAD_DIM] / acc[V_HEAD_DIM:V_HEAD_DIM + 1] for _, acc in state], axis=0)
            o_ref[0, self.rows, :] = o_t.T.astype(BF16)

    blocks = [QueryBlock(i) for i in range(q_ref.shape[1] // tq)]
    blocks[0].setup()
    blocks[0].scores(0, 0)
    for blk, nxt in zip(blocks, blocks[1:] + [None]):
        if nxt is not None:
            nxt.setup()
        state = blk.unmasked()
        diag = 2 * blk.i
        blk.scores(diag + 1, 1)
        state = blk.update(diag, 0, state, 0)
        if nxt is not None:
            nxt.scores(0, 0)
        blk.finish(blk.update(diag + 1, 1, state, tk))


def _attention(q, k, v, tq):
    b, s, _ = q.shape
    pairs = MLA_HEADS // 2
    tk = tq // 2
    nkv = s // tk
    grid_spec = pltpu.PrefetchScalarGridSpec(
        num_scalar_prefetch=1,
        grid=(b, pairs),
        in_specs=[pl.BlockSpec((1, s, 2 * HEAD_PAD), lambda bi, hp, idx: (bi, 0, hp)),
                  pl.BlockSpec((1, s, 2 * HEAD_PAD), lambda bi, hp, idx: (bi, 0, hp)),
                  pl.BlockSpec((1, s, 2 * V_HEAD_DIM), lambda bi, hp, idx: (bi, 0, hp))],
        out_specs=pl.BlockSpec((1, s, 2 * V_HEAD_DIM), lambda bi, hp, idx: (bi, 0, hp)),
        scratch_shapes=[pltpu.VMEM((2, 2, 2, tk, tq), F32), pltpu.VMEM((nkv, 2, V_AUG, tk), BF16),
                        pltpu.VMEM((2, 2, HEAD_PAD, tq), BF16)])
    return pl.pallas_call(
        functools.partial(_attn_kernel, tq=tq, tk=tk),
        grid_spec=grid_spec,
        out_shape=jax.ShapeDtypeStruct((b, s, MLA_HEADS * V_HEAD_DIM), BF16),
        compiler_params=pltpu.CompilerParams(
            dimension_semantics=("parallel", "parallel"), vmem_limit_bytes=VMEM_LIMIT),
        name="attention",
    )(jnp.arange(s // tq, dtype=jnp.int32), q, k, v)


def _gelu_tanh(x):
    k = -2.0 * math.sqrt(2.0 / math.pi) * math.log2(math.e)
    return x / (1.0 + jnp.exp2(x * (k + (k * 0.044715) * (x * x))))


def _s5_kernel(ut_ref, tt_ref, wt_ref, vt_ref, sc_ref, d_ref, y_ref, *, seq_chunks):
    nc = ut_ref.shape[2]
    tiles = range(tt_ref.shape[1])
    tile = lambda t: slice(t * MXU_DIM, (t + 1) * MXU_DIM)
    z = _dot(wt_ref[0], ut_ref[0])
    local = []
    for t in tiles:
        acc = _dot(tt_ref[0, t], ut_ref[0, tile(0), :])
        for s in range(1, t + 1):
            acc += _dot(tt_ref[0, t - s], ut_ref[0, tile(s), :])
        local.append(acc)

    half = z.shape[0] // 2
    pos = lax.broadcasted_iota(jnp.int32, (1, nc), 1) % seq_chunks
    x = z
    for k in range(seq_chunks.bit_length() - 1):
        sh = 1 << k
        a = sc_ref[0, :, k:k + 1]
        a_re, a_im = a[:half], a[half:]
        xs = jnp.where(pos >= sh, pltpu.roll(x, sh, axis=1), 0.0)
        xs_re, xs_im = xs[:half], xs[half:]
        x = x + jnp.concatenate([a_re * xs_re - a_im * xs_im, a_re * xs_im + a_im * xs_re], axis=0)
    x_in = jnp.where(pos >= 1, pltpu.roll(x, 1, axis=1), 0.0).astype(BF16)

    carried = [_dot(vt_ref[0, tile(t), :], x_in) for t in tiles]
    for t in tiles:
        y = local[t] + carried[t] + d_ref[0, tile(t), :] * ut_ref[0, tile(t), :].astype(F32)
        y_ref[0, tile(t), :] = _gelu_tanh(y).astype(BF16)


def _s5_param_kernel(are_ref, aim_ref, ldt_ref, btr_ref, bti_ref, cr_ref, ci_ref,
                     tt_ref, wt_ref, vt_ref, sc_ref, *, chunk):
    n, p = cr_ref.shape[1:]
    hi = lax.Precision.HIGHEST
    dt = jnp.exp(ldt_ref[0])
    lam_re = jnp.minimum(are_ref[0], -1e-4)
    lam_im = aim_ref[0]
    twice = lambda x: jnp.concatenate([x, x], axis=1)
    rate, freq = twice(lam_re) * dt, twice(lam_im) * dt
    first = lax.broadcasted_iota(jnp.int32, (1, 2 * p), 1) < p

    def powers(j):
        ang = freq * j
        return jnp.exp(rate * j) * jnp.where(first, jnp.cos(ang), jnp.sin(ang))

    j0 = lax.broadcasted_iota(jnp.int32, (chunk, 1), 0).astype(F32)
    abar = powers(jnp.ones((1, 1), F32))
    k_idx = lax.broadcasted_iota(jnp.int32, (sc_ref.shape[2], 1), 0).astype(F32)
    sc_ref[0] = powers(jnp.exp2(k_idx) * chunk).T
    nr, ni = abar[:, :p] - 1.0, abar[:, p:]
    den = lam_re * lam_re + lam_im * lam_im
    f_re = (nr * lam_re + ni * lam_im) / den
    f_im = (ni * lam_re - nr * lam_im) / den
    bt_re = f_re * btr_ref[0] - f_im * bti_ref[0]
    bt_im = f_re * bti_ref[0] + f_im * btr_ref[0]

    rows = chunk * n

    def per_step(pw):
        x = jnp.broadcast_to(pw[:, None, :], (chunk, n, 2 * p)).reshape(rows, 2 * p)
        return x, pltpu.roll(x, p, axis=1)

    def per_chan(x):
        return jnp.broadcast_to(twice(x)[None], (chunk, n, 2 * p)).reshape(rows, 2 * p)

    x, x_sw = per_step(powers(j0 + 1.0))
    a, b = per_chan(cr_ref[0]) * x, per_chan(ci_ref[0]) * x_sw
    vt_ref[0] = jnp.where(first, a - b, -(a + b)).astype(BF16)
    x, x_sw = per_step(powers(chunk - 1.0 - j0))
    a, b = x * per_chan(bt_re), x_sw * per_chan(bt_im)
    w = jnp.where(first, a - b, a + b)
    wt_ref[0] = w.T.astype(BF16)

    c_cat = jnp.concatenate([cr_ref[0], -ci_ref[0]], axis=1)
    k_rev = lax.dot_general(c_cat, w, (((1,), (1,)), ((), ())), precision=hi, preferred_element_type=F32)
    sub = SSM_SUB
    k_rev = jnp.concatenate([k_rev, jnp.zeros((n, (sub - 1) * n), F32)], axis=1)
    for d in range(chunk // sub):
        for t in range(sub):
            off = (chunk - 1 - sub * d - t) * n
            tt_ref[0, d, t * n:(t + 1) * n, :] = k_rev[:, off:off + sub * n].astype(BF16)


def _s5_weights(a_re, a_im, log_dt, b_re, b_im, c_re, c_im, chunk):
    g, p = a_re.shape
    n = b_re.shape[-1]
    tiles, tw, cw = chunk // SSM_SUB, SSM_SUB * n, chunk * n
    grp = lambda shape: pl.BlockSpec((1,) + shape, lambda i: (i,) + (0,) * len(shape))
    return pl.pallas_call(
        functools.partial(_s5_param_kernel, chunk=chunk),
        grid=(g,),
        in_specs=[grp((1, p)), grp((1, p)), grp((1, 1)), grp((n, p)), grp((n, p)), grp((n, p)), grp((n, p))],
        out_specs=[grp((tiles, tw, tw)), grp((2 * p, cw)), grp((cw, 2 * p)), grp((2 * p, SCAN_STEPS))],
        out_shape=[jax.ShapeDtypeStruct((g, tiles, tw, tw), BF16), jax.ShapeDtypeStruct((g, 2 * p, cw), BF16),
                   jax.ShapeDtypeStruct((g, cw, 2 * p), BF16), jax.ShapeDtypeStruct((g, 2 * p, SCAN_STEPS), F32)],
        compiler_params=pltpu.CompilerParams(dimension_semantics=("parallel",)),
        name="s5_params",
    )(a_re[:, None], a_im[:, None], log_dt[:, None, None], jnp.swapaxes(b_re, 1, 2),
      jnp.swapaxes(b_im, 1, 2), c_re, c_im)


def _s5(u_t, seq_chunks, a_re, a_im, log_dt, b_re, b_im, c_re, c_im, d_skip):
    g, cw, nc = u_t.shape
    n = SSM_GROUP
    chunk = cw // n
    assert seq_chunks & (seq_chunks - 1) == 0 and seq_chunks <= 1 << SCAN_STEPS
    tt, wt, vt, sc = _s5_weights(a_re, a_im, log_dt, b_re, b_im, c_re, c_im, chunk)
    dcol = jnp.tile(d_skip.reshape(g, 1, n), (1, chunk, 1)).reshape(g, cw, 1)
    grp = lambda shape: pl.BlockSpec((1,) + shape, lambda i: (i,) + (0,) * len(shape))
    return pl.pallas_call(
        functools.partial(_s5_kernel, seq_chunks=seq_chunks),
        grid=(g,),
        in_specs=[grp((cw, nc)), grp(tt.shape[1:]), grp(wt.shape[1:]), grp(vt.shape[1:]), grp(sc.shape[1:]),
                  grp((cw, 1))],
        out_specs=grp((cw, nc)),
        out_shape=jax.ShapeDtypeStruct((g, cw, nc), BF16),
        compiler_params=pltpu.CompilerParams(dimension_semantics=("parallel",),
                                             vmem_limit_bytes=VMEM_LIMIT),
        name="s5",
    )(u_t, tt, wt, vt, sc, dcol)


def _merge_kernel(x_ref, attn_ref, y_ref, ga_ref, gb_ref, lng_ref, lnb_ref, wglu_ref, bglu_ref,
                  wsb_ref, wab_ref, wo_ref, g1_ref, b1_ref, o_ref, y_buf):
    ncb, sub, d = x_ref.shape
    n = y_ref.shape[1] // sub
    gl = LANES // n
    for j in range(sub):
        for c in range(y_buf.shape[0]):
            y_j = y_ref[c * gl:(c + 1) * gl, j * n:(j + 1) * n, :].reshape(LANES, ncb).astype(F32)
            y_buf[c, pl.ds(j, ncb, stride=sub), :] = y_j.T
    hb = ncb // 2
    parts = range(2)
    flat = lambda ref, p: ref[p * hb:(p + 1) * hb].reshape(hb * sub, ref.shape[-1])
    ys = [jnp.concatenate([y_buf[c, p * hb * sub:(p + 1) * hb * sub, :] for c in range(y_buf.shape[0])], axis=1)
          for p in parts]
    gate = [_dot(ys[p].astype(BF16), wglu_ref[...]) for p in parts]
    branch_a = [_dot(flat(attn_ref, p), wab_ref[...]) for p in parts]
    yg = [(ys[p] * jax.nn.sigmoid(gate[p] + bglu_ref[...])).astype(BF16) for p in parts]
    branch_b = [_dot(yg[p], wsb_ref[...]) for p in parts]
    merged = [(flat(ga_ref, p).astype(F32) * branch_a[p] + flat(gb_ref, p).astype(F32) * branch_b[p]).astype(BF16)
              for p in parts]
    mixed = [_dot(merged[p], wo_ref[...]) for p in parts]
    for p in parts:
        h = _layer_norm(flat(x_ref, p), lng_ref[...], lnb_ref[...])
        r = DEEPNORM_ALPHA * h + mixed[p]
        o_ref[p * hb:(p + 1) * hb] = _layer_norm(r, g1_ref[...], b1_ref[...]).reshape(hb, sub, d)


def _merge(x4, attn4, y_t, ga4, gb4, ln_g, ln_b, w_glu, b_glu, w_ssm_br, w_attn_br, w_o, ln1_g, ln1_b):
    nc, slabs, sub, d = x4.shape
    groups, cw, _ = y_t.shape
    ncb = CHUNK_BLOCK
    slab = lambda a: pl.BlockSpec((ncb, None, sub, a.shape[-1]), lambda i, c: (c, i, 0, 0))
    vec = lambda v: v[None]
    ws = [w_glu.astype(BF16), vec(b_glu), w_ssm_br.astype(BF16), w_attn_br.astype(BF16),
          w_o.astype(BF16), vec(ln1_g), vec(ln1_b)]
    return pl.pallas_call(
        _merge_kernel,
        grid=(slabs, nc // ncb),
        in_specs=[slab(x4), slab(attn4), pl.BlockSpec((groups, cw // slabs, ncb), lambda i, c: (0, i, c)),
                  slab(ga4), slab(gb4), _const_spec((1, d)), _const_spec((1, d))]
                 + [_const_spec(w.shape) for w in ws],
        out_specs=slab(x4),
        out_shape=jax.ShapeDtypeStruct(x4.shape, F32),
        scratch_shapes=[pltpu.VMEM((groups * SSM_GROUP // LANES, ncb * sub, LANES), F32)],
        compiler_params=pltpu.CompilerParams(dimension_semantics=("parallel", "parallel"),
                                             vmem_limit_bytes=VMEM_LIMIT),
        name="merge",
    )(x4, attn4, y_t, ga4, gb4, vec(ln_g), vec(ln_b), *ws)


def _ffn_kernel(h_ref, p_ref, wup_ref, wdn_ref, g2_ref, b2_ref, wpg_ref, bpg_ref, wple_ref,
                g3_ref, b3_ref, o_ref, *, ff_chunk):
    half = h_ref.shape[0] // 2
    parts = range(2)
    rs = lambda p: slice(p * half, (p + 1) * half)
    h = [h_ref[rs(p), :] for p in parts]
    hb = [h[p].astype(BF16) for p in parts]
    ff = [jnp.zeros_like(h[p]) for p in parts]
    for c in range(wup_ref.shape[1] // ff_chunk):
        cs = slice(c * ff_chunk, (c + 1) * ff_chunk)
        up = [jnp.maximum(_dot(hb[p], wup_ref[:, cs]), 0.0) for p in parts]
        ff = [ff[p] + _dot((up[p] * up[p]).astype(BF16), wdn_ref[cs, :]) for p in parts]
    ple = [_dot(p_ref[rs(p), :].astype(BF16), wple_ref[...]) for p in parts]
    h2 = [_layer_norm(DEEPNORM_ALPHA * h[p] + ff[p], g2_ref[...], b2_ref[...]) for p in parts]
    gate = [_dot(h2[p].astype(BF16), wpg_ref[...]) for p in parts]
    for p in parts:
        r = DEEPNORM_ALPHA * h2[p] + jax.nn.sigmoid(gate[p] + bpg_ref[...]) * ple[p]
        o_ref[rs(p), :] = _layer_norm(r, g3_ref[...], b3_ref[...])


def _ffn(h1, p2, w_up, w_down, ln2_g, ln2_b, w_pg, b_pg, w_ple, ln3_g, ln3_b, tm):
    t, d = h1.shape
    row = lambda w: pl.BlockSpec((tm, w), lambda i: (i, 0))
    vec = lambda v: v[None]
    ws = [w_up.astype(BF16), w_down.astype(BF16), vec(ln2_g), vec(ln2_b), w_pg.astype(BF16), vec(b_pg),
          w_ple.astype(BF16), vec(ln3_g), vec(ln3_b)]
    return pl.pallas_call(
        functools.partial(_ffn_kernel, ff_chunk=1024),
        grid=(t // tm,),
        in_specs=[row(d), row(p2.shape[1])] + [_const_spec(w.shape) for w in ws],
        out_specs=row(d),
        out_shape=jax.ShapeDtypeStruct((t, d), F32),
        compiler_params=pltpu.CompilerParams(dimension_semantics=("parallel",),
                                             vmem_limit_bytes=VMEM_LIMIT),
        name="ffn",
    )(h1, p2, *ws)


def kernel(x, p, positions, ln_in_g, ln_in_b, w_in, b_gate, q_norm_g, w_uq, kv_norm_g, w_ukv, w_attn_br, a_re, a_im, log_dt, b_re, b_im, c_re, c_im, d_skip, w_glu, b_glu, w_ssm_br, w_o, ln1_g, ln1_b, w_up, w_down, ln2_g, ln2_b, w_ple_gate, b_ple_gate, w_ple, ln3_g, ln3_b):
    assert w_in.shape[0] == DEPTH
    b, s, d = x.shape
    t = b * s
    chunk = SSM_CHUNK
    seq_chunks = s // chunk
    nc = b * seq_chunks
    tm = min(1024, s)
    tq = min(512, s)
    sub = 8
    slabs = chunk // sub
    view = lambda a: a.reshape(nc, slabs, sub, a.shape[-1])
    x4 = view(x)

    q, k, v, u_t, ga, gb = _in_proj(x4, positions, ln_in_g, ln_in_b, w_in[0], b_gate[0],
                                    q_norm_g[0], w_uq[0], kv_norm_g[0], w_ukv[0])
    attn = _attention(q.reshape(b, s, -1), k.reshape(b, s, -1), v.reshape(b, s, -1), tq)
    y_t = _s5(u_t, seq_chunks, a_re[0], a_im[0], log_dt[0], b_re[0], b_im[0], c_re[0], c_im[0], d_skip[0])
    h1 = _merge(x4, view(attn), y_t, ga, gb, ln_in_g, ln_in_b, w_glu[0], b_glu[0], w_ssm_br[0],
                w_attn_br[0], w_o[0], ln1_g[0], ln1_b[0])
    out = _ffn(h1.reshape(t, d), p[0].reshape(t, -1), w_up[0], w_down[0], ln2_g[0], ln2_b[0], w_ple_gate[0],
               b_ple_gate[0], w_ple[0], ln3_g[0], ln3_b[0], tm)
    return out.reshape(b, s, d)
```

```python
import functools
import math

import jax
import jax.numpy as jnp
from jax import lax
from jax.experimental import pallas as pl
from jax.experimental.pallas import tpu as pltpu

F32 = jnp.float32
BF16 = jnp.bfloat16

MLA_HEADS = 8
QK_NOPE_DIM = 64
QK_ROPE_DIM = 32
V_HEAD_DIM = 64
ROPE_THETA = 10000.0
SSM_GROUP = 16
LN_EPS = 1e-5
RMS_EPS = 1e-6
DEPTH = 1
DEEPNORM_ALPHA = (2.0 * DEPTH) ** 0.25

LANES = 128
MXU_DIM = 256
HEAD_PAD = LANES
BF16_ROWS = 16
V_AUG = V_HEAD_DIM + BF16_ROWS
SSM_SUB = MXU_DIM // SSM_GROUP
SSM_CHUNK = 64
SCAN_STEPS = 8
CHUNK_BLOCK = LANES
VMEM_LIMIT = 56 * 1024 * 1024
MASKED = -1e30
FF_CHUNK = 1024


def _const_spec(shape):
    nd = len(shape)
    return pl.BlockSpec(shape, lambda *_: (0,) * nd, pipeline_mode=pl.Buffered(1))


def _layer_norm(x, g, b):
    mu = jnp.mean(x, axis=-1, keepdims=True)
    xc = x - mu
    var = jnp.mean(xc * xc, axis=-1, keepdims=True)
    return xc * lax.rsqrt(var + LN_EPS) * g + b


def _rms_norm(x, g):
    return x * lax.rsqrt(jnp.mean(x * x, axis=-1, keepdims=True) + RMS_EPS) * g


def _dot(a, b):
    return jnp.dot(a, b, preferred_element_type=F32)


def _dot_t(a, b):
    return lax.dot_general(a, b, (((1,), (1,)), ((), ())), preferred_element_type=F32)


def _in_proj_kernel(x_ref, pos_ref, lng_ref, lnb_ref, wa_ref, bg_ref, qg_ref, wq_ref, kvg_ref,
                    wk_ref, wv_ref, wu_ref, invf_ref,
                    q_ref, k_ref, v_ref, ut_ref, ga_ref, gb_ref, u_buf, *, cols, scale):
    ncb, sub, d_in = x_ref.shape
    hb = ncb // 2
    rows = hb * sub
    parts = range(2)
    cs = lambda p: slice(p * hb, (p + 1) * hb)
    tile = lambda v: v.reshape(hb, sub, v.shape[-1])
    h = [_layer_norm(x_ref[cs(p)].reshape(rows, d_in), lng_ref[...], lnb_ref[...]).astype(BF16) for p in parts]

    def proj(name):
        a, b = cols[name]
        return [_dot(h[p], wa_ref[:, a:b]) for p in parts]

    cq, ckv, kr = proj("cq"), proj("ckv"), proj("kr")

    one = jnp.ones((rows, QK_NOPE_DIM), F32)
    pad = HEAD_PAD - QK_NOPE_DIM - QK_ROPE_DIM
    half = QK_ROPE_DIM // 2
    zero = one * 0.0
    cos2, sin_lo, sin_hi, k_rope2 = [], [], [], []
    for p in parts:
        ang_t = invf_ref[...] * pos_ref[:, p * rows:(p + 1) * rows].astype(F32)
        cos_f, sin_f = jnp.cos(ang_t).T, jnp.sin(ang_t).T
        cos = jnp.concatenate([one, cos_f, cos_f, one[:, :pad]], axis=1)
        sin = jnp.concatenate([one * 0.0, sin_f, sin_f, one[:, :pad] * 0.0], axis=1)
        cos2.append(jnp.concatenate([cos, cos], axis=1))
        sin_lo.append(jnp.concatenate([zero, -sin_f, zero[:, :half + pad]], axis=1))
        sin_hi.append(jnp.concatenate([zero, zero[:, :half], sin_f, zero[:, :pad]], axis=1))
        k_rope = kr[p][:, :HEAD_PAD] * cos + kr[p][:, HEAD_PAD:] * sin
        k_rope2.append(jnp.concatenate([k_rope, k_rope], axis=1))

    cqn = [_rms_norm(cq[p], qg_ref[...]).astype(BF16) for p in parts]
    ckvn = [_rms_norm(ckv[p], kvg_ref[...]).astype(BF16) for p in parts]
    width = wk_ref.shape[1]
    for c in range(width // MXU_DIM):
        sl = slice(c * MXU_DIM, (c + 1) * MXU_DIM)
        q = [_dot(cqn[p], wq_ref[:, sl]) for p in parts]
        k = [_dot(ckvn[p], wk_ref[:, sl]) for p in parts]
        for p in parts:
            q_rot = jnp.concatenate(
                [pltpu.roll(q[p][:, hh * HEAD_PAD:(hh + 1) * HEAD_PAD], HEAD_PAD - half, axis=1) * sin_lo[p]
                 + pltpu.roll(q[p][:, hh * HEAD_PAD:(hh + 1) * HEAD_PAD], half, axis=1) * sin_hi[p]
                 for hh in range(MXU_DIM // HEAD_PAD)], axis=1)
            q_ref[cs(p), :, sl] = tile(((q[p] * cos2[p] + q_rot) * scale).astype(BF16))
            k_ref[cs(p), :, sl] = tile((k[p] + k_rope2[p]).astype(BF16))
    v = [_dot(ckvn[p], wv_ref[...]) for p in parts]
    u = [_dot(h[p], wu_ref[...]) for p in parts]
    ga, gb = proj("ga"), proj("gb")
    d = ga_ref.shape[-1]
    for p in parts:
        v_ref[cs(p)] = tile(v[p].astype(BF16))
        for c in range(u_buf.shape[0]):
            u_buf[c, p * rows:(p + 1) * rows, :] = u[p][:, c * LANES:(c + 1) * LANES]
        ga_ref[cs(p)] = tile(jax.nn.sigmoid(ga[p] + bg_ref[:, :d]).astype(BF16))
        gb_ref[cs(p)] = tile(jax.nn.sigmoid(gb[p] + bg_ref[:, d:]).astype(BF16))
    n = ut_ref.shape[1] // sub
    gl = LANES // n
    for j in range(sub):
        for c in range(u_buf.shape[0]):
            u_j = u_buf[c, pl.ds(j, ncb, stride=sub), :].T
            ut_ref[c * gl:(c + 1) * gl, j * n:(j + 1) * n, :] = u_j.astype(BF16).reshape(gl, n, ncb)


def _rope_swap(w):
    half = w.shape[-1] // 2
    return jnp.concatenate([-w[..., half:], w[..., :half]], axis=-1)


def _head_slot(nope, rope):
    k, h, _ = rope.shape
    parts = [nope if nope is not None else jnp.zeros((k, h, QK_NOPE_DIM), rope.dtype), rope]
    used = QK_NOPE_DIM + QK_ROPE_DIM
    parts.append(jnp.zeros((k, h, HEAD_PAD - used), rope.dtype))
    return jnp.concatenate(parts, axis=-1).reshape(k, h * HEAD_PAD)


def _in_proj(x4, positions, ln_g, ln_b, w_in, b_gate, q_norm_g, w_uq, kv_norm_g, w_ukv):
    nc, slabs, sub, d = x4.shape
    ncb = CHUNK_BLOCK
    pos4 = positions.reshape(nc // ncb, ncb, slabs, sub).transpose(0, 2, 1, 3).reshape(nc // ncb, slabs, 1, ncb * sub)
    q_lora = q_norm_g.shape[0]
    kv_lora = kv_norm_g.shape[0]
    ssm_w = w_in.shape[1] - (q_lora + kv_lora + QK_ROPE_DIM + 2 * d)
    heads = MLA_HEADS

    o = 0
    w_cq = w_in[:, o:o + q_lora]; o += q_lora
    w_ckv = w_in[:, o:o + kv_lora]; o += kv_lora
    w_kr = w_in[:, o:o + QK_ROPE_DIM]; o += QK_ROPE_DIM
    w_u = w_in[:, o:o + ssm_w]; o += ssm_w
    w_ga = w_in[:, o:o + d]; o += d
    w_gb = w_in[:, o:o + d]
    kr_slot = _head_slot(None, w_kr[:, None, :])
    kr_sw_slot = _head_slot(None, _rope_swap(w_kr)[:, None, :])
    pieces = [("cq", w_cq), ("ckv", w_ckv), ("kr", jnp.concatenate([kr_slot, kr_sw_slot], axis=1)),
              ("ga", w_ga), ("gb", w_gb)]
    cols, o = {}, 0
    for name, w in pieces:
        cols[name] = (o, o + w.shape[1])
        o += w.shape[1]
    w_a = jnp.concatenate([w for _, w in pieces], axis=1).astype(BF16)

    wq = w_uq.reshape(q_lora, heads, QK_NOPE_DIM + QK_ROPE_DIM)
    wq_nope, wq_rope = wq[..., :QK_NOPE_DIM], wq[..., QK_NOPE_DIM:]
    w_q = _head_slot(wq_nope, wq_rope).astype(BF16)
    wkv = w_ukv.reshape(kv_lora, heads, QK_NOPE_DIM + V_HEAD_DIM)
    w_k = _head_slot(wkv[..., :QK_NOPE_DIM], jnp.zeros((kv_lora, heads, QK_ROPE_DIM), F32)).astype(BF16)
    w_v = wkv[..., QK_NOPE_DIM:].reshape(kv_lora, heads * V_HEAD_DIM).astype(BF16)
    w_u = w_u.astype(BF16)

    inv_freq = ROPE_THETA ** (-jnp.arange(0, QK_ROPE_DIM, 2, dtype=F32) / QK_ROPE_DIM)
    invf = inv_freq[:, None]

    hw = heads * HEAD_PAD
    hv = heads * V_HEAD_DIM
    groups = ssm_w // SSM_GROUP
    slab = lambda w: pl.BlockSpec((ncb, None, sub, w), lambda i, c: (c, i, 0, 0))
    scale = (QK_NOPE_DIM + QK_ROPE_DIM) ** -0.5 * math.log2(math.e)
    nat = lambda w: jax.ShapeDtypeStruct((nc, slabs, sub, w), BF16)
    return pl.pallas_call(
        functools.partial(_in_proj_kernel, cols=cols, scale=scale),
        grid=(slabs, nc // ncb),
        in_specs=[slab(d), pl.BlockSpec((None, None, 1, ncb * sub), lambda i, c: (c, i, 0, 0)),
                  _const_spec((1, d)), _const_spec((1, d)), _const_spec(w_a.shape),
                  _const_spec((1, 2 * d)), _const_spec((1, q_lora)), _const_spec(w_q.shape),
                  _const_spec((1, kv_lora)), _const_spec(w_k.shape), _const_spec(w_v.shape),
                  _const_spec(w_u.shape), _const_spec(invf.shape)],
        out_specs=[slab(hw), slab(hw), slab(hv),
                   pl.BlockSpec((groups, sub * SSM_GROUP, ncb), lambda i, c: (0, i, c)), slab(d), slab(d)],
        out_shape=[nat(hw), nat(hw), nat(hv),
                   jax.ShapeDtypeStruct((groups, slabs * sub * SSM_GROUP, nc), BF16), nat(d), nat(d)],
        scratch_shapes=[pltpu.VMEM((ssm_w // LANES, ncb * sub, LANES), F32)],
        compiler_params=pltpu.CompilerParams(dimension_semantics=("parallel", "parallel"),
                                             vmem_limit_bytes=VMEM_LIMIT),
        name="in_proj",
    )(x4, pos4, ln_g[None], ln_b[None], w_a, b_gate[None], q_norm_g[None], w_q, kv_norm_g[None],
      w_k, w_v, w_u, invf)


def _attn_kernel(idx_ref, q_ref, k_ref, v_ref, o_ref, s_buf, m_buf, vt_buf, qt_buf, *, tq, tk):
    extra = (lax.broadcasted_iota(jnp.int32, (V_AUG - V_HEAD_DIM, tk), 0) == 0).astype(BF16)
    for j in range(vt_buf.shape[0]):
        v_t = v_ref[0, j * tk:(j + 1) * tk, :].T
        for hh in range(2):
            vt_buf[j, hh] = jnp.concatenate([v_t[hh * V_HEAD_DIM:(hh + 1) * V_HEAD_DIM], extra], axis=0)

    key = lax.broadcasted_iota(jnp.int32, (tk, tq), 0)
    qry = lax.broadcasted_iota(jnp.int32, (tk, tq), 1)
    heads = range(2)

    class QueryBlock:
        def __init__(self, i):
            self.side, self.i, self.quads = i % 2, i, idx_ref[i] // 2
            self.rows = slice(i * tq, (i + 1) * tq)

        def setup(self):
            for hh in heads:
                qt_buf[self.side, hh] = q_ref[0, self.rows, hh * HEAD_PAD:(hh + 1) * HEAD_PAD].T

        def scores(self, j, slot):
            start = pl.multiple_of(j * tk, tk)
            for hh in heads:
                s = _dot(k_ref[0, pl.ds(start, tk), hh * HEAD_PAD:(hh + 1) * HEAD_PAD], qt_buf[self.side, hh])
                s_buf[self.side, slot, hh] = s
                m_buf[self.side, slot, hh] = jnp.max(s, axis=0, keepdims=True)

        def update(self, j, slot, state, mask_offset):
            ps, stats = [], []
            for hh in heads:
                m, _ = state[hh]
                s = s_buf[self.side, slot, hh]
                if mask_offset is not None:
                    s = jnp.where(key + mask_offset <= qry, s, MASKED)
                    m_new = jnp.maximum(m, jnp.max(s, axis=0, keepdims=True))
                else:
                    m_new = jnp.maximum(m, m_buf[self.side, slot, hh])
                stats.append((m_new, jnp.exp2(m - m_new)))
                ps.append(jnp.exp2(s - m_new).astype(BF16))
            out = []
            for hh in heads:
                m_new, a = stats[hh]
                out.append((m_new, a * state[hh][1] + _dot(vt_buf[j, hh], ps[hh])))
            return tuple(out)

        def pair(self, p, state):
            blk = 2 * p
            self.scores(blk + 1, 1)
            state = self.update(blk, 0, state, None)
            self.scores(blk + 2, 0)
            return self.update(blk + 1, 1, state, None)

        def unmasked(self):
            i = self.i
            init = tuple((jnp.full((1, tq), MASKED, F32), jnp.zeros((V_AUG, tq), F32)) for _ in heads)
            state = lax.fori_loop(0, self.quads, lambda qi, st: self.pair(2 * qi + 1, self.pair(2 * qi, st)), init)
            return self.pair(i - 1, state) if i % 2 else state

        def finish(self, state):
            o_t = jnp.concatenate([acc[:V_HEAD_DIM] / acc[V_HEAD_DIM:V_HEAD_DIM + 1] for _, acc in state], axis=0)
            o_ref[0, self.rows, :] = o_t.T.astype(BF16)

    blocks = [QueryBlock(i) for i in range(q_ref.shape[1] // tq)]
    blocks[0].setup()
    blocks[0].scores(0, 0)
    for blk, nxt in zip(blocks, blocks[1:] + [None]):
        if nxt is not None:
            nxt.setup()
        state = blk.unmasked()
        diag = 2 * blk.i
        blk.scores(diag + 1, 1)
        state = blk.update(diag, 0, state, 0)
        if nxt is not None:
            nxt.scores(0, 0)
        blk.finish(blk.update(diag + 1, 1, state, tk))


def _attention(q, k, v, tq):
    b, s, _ = q.shape
    pairs = MLA_HEADS // 2
    tk = tq // 2
    nkv = s // tk
    grid_spec = pltpu.PrefetchScalarGridSpec(
        num_scalar_prefetch=1,
        grid=(b, pairs),
        in_specs=[pl.BlockSpec((1, s, 2 * HEAD_PAD), lambda bi, hp, idx: (bi, 0, hp)),
                  pl.BlockSpec((1, s, 2 * HEAD_PAD), lambda bi, hp, idx: (bi, 0, hp)),
                  pl.BlockSpec((1, s, 2 * V_HEAD_DIM), lambda bi, hp, idx: (bi, 0, hp))],
        out_specs=pl.BlockSpec((1, s, 2 * V_HEAD_DIM), lambda bi, hp, idx: (bi, 0, hp)),
        scratch_shapes=[pltpu.VMEM((2, 2, 2, tk, tq), F32), pltpu.VMEM((2, 2, 2, 1, tq), F32),
                        pltpu.VMEM((nkv, 2, V_AUG, tk), BF16),
                        pltpu.VMEM((2, 2, HEAD_PAD, tq), BF16)])
    return pl.pallas_call(
        functools.partial(_attn_kernel, tq=tq, tk=tk),
        grid_spec=grid_spec,
        out_shape=jax.ShapeDtypeStruct((b, s, MLA_HEADS * V_HEAD_DIM), BF16),
        compiler_params=pltpu.CompilerParams(
            dimension_semantics=("parallel", "parallel"), vmem_limit_bytes=VMEM_LIMIT),
        name="attention",
    )(jnp.arange(s // tq, dtype=jnp.int32), q, k, v)


def _gelu_tanh(x):
    k = -2.0 * math.sqrt(2.0 / math.pi) * math.log2(math.e)
    return x / (1.0 + jnp.exp2(x * (k + (k * 0.044715) * (x * x))))


def _s5_kernel(ut_ref, tt_ref, wt_ref, vt_ref, sc_ref, d_ref, y_ref, *, seq_chunks):
    nc = ut_ref.shape[2]
    tiles = range(tt_ref.shape[1])
    tile = lambda t: slice(t * MXU_DIM, (t + 1) * MXU_DIM)
    z = _dot(wt_ref[0], ut_ref[0])
    local = []
    for t in tiles:
        acc = _dot(tt_ref[0, t], ut_ref[0, tile(0), :])
        for s in range(1, t + 1):
            acc += _dot(tt_ref[0, t - s], ut_ref[0, tile(s), :])
        local.append(acc)

    half = z.shape[0] // 2
    pos = lax.broadcasted_iota(jnp.int32, (1, nc), 1) % seq_chunks
    x = z
    for k in range(seq_chunks.bit_length() - 1):
        sh = 1 << k
        a = sc_ref[0, :, k:k + 1]
        a_re, a_im = a[:half], a[half:]
        xs = jnp.where(pos >= sh, pltpu.roll(x, sh, axis=1), 0.0)
        xs_re, xs_im = xs[:half], xs[half:]
        x = x + jnp.concatenate([a_re * xs_re - a_im * xs_im, a_re * xs_im + a_im * xs_re], axis=0)
    x_in = jnp.where(pos >= 1, pltpu.roll(x, 1, axis=1), 0.0).astype(BF16)

    carried = [_dot(vt_ref[0, tile(t), :], x_in) for t in tiles]
    for t in tiles:
        y = local[t] + carried[t] + d_ref[0, tile(t), :] * ut_ref[0, tile(t), :].astype(F32)
        y_ref[0, tile(t), :] = _gelu_tanh(y).astype(BF16)


def _s5_param_kernel(are_ref, aim_ref, ldt_ref, btr_ref, bti_ref, cr_ref, ci_ref,
                     tt_ref, wt_ref, vt_ref, sc_ref, *, chunk):
    n, p = cr_ref.shape[1:]
    hi = lax.Precision.HIGHEST
    dt = jnp.exp(ldt_ref[0])
    lam_re = jnp.minimum(are_ref[0], -1e-4)
    lam_im = aim_ref[0]
    twice = lambda x: jnp.concatenate([x, x], axis=1)
    rate, freq = twice(lam_re) * dt, twice(lam_im) * dt
    first = lax.broadcasted_iota(jnp.int32, (1, 2 * p), 1) < p

    def powers(j):
        ang = freq * j
        return jnp.exp(rate * j) * jnp.where(first, jnp.cos(ang), jnp.sin(ang))

    j0 = lax.broadcasted_iota(jnp.int32, (chunk, 1), 0).astype(F32)
    abar = powers(jnp.ones((1, 1), F32))
    k_idx = lax.broadcasted_iota(jnp.int32, (sc_ref.shape[2], 1), 0).astype(F32)
    sc_ref[0] = powers(jnp.exp2(k_idx) * chunk).T
    nr, ni = abar[:, :p] - 1.0, abar[:, p:]
    den = lam_re * lam_re + lam_im * lam_im
    f_re = (nr * lam_re + ni * lam_im) / den
    f_im = (ni * lam_re - nr * lam_im) / den
    bt_re = f_re * btr_ref[0] - f_im * bti_ref[0]
    bt_im = f_re * bti_ref[0] + f_im * btr_ref[0]

    rows = chunk * n

    def per_step(pw):
        x = jnp.broadcast_to(pw[:, None, :], (chunk, n, 2 * p)).reshape(rows, 2 * p)
        return x, pltpu.roll(x, p, axis=1)

    def per_chan(x):
        return jnp.broadcast_to(twice(x)[None], (chunk, n, 2 * p)).reshape(rows, 2 * p)

    x, x_sw = per_step(powers(j0 + 1.0))
    a, b = per_chan(cr_ref[0]) * x, per_chan(ci_ref[0]) * x_sw
    vt_ref[0] = jnp.where(first, a - b, -(a + b)).astype(BF16)
    x, x_sw = per_step(powers(chunk - 1.0 - j0))
    a, b = x * per_chan(bt_re), x_sw * per_chan(bt_im)
    w = jnp.where(first, a - b, a + b)
    wt_ref[0] = w.T.astype(BF16)

    c_cat = jnp.concatenate([cr_ref[0], -ci_ref[0]], axis=1)
    k_rev = lax.dot_general(c_cat, w, (((1,), (1,)), ((), ())), precision=hi, preferred_element_type=F32)
    sub = SSM_SUB
    k_rev = jnp.concatenate([k_rev, jnp.zeros((n, (sub - 1) * n), F32)], axis=1)
    for d in range(chunk // sub):
        for t in range(sub):
            off = (chunk - 1 - sub * d - t) * n
            tt_ref[0, d, t * n:(t + 1) * n, :] = k_rev[:, off:off + sub * n].astype(BF16)


def _s5_weights(a_re, a_im, log_dt, b_re, b_im, c_re, c_im, chunk):
    g, p = a_re.shape
    n = b_re.shape[-1]
    tiles, tw, cw = chunk // SSM_SUB, SSM_SUB * n, chunk * n
    grp = lambda shape: pl.BlockSpec((1,) + shape, lambda i: (i,) + (0,) * len(shape))
    return pl.pallas_call(
        functools.partial(_s5_param_kernel, chunk=chunk),
        grid=(g,),
        in_specs=[grp((1, p)), grp((1, p)), grp((1, 1)), grp((n, p)), grp((n, p)), grp((n, p)), grp((n, p))],
        out_specs=[grp((tiles, tw, tw)), grp((2 * p, cw)), grp((cw, 2 * p)), grp((2 * p, SCAN_STEPS))],
        out_shape=[jax.ShapeDtypeStruct((g, tiles, tw, tw), BF16), jax.ShapeDtypeStruct((g, 2 * p, cw), BF16),
                   jax.ShapeDtypeStruct((g, cw, 2 * p), BF16), jax.ShapeDtypeStruct((g, 2 * p, SCAN_STEPS), F32)],
        compiler_params=pltpu.CompilerParams(dimension_semantics=("parallel",)),
        name="s5_params",
    )(a_re[:, None], a_im[:, None], log_dt[:, None, None], jnp.swapaxes(b_re, 1, 2),
      jnp.swapaxes(b_im, 1, 2), c_re, c_im)


def _s5(u_t, seq_chunks, a_re, a_im, log_dt, b_re, b_im, c_re, c_im, d_skip):
    g, cw, nc = u_t.shape
    n = SSM_GROUP
    chunk = cw // n
    assert seq_chunks & (seq_chunks - 1) == 0 and seq_chunks <= 1 << SCAN_STEPS
    tt, wt, vt, sc = _s5_weights(a_re, a_im, log_dt, b_re, b_im, c_re, c_im, chunk)
    dcol = jnp.tile(d_skip.reshape(g, 1, n), (1, chunk, 1)).reshape(g, cw, 1)
    grp = lambda shape: pl.BlockSpec((1,) + shape, lambda i: (i,) + (0,) * len(shape))
    return pl.pallas_call(
        functools.partial(_s5_kernel, seq_chunks=seq_chunks),
        grid=(g,),
        in_specs=[grp((cw, nc)), grp(tt.shape[1:]), grp(wt.shape[1:]), grp(vt.shape[1:]), grp(sc.shape[1:]),
                  grp((cw, 1))],
        out_specs=grp((cw, nc)),
        out_shape=jax.ShapeDtypeStruct((g, cw, nc), BF16),
        compiler_params=pltpu.CompilerParams(dimension_semantics=("parallel",),
                                             vmem_limit_bytes=VMEM_LIMIT),
        name="s5",
    )(u_t, tt, wt, vt, sc, dcol)


def _merge_kernel(x_ref, attn_ref, y_ref, ga_ref, gb_ref, lng_ref, lnb_ref, wglu_ref, bglu_ref,
                  wsb_ref, wab_ref, wo_ref, g1_ref, b1_ref, o_ref, y_buf):
    ncb, sub, d = x_ref.shape
    n = y_ref.shape[1] // sub
    gl = LANES // n
    for j in range(sub):
        for c in range(y_buf.shape[0]):
            y_j = y_ref[c * gl:(c + 1) * gl, j * n:(j + 1) * n, :].reshape(LANES, ncb).astype(F32)
            y_buf[c, pl.ds(j, ncb, stride=sub), :] = y_j.T
    hb = ncb // 2
    parts = range(2)
    flat = lambda ref, p: ref[p * hb:(p + 1) * hb].reshape(hb * sub, ref.shape[-1])
    ys = [jnp.concatenate([y_buf[c, p * hb * sub:(p + 1) * hb * sub, :] for c in range(y_buf.shape[0])], axis=1)
          for p in parts]
    gate = [_dot(ys[p].astype(BF16), wglu_ref[...]) for p in parts]
    branch_a = [_dot(flat(attn_ref, p), wab_ref[...]) for p in parts]
    yg = [(ys[p] * jax.nn.sigmoid(gate[p] + bglu_ref[...])).astype(BF16) for p in parts]
    branch_b = [_dot(yg[p], wsb_ref[...]) for p in parts]
    merged = [(flat(ga_ref, p).astype(F32) * branch_a[p] + flat(gb_ref, p).astype(F32) * branch_b[p]).astype(BF16)
              for p in parts]
    mixed = [_dot(merged[p], wo_ref[...]) for p in parts]
    for p in parts:
        h = _layer_norm(flat(x_ref, p), lng_ref[...], lnb_ref[...])
        r = DEEPNORM_ALPHA * h + mixed[p]
        o_ref[p * hb:(p + 1) * hb] = _layer_norm(r, g1_ref[...], b1_ref[...]).reshape(hb, sub, d)


def _merge(x4, attn4, y_t, ga4, gb4, ln_g, ln_b, w_glu, b_glu, w_ssm_br, w_attn_br, w_o, ln1_g, ln1_b):
    nc, slabs, sub, d = x4.shape
    groups, cw, _ = y_t.shape
    ncb = CHUNK_BLOCK
    slab = lambda a: pl.BlockSpec((ncb, None, sub, a.shape[-1]), lambda i, c: (c, i, 0, 0))
    vec = lambda v: v[None]
    ws = [w_glu.astype(BF16), vec(b_glu), w_ssm_br.astype(BF16), w_attn_br.astype(BF16),
          w_o.astype(BF16), vec(ln1_g), vec(ln1_b)]
    return pl.pallas_call(
        _merge_kernel,
        grid=(slabs, nc // ncb),
        in_specs=[slab(x4), slab(attn4), pl.BlockSpec((groups, cw // slabs, ncb), lambda i, c: (0, i, c)),
                  slab(ga4), slab(gb4), _const_spec((1, d)), _const_spec((1, d))]
                 + [_const_spec(w.shape) for w in ws],
        out_specs=slab(x4),
        out_shape=jax.ShapeDtypeStruct(x4.shape, F32),
        scratch_shapes=[pltpu.VMEM((groups * SSM_GROUP // LANES, ncb * sub, LANES), F32)],
        compiler_params=pltpu.CompilerParams(dimension_semantics=("parallel", "parallel"),
                                             vmem_limit_bytes=VMEM_LIMIT),
        name="merge",
    )(x4, attn4, y_t, ga4, gb4, vec(ln_g), vec(ln_b), *ws)


def _ffn_kernel(h_ref, p_ref, wup_ref, wdn_ref, g2_ref, b2_ref, wpg_ref, bpg_ref, wple_ref,
                g3_ref, b3_ref, o_ref, *, ff_chunk):
    half = h_ref.shape[0] // 2
    parts = range(2)
    rs = lambda p: slice(p * half, (p + 1) * half)
    h = [h_ref[rs(p), :] for p in parts]
    hb = [h[p].astype(BF16) for p in parts]
    ff = [jnp.zeros_like(h[p]) for p in parts]
    for c in range(wup_ref.shape[1] // ff_chunk):
        cs = slice(c * ff_chunk, (c + 1) * ff_chunk)
        up = [jnp.maximum(_dot(hb[p], wup_ref[:, cs]), 0.0) for p in parts]
        ff = [ff[p] + _dot((up[p] * up[p]).astype(BF16), wdn_ref[cs, :]) for p in parts]
    ple = [_dot(p_ref[rs(p), :].astype(BF16), wple_ref[...]) for p in parts]
    h2 = [_layer_norm(DEEPNORM_ALPHA * h[p] + ff[p], g2_ref[...], b2_ref[...]) for p in parts]
    gate = [_dot(h2[p].astype(BF16), wpg_ref[...]) for p in parts]
    for p in parts:
        r = DEEPNORM_ALPHA * h2[p] + jax.nn.sigmoid(gate[p] + bpg_ref[...]) * ple[p]
        o_ref[rs(p), :] = _layer_norm(r, g3_ref[...], b3_ref[...])


def _ffn(h1, p2, w_up, w_down, ln2_g, ln2_b, w_pg, b_pg, w_ple, ln3_g, ln3_b, tm):
    t, d = h1.shape
    row = lambda w: pl.BlockSpec((tm, w), lambda i: (i, 0))
    vec = lambda v: v[None]
    ws = [w_up.astype(BF16), w_down.astype(BF16), vec(ln2_g), vec(ln2_b), w_pg.astype(BF16), vec(b_pg),
          w_ple.astype(BF16), vec(ln3_g), vec(ln3_b)]
    return pl.pallas_call(
        functools.partial(_ffn_kernel, ff_chunk=FF_CHUNK),
        grid=(t // tm,),
        in_specs=[row(d), row(p2.shape[1])] + [_const_spec(w.shape) for w in ws],
        out_specs=row(d),
        out_shape=jax.ShapeDtypeStruct((t, d), F32),
        compiler_params=pltpu.CompilerParams(dimension_semantics=("parallel",),
                                             vmem_limit_bytes=VMEM_LIMIT),
        name="ffn",
    )(h1, p2, *ws)


def kernel(x, p, positions, ln_in_g, ln_in_b, w_in, b_gate, q_norm_g, w_uq, kv_norm_g, w_ukv, w_attn_br, a_re, a_im, log_dt, b_re, b_im, c_re, c_im, d_skip, w_glu, b_glu, w_ssm_br, w_o, ln1_g, ln1_b, w_up, w_down, ln2_g, ln2_b, w_ple_gate, b_ple_gate, w_ple, ln3_g, ln3_b):
    assert w_in.shape[0] == DEPTH
    b, s, d = x.shape
    t = b * s
    chunk = SSM_CHUNK
    seq_chunks = s // chunk
    nc = b * seq_chunks
    tm = min(1024, s)
    tq = min(512, s)
    sub = 8
    slabs = chunk // sub
    view = lambda a: a.reshape(nc, slabs, sub, a.shape[-1])
    x4 = view(x)

    q, k, v, u_t, ga, gb = _in_proj(x4, positions, ln_in_g, ln_in_b, w_in[0], b_gate[0],
                                    q_norm_g[0], w_uq[0], kv_norm_g[0], w_ukv[0])
    attn = _attention(q.reshape(b, s, -1), k.reshape(b, s, -1), v.reshape(b, s, -1), tq)
    y_t = _s5(u_t, seq_chunks, a_re[0], a_im[0], log_dt[0], b_re[0], b_im[0], c_re[0], c_im[0], d_skip[0])
    h1 = _merge(x4, view(attn), y_t, ga, gb, ln_in_g, ln_in_b, w_glu[0], b_glu[0], w_ssm_br[0],
                w_attn_br[0], w_o[0], ln1_g[0], ln1_b[0])
    out = _ffn(h1.reshape(t, d), p[0].reshape(t, -1), w_up[0], w_down[0], ln2_g[0], ln2_b[0], w_ple_gate[0],
               b_ple_gate[0], w_ple[0], ln3_g[0], ln3_b[0], tm)
    return out.reshape(b, s, d)
```

```python
import functools
import math

import jax
import jax.numpy as jnp
from jax import lax
from jax.experimental import pallas as pl
from jax.experimental.pallas import tpu as pltpu

F32 = jnp.float32
BF16 = jnp.bfloat16

MLA_HEADS = 8
QK_NOPE_DIM = 64
QK_ROPE_DIM = 32
V_HEAD_DIM = 64
ROPE_THETA = 10000.0
SSM_GROUP = 16
LN_EPS = 1e-5
RMS_EPS = 1e-6
DEPTH = 1
DEEPNORM_ALPHA = (2.0 * DEPTH) ** 0.25

LANES = 128
MXU_DIM = 256
HEAD_PAD = LANES
BF16_ROWS = 16
V_AUG = V_HEAD_DIM + BF16_ROWS
SSM_SUB = MXU_DIM // SSM_GROUP
SSM_CHUNK = 64
SCAN_STEPS = 8
CHUNK_BLOCK = LANES
VMEM_LIMIT = 56 * 1024 * 1024
S5_GROUPS_PER_STEP = 2
MASKED = -1e30
FF_CHUNK = 1024


def _const_spec(shape):
    nd = len(shape)
    return pl.BlockSpec(shape, lambda *_: (0,) * nd, pipeline_mode=pl.Buffered(1))


def _layer_norm(x, g, b):
    mu = jnp.mean(x, axis=-1, keepdims=True)
    xc = x - mu
    var = jnp.mean(xc * xc, axis=-1, keepdims=True)
    return xc * lax.rsqrt(var + LN_EPS) * g + b


def _rms_norm(x, g):
    return x * lax.rsqrt(jnp.mean(x * x, axis=-1, keepdims=True) + RMS_EPS) * g


def _dot(a, b):
    return jnp.dot(a, b, preferred_element_type=F32)


def _in_proj_kernel(x_ref, pos_ref, lng_ref, lnb_ref, wcq_ref, wckv_ref, wkr_ref, wga_ref, wgb_ref, bg_ref,
                    qg_ref, wq_ref, kvg_ref, wk_ref, wv_ref, wu_ref, invf_ref,
                    q_ref, k_ref, v_ref, ut_ref, ga_ref, gb_ref, u_buf, *, scale):
    ncb, sub, d_in = x_ref.shape
    hb = ncb // 2
    rows = hb * sub
    parts = range(2)
    cs = lambda p: slice(p * hb, (p + 1) * hb)
    tile = lambda v: v.reshape(hb, sub, v.shape[-1])
    h = [_layer_norm(x_ref[cs(p)].reshape(rows, d_in), lng_ref[...], lnb_ref[...]).astype(BF16) for p in parts]

    proj = lambda w_ref: [_dot(h[p], w_ref[...]) for p in parts]
    cq, ckv, kr = proj(wcq_ref), proj(wckv_ref), proj(wkr_ref)

    one = jnp.ones((rows, QK_NOPE_DIM), F32)
    pad = HEAD_PAD - QK_NOPE_DIM - QK_ROPE_DIM
    half = QK_ROPE_DIM // 2
    zero = one * 0.0
    cos2, sin_lo, sin_hi, k_rope2 = [], [], [], []
    for p in parts:
        ang_t = invf_ref[...] * pos_ref[:, p * rows:(p + 1) * rows].astype(F32)
        cos_f, sin_f = jnp.cos(ang_t).T, jnp.sin(ang_t).T
        cos = jnp.concatenate([one, cos_f, cos_f, one[:, :pad]], axis=1)
        sin = jnp.concatenate([one * 0.0, sin_f, sin_f, one[:, :pad] * 0.0], axis=1)
        cos2.append(jnp.concatenate([cos, cos], axis=1))
        sin_lo.append(jnp.concatenate([zero, -sin_f, zero[:, :half + pad]], axis=1))
        sin_hi.append(jnp.concatenate([zero, zero[:, :half], sin_f, zero[:, :pad]], axis=1))
        k_rope = kr[p][:, :HEAD_PAD] * cos + kr[p][:, HEAD_PAD:] * sin
        k_rope2.append(jnp.concatenate([k_rope, k_rope], axis=1))

    cqn = [_rms_norm(cq[p], qg_ref[...]).astype(BF16) for p in parts]
    ckvn = [_rms_norm(ckv[p], kvg_ref[...]).astype(BF16) for p in parts]
    width = wk_ref.shape[1]
    for c in range(width // MXU_DIM):
        sl = slice(c * MXU_DIM, (c + 1) * MXU_DIM)
        q = [_dot(cqn[p], wq_ref[:, sl]) for p in parts]
        k = [_dot(ckvn[p], wk_ref[:, sl]) for p in parts]
        for p in parts:
            q_rot = jnp.concatenate(
                [pltpu.roll(q[p][:, hh * HEAD_PAD:(hh + 1) * HEAD_PAD], HEAD_PAD - half, axis=1) * sin_lo[p]
                 + pltpu.roll(q[p][:, hh * HEAD_PAD:(hh + 1) * HEAD_PAD], half, axis=1) * sin_hi[p]
                 for hh in range(MXU_DIM // HEAD_PAD)], axis=1)
            q_ref[cs(p), :, sl] = tile(((q[p] * cos2[p] + q_rot) * scale).astype(BF16))
            k_ref[cs(p), :, sl] = tile((k[p] + k_rope2[p]).astype(BF16))
    v = [_dot(ckvn[p], wv_ref[...]) for p in parts]
    u = [_dot(h[p], wu_ref[...]) for p in parts]
    ga, gb = proj(wga_ref), proj(wgb_ref)
    d = ga_ref.shape[-1]
    for p in parts:
        v_ref[cs(p)] = tile(v[p].astype(BF16))
        for c in range(u_buf.shape[0]):
            u_buf[c, p * rows:(p + 1) * rows, :] = u[p][:, c * LANES:(c + 1) * LANES]
        ga_ref[cs(p)] = tile(jax.nn.sigmoid(ga[p] + bg_ref[:, :d]).astype(BF16))
        gb_ref[cs(p)] = tile(jax.nn.sigmoid(gb[p] + bg_ref[:, d:]).astype(BF16))
    n = ut_ref.shape[1] // sub
    gl = LANES // n
    for j in range(sub):
        for c in range(u_buf.shape[0]):
            u_j = u_buf[c, pl.ds(j, ncb, stride=sub), :].T
            ut_ref[c * gl:(c + 1) * gl, j * n:(j + 1) * n, :] = u_j.astype(BF16).reshape(gl, n, ncb)


def _rope_swap(w):
    half = w.shape[-1] // 2
    return jnp.concatenate([-w[..., half:], w[..., :half]], axis=-1)


def _head_slot(nope, rope):
    k, h, _ = rope.shape
    parts = [nope if nope is not None else jnp.zeros((k, h, QK_NOPE_DIM), rope.dtype), rope]
    used = QK_NOPE_DIM + QK_ROPE_DIM
    parts.append(jnp.zeros((k, h, HEAD_PAD - used), rope.dtype))
    return jnp.concatenate(parts, axis=-1).reshape(k, h * HEAD_PAD)


def _in_proj(x4, positions, ln_g, ln_b, w_in, b_gate, q_norm_g, w_uq, kv_norm_g, w_ukv):
    nc, slabs, sub, d = x4.shape
    ncb = CHUNK_BLOCK
    pos4 = positions.reshape(nc // ncb, ncb, slabs, sub).transpose(0, 2, 1, 3).reshape(nc // ncb, slabs, 1, ncb * sub)
    q_lora = q_norm_g.shape[0]
    kv_lora = kv_norm_g.shape[0]
    ssm_w = w_in.shape[1] - (q_lora + kv_lora + QK_ROPE_DIM + 2 * d)
    heads = MLA_HEADS

    o = 0
    w_cq = w_in[:, o:o + q_lora]; o += q_lora
    w_ckv = w_in[:, o:o + kv_lora]; o += kv_lora
    w_kr = w_in[:, o:o + QK_ROPE_DIM]; o += QK_ROPE_DIM
    w_u = w_in[:, o:o + ssm_w]; o += ssm_w
    w_ga = w_in[:, o:o + d]; o += d
    w_gb = w_in[:, o:o + d]
    kr_slot = _head_slot(None, w_kr[:, None, :])
    kr_sw_slot = _head_slot(None, _rope_swap(w_kr)[:, None, :])
    w_kr = jnp.concatenate([kr_slot, kr_sw_slot], axis=1)
    w_h = [w.astype(BF16) for w in (w_cq, w_ckv, w_kr, w_ga, w_gb)]

    wq = w_uq.reshape(q_lora, heads, QK_NOPE_DIM + QK_ROPE_DIM)
    wq_nope, wq_rope = wq[..., :QK_NOPE_DIM], wq[..., QK_NOPE_DIM:]
    w_q = _head_slot(wq_nope, wq_rope).astype(BF16)
    wkv = w_ukv.reshape(kv_lora, heads, QK_NOPE_DIM + V_HEAD_DIM)
    w_k = _head_slot(wkv[..., :QK_NOPE_DIM], jnp.zeros((kv_lora, heads, QK_ROPE_DIM), F32)).astype(BF16)
    w_v = wkv[..., QK_NOPE_DIM:].reshape(kv_lora, heads * V_HEAD_DIM).astype(BF16)
    w_u = w_u.astype(BF16)

    inv_freq = ROPE_THETA ** (-jnp.arange(0, QK_ROPE_DIM, 2, dtype=F32) / QK_ROPE_DIM)
    invf = inv_freq[:, None]

    hw = heads * HEAD_PAD
    hv = heads * V_HEAD_DIM
    groups = ssm_w // SSM_GROUP
    slab = lambda w: pl.BlockSpec((ncb, None, sub, w), lambda i, c: (c, i, 0, 0))
    scale = (QK_NOPE_DIM + QK_ROPE_DIM) ** -0.5 * math.log2(math.e)
    nat = lambda w: jax.ShapeDtypeStruct((nc, slabs, sub, w), BF16)
    return pl.pallas_call(
        functools.partial(_in_proj_kernel, scale=scale),
        grid=(slabs, nc // ncb),
        in_specs=[slab(d), pl.BlockSpec((None, None, 1, ncb * sub), lambda i, c: (c, i, 0, 0)),
                  _const_spec((1, d)), _const_spec((1, d)), *[_const_spec(w.shape) for w in w_h],
                  _const_spec((1, 2 * d)), _const_spec((1, q_lora)), _const_spec(w_q.shape),
                  _const_spec((1, kv_lora)), _const_spec(w_k.shape), _const_spec(w_v.shape),
                  _const_spec(w_u.shape), _const_spec(invf.shape)],
        out_specs=[slab(hw), slab(hw), slab(hv),
                   pl.BlockSpec((groups, sub * SSM_GROUP, ncb), lambda i, c: (0, i, c)), slab(d), slab(d)],
        out_shape=[nat(hw), nat(hw), nat(hv),
                   jax.ShapeDtypeStruct((groups, slabs * sub * SSM_GROUP, nc), BF16), nat(d), nat(d)],
        scratch_shapes=[pltpu.VMEM((ssm_w // LANES, ncb * sub, LANES), F32)],
        compiler_params=pltpu.CompilerParams(dimension_semantics=("parallel", "parallel"),
                                             vmem_limit_bytes=VMEM_LIMIT),
        name="in_proj",
    )(x4, pos4, ln_g[None], ln_b[None], *w_h, b_gate[None], q_norm_g[None], w_q, kv_norm_g[None],
      w_k, w_v, w_u, invf)


def _attn_kernel(idx_ref, q_ref, k_ref, v_ref, o_ref, s_buf, m_buf, vt_buf, qt_buf, *, tq, tk):
    extra = (lax.broadcasted_iota(jnp.int32, (V_AUG - V_HEAD_DIM, tk), 0) == 0).astype(BF16)
    for j in range(vt_buf.shape[0]):
        v_t = v_ref[0, j * tk:(j + 1) * tk, :].T
        for hh in range(2):
            vt_buf[j, hh] = jnp.concatenate([v_t[hh * V_HEAD_DIM:(hh + 1) * V_HEAD_DIM], extra], axis=0)

    key = lax.broadcasted_iota(jnp.int32, (tk, tq), 0)
    qry = lax.broadcasted_iota(jnp.int32, (tk, tq), 1)
    heads = range(2)

    class QueryBlock:
        def __init__(self, i):
            self.side, self.i, self.quads = i % 2, i, idx_ref[i] // 2
            self.rows = slice(i * tq, (i + 1) * tq)

        def setup(self):
            for hh in heads:
                qt_buf[self.side, hh] = q_ref[0, self.rows, hh * HEAD_PAD:(hh + 1) * HEAD_PAD].T

        def scores(self, j, slot):
            start = pl.multiple_of(j * tk, tk)
            for hh in heads:
                s = _dot(k_ref[0, pl.ds(start, tk), hh * HEAD_PAD:(hh + 1) * HEAD_PAD], qt_buf[self.side, hh])
                s_buf[self.side, slot, hh] = s
                m_buf[self.side, slot, hh] = jnp.max(s, axis=0, keepdims=True)

        def update(self, j, slot, state, mask_offset):
            ps, stats = [], []
            for hh in heads:
                m, _ = state[hh]
                s = s_buf[self.side, slot, hh]
                if mask_offset is not None:
                    s = jnp.where(key + mask_offset <= qry, s, MASKED)
                    m_new = jnp.maximum(m, jnp.max(s, axis=0, keepdims=True))
                else:
                    m_new = jnp.maximum(m, m_buf[self.side, slot, hh])
                stats.append((m_new, jnp.exp2(m - m_new)))
                ps.append(jnp.exp2(s - m_new).astype(BF16))
            out = []
            for hh in heads:
                m_new, a = stats[hh]
                out.append((m_new, a * state[hh][1] + _dot(vt_buf[j, hh], ps[hh])))
            return tuple(out)

        def pair(self, p, state):
            blk = 2 * p
            self.scores(blk + 1, 1)
            state = self.update(blk, 0, state, None)
            self.scores(blk + 2, 0)
            return self.update(blk + 1, 1, state, None)

        def unmasked(self):
            i = self.i
            init = tuple((jnp.full((1, tq), MASKED, F32), jnp.zeros((V_AUG, tq), F32)) for _ in heads)
            state = lax.fori_loop(0, self.quads, lambda qi, st: self.pair(2 * qi + 1, self.pair(2 * qi, st)), init)
            return self.pair(i - 1, state) if i % 2 else state

        def finish(self, state):
            o_t = jnp.concatenate([acc[:V_HEAD_DIM] / acc[V_HEAD_DIM:V_HEAD_DIM + 1] for _, acc in state], axis=0)
            o_ref[0, self.rows, :] = o_t.T.astype(BF16)

    blocks = [QueryBlock(i) for i in range(q_ref.shape[1] // tq)]
    blocks[0].setup()
    blocks[0].scores(0, 0)
    for blk, nxt in zip(blocks, blocks[1:] + [None]):
        if nxt is not None:
            nxt.setup()
        state = blk.unmasked()
        diag = 2 * blk.i
        blk.scores(diag + 1, 1)
        state = blk.update(diag, 0, state, 0)
        if nxt is not None:
            nxt.scores(0, 0)
        blk.finish(blk.update(diag + 1, 1, state, tk))


def _attention(q, k, v, tq):
    b, s, _ = q.shape
    pairs = MLA_HEADS // 2
    tk = tq // 2
    nkv = s // tk
    grid_spec = pltpu.PrefetchScalarGridSpec(
        num_scalar_prefetch=1,
        grid=(b, pairs),
        in_specs=[pl.BlockSpec((1, s, 2 * HEAD_PAD), lambda bi, hp, idx: (bi, 0, hp)),
                  pl.BlockSpec((1, s, 2 * HEAD_PAD), lambda bi, hp, idx: (bi, 0, hp)),
                  pl.BlockSpec((1, s, 2 * V_HEAD_DIM), lambda bi, hp, idx: (bi, 0, hp))],
        out_specs=pl.BlockSpec((1, s, 2 * V_HEAD_DIM), lambda bi, hp, idx: (bi, 0, hp)),
        scratch_shapes=[pltpu.VMEM((2, 2, 2, tk, tq), F32), pltpu.VMEM((2, 2, 2, 1, tq), F32),
                        pltpu.VMEM((nkv, 2, V_AUG, tk), BF16),
                        pltpu.VMEM((2, 2, HEAD_PAD, tq), BF16)])
    return pl.pallas_call(
        functools.partial(_attn_kernel, tq=tq, tk=tk),
        grid_spec=grid_spec,
        out_shape=jax.ShapeDtypeStruct((b, s, MLA_HEADS * V_HEAD_DIM), BF16),
        compiler_params=pltpu.CompilerParams(
            dimension_semantics=("parallel", "parallel"), vmem_limit_bytes=VMEM_LIMIT),
        name="attention",
    )(jnp.arange(s // tq, dtype=jnp.int32), q, k, v)


def _gelu_tanh(x):
    k = -2.0 * math.sqrt(2.0 / math.pi) * math.log2(math.e)
    return x / (1.0 + jnp.exp2(x * (k + (k * 0.044715) * (x * x))))


def _s5_kernel(ut_ref, tt_ref, wt_ref, vt_ref, sc_ref, d_ref, y_ref, *, seq_chunks):
    groups = range(ut_ref.shape[0])
    nc = ut_ref.shape[2]
    tiles = range(tt_ref.shape[1])
    tile = lambda t: slice(t * MXU_DIM, (t + 1) * MXU_DIM)
    z = [_dot(wt_ref[g], ut_ref[g]) for g in groups]
    local = []
    for g in groups:
        for t in tiles:
            acc = _dot(tt_ref[g, t], ut_ref[g, tile(0), :])
            for s in range(1, t + 1):
                acc += _dot(tt_ref[g, t - s], ut_ref[g, tile(s), :])
            local.append(acc)

    half = z[0].shape[0] // 2
    pos = lax.broadcasted_iota(jnp.int32, (1, nc), 1) % seq_chunks
    x = z
    for k in range(seq_chunks.bit_length() - 1):
        sh = 1 << k
        nxt = []
        for g in groups:
            a = sc_ref[g, :, k:k + 1]
            a_re, a_im = a[:half], a[half:]
            xs = jnp.where(pos >= sh, pltpu.roll(x[g], sh, axis=1), 0.0)
            xs_re, xs_im = xs[:half], xs[half:]
            nxt.append(x[g] + jnp.concatenate([a_re * xs_re - a_im * xs_im, a_re * xs_im + a_im * xs_re], axis=0))
        x = nxt
    x_in = [jnp.where(pos >= 1, pltpu.roll(x[g], 1, axis=1), 0.0).astype(BF16) for g in groups]

    carried = [_dot(vt_ref[g, tile(t), :], x_in[g]) for g in groups for t in tiles]
    for g in groups:
        for t in tiles:
            n = g * len(tiles) + t
            y = local[n] + carried[n] + d_ref[g, tile(t), :] * ut_ref[g, tile(t), :].astype(F32)
            y_ref[g, tile(t), :] = _gelu_tanh(y).astype(BF16)


def _s5_param_kernel(are_ref, aim_ref, ldt_ref, btr_ref, bti_ref, cr_ref, ci_ref,
                     tt_ref, wt_ref, vt_ref, sc_ref, *, chunk):
    n, p = cr_ref.shape[1:]
    hi = lax.Precision.HIGHEST
    dt = jnp.exp(ldt_ref[0])
    lam_re = jnp.minimum(are_ref[0], -1e-4)
    lam_im = aim_ref[0]
    twice = lambda x: jnp.concatenate([x, x], axis=1)
    rate, freq = twice(lam_re) * dt, twice(lam_im) * dt
    first = lax.broadcasted_iota(jnp.int32, (1, 2 * p), 1) < p

    def powers(j):
        ang = freq * j
        return jnp.exp(rate * j) * jnp.where(first, jnp.cos(ang), jnp.sin(ang))

    j0 = lax.broadcasted_iota(jnp.int32, (chunk, 1), 0).astype(F32)
    abar = powers(jnp.ones((1, 1), F32))
    k_idx = lax.broadcasted_iota(jnp.int32, (sc_ref.shape[2], 1), 0).astype(F32)
    sc_ref[0] = powers(jnp.exp2(k_idx) * chunk).T
    nr, ni = abar[:, :p] - 1.0, abar[:, p:]
    den = lam_re * lam_re + lam_im * lam_im
    f_re = (nr * lam_re + ni * lam_im) / den
    f_im = (ni * lam_re - nr * lam_im) / den
    bt_re = f_re * btr_ref[0] - f_im * bti_ref[0]
    bt_im = f_re * bti_ref[0] + f_im * btr_ref[0]

    rows = chunk * n

    def per_step(pw):
        x = jnp.broadcast_to(pw[:, None, :], (chunk, n, 2 * p)).reshape(rows, 2 * p)
        return x, pltpu.roll(x, p, axis=1)

    def per_chan(x):
        return jnp.broadcast_to(twice(x)[None], (chunk, n, 2 * p)).reshape(rows, 2 * p)

    x, x_sw = per_step(powers(j0 + 1.0))
    a, b = per_chan(cr_ref[0]) * x, per_chan(ci_ref[0]) * x_sw
    vt_ref[0] = jnp.where(first, a - b, -(a + b)).astype(BF16)
    x, x_sw = per_step(powers(chunk - 1.0 - j0))
    a, b = x * per_chan(bt_re), x_sw * per_chan(bt_im)
    w = jnp.where(first, a - b, a + b)
    wt_ref[0] = w.T.astype(BF16)

    c_cat = jnp.concatenate([cr_ref[0], -ci_ref[0]], axis=1)
    k_rev = lax.dot_general(c_cat, w, (((1,), (1,)), ((), ())), precision=hi, preferred_element_type=F32)
    sub = SSM_SUB
    k_rev = jnp.concatenate([k_rev, jnp.zeros((n, (sub - 1) * n), F32)], axis=1)
    for d in range(chunk // sub):
        for t in range(sub):
            off = (chunk - 1 - sub * d - t) * n
            tt_ref[0, d, t * n:(t + 1) * n, :] = k_rev[:, off:off + sub * n].astype(BF16)


def _s5_weights(a_re, a_im, log_dt, b_re, b_im, c_re, c_im, chunk):
    g, p = a_re.shape
    n = b_re.shape[-1]
    tiles, tw, cw = chunk // SSM_SUB, SSM_SUB * n, chunk * n
    grp = lambda shape: pl.BlockSpec((1,) + shape, lambda i: (i,) + (0,) * len(shape))
    return pl.pallas_call(
        functools.partial(_s5_param_kernel, chunk=chunk),
        grid=(g,),
        in_specs=[grp((1, p)), grp((1, p)), grp((1, 1)), grp((n, p)), grp((n, p)), grp((n, p)), grp((n, p))],
        out_specs=[grp((tiles, tw, tw)), grp((2 * p, cw)), grp((cw, 2 * p)), grp((2 * p, SCAN_STEPS))],
        out_shape=[jax.ShapeDtypeStruct((g, tiles, tw, tw), BF16), jax.ShapeDtypeStruct((g, 2 * p, cw), BF16),
                   jax.ShapeDtypeStruct((g, cw, 2 * p), BF16), jax.ShapeDtypeStruct((g, 2 * p, SCAN_STEPS), F32)],
        compiler_params=pltpu.CompilerParams(dimension_semantics=("parallel",)),
        name="s5_params",
    )(a_re[:, None], a_im[:, None], log_dt[:, None, None], jnp.swapaxes(b_re, 1, 2),
      jnp.swapaxes(b_im, 1, 2), c_re, c_im)


def _s5(u_t, seq_chunks, a_re, a_im, log_dt, b_re, b_im, c_re, c_im, d_skip):
    g, cw, nc = u_t.shape
    n = SSM_GROUP
    chunk = cw // n
    assert seq_chunks & (seq_chunks - 1) == 0 and seq_chunks <= 1 << SCAN_STEPS
    tt, wt, vt, sc = _s5_weights(a_re, a_im, log_dt, b_re, b_im, c_re, c_im, chunk)
    dcol = jnp.tile(d_skip.reshape(g, 1, n), (1, chunk, 1)).reshape(g, cw, 1)
    gb = S5_GROUPS_PER_STEP
    grp = lambda shape: pl.BlockSpec((gb,) + shape, lambda i: (i,) + (0,) * len(shape))
    return pl.pallas_call(
        functools.partial(_s5_kernel, seq_chunks=seq_chunks),
        grid=(g // gb,),
        in_specs=[grp((cw, nc)), grp(tt.shape[1:]), grp(wt.shape[1:]), grp(vt.shape[1:]), grp(sc.shape[1:]),
                  grp((cw, 1))],
        out_specs=grp((cw, nc)),
        out_shape=jax.ShapeDtypeStruct((g, cw, nc), BF16),
        compiler_params=pltpu.CompilerParams(dimension_semantics=("parallel",),
                                             vmem_limit_bytes=VMEM_LIMIT),
        name="s5",
    )(u_t, tt, wt, vt, sc, dcol)


def _merge_kernel(x_ref, attn_ref, y_ref, ga_ref, gb_ref, lng_ref, lnb_ref, wglu_ref, bglu_ref,
                  wsb_ref, wab_ref, wo_ref, g1_ref, b1_ref, o_ref, y_buf):
    ncb, sub, d = x_ref.shape
    n = y_ref.shape[1] // sub
    gl = LANES // n
    for j in range(sub):
        for c in range(y_buf.shape[0]):
            y_j = y_ref[c * gl:(c + 1) * gl, j * n:(j + 1) * n, :].reshape(LANES, ncb).astype(F32)
            y_buf[c, pl.ds(j, ncb, stride=sub), :] = y_j.T
    hb = ncb // 2
    parts = range(2)
    flat = lambda ref, p: ref[p * hb:(p + 1) * hb].reshape(hb * sub, ref.shape[-1])
    ys = [jnp.concatenate([y_buf[c, p * hb * sub:(p + 1) * hb * sub, :] for c in range(y_buf.shape[0])], axis=1)
          for p in parts]
    gate = [_dot(ys[p].astype(BF16), wglu_ref[...]) for p in parts]
    branch_a = [_dot(flat(attn_ref, p), wab_ref[...]) for p in parts]
    yg = [(ys[p] * jax.nn.sigmoid(gate[p] + bglu_ref[...])).astype(BF16) for p in parts]
    branch_b = [_dot(yg[p], wsb_ref[...]) for p in parts]
    merged = [(flat(ga_ref, p).astype(F32) * branch_a[p] + flat(gb_ref, p).astype(F32) * branch_b[p]).astype(BF16)
              for p in parts]
    mixed = [_dot(merged[p], wo_ref[...]) for p in parts]
    for p in parts:
        h = _layer_norm(flat(x_ref, p), lng_ref[...], lnb_ref[...])
        r = DEEPNORM_ALPHA * h + mixed[p]
        o_ref[p * hb:(p + 1) * hb] = _layer_norm(r, g1_ref[...], b1_ref[...]).reshape(hb, sub, d)


def _merge(x4, attn4, y_t, ga4, gb4, ln_g, ln_b, w_glu, b_glu, w_ssm_br, w_attn_br, w_o, ln1_g, ln1_b):
    nc, slabs, sub, d = x4.shape
    groups, cw, _ = y_t.shape
    ncb = CHUNK_BLOCK
    slab = lambda a: pl.BlockSpec((ncb, None, sub, a.shape[-1]), lambda i, c: (c, i, 0, 0))
    vec = lambda v: v[None]
    ws = [w_glu.astype(BF16), vec(b_glu), w_ssm_br.astype(BF16), w_attn_br.astype(BF16),
          w_o.astype(BF16), vec(ln1_g), vec(ln1_b)]
    return pl.pallas_call(
        _merge_kernel,
        grid=(slabs, nc // ncb),
        in_specs=[slab(x4), slab(attn4), pl.BlockSpec((groups, cw // slabs, ncb), lambda i, c: (0, i, c)),
                  slab(ga4), slab(gb4), _const_spec((1, d)), _const_spec((1, d))]
                 + [_const_spec(w.shape) for w in ws],
        out_specs=slab(x4),
        out_shape=jax.ShapeDtypeStruct(x4.shape, F32),
        scratch_shapes=[pltpu.VMEM((groups * SSM_GROUP // LANES, ncb * sub, LANES), F32)],
        compiler_params=pltpu.CompilerParams(dimension_semantics=("parallel", "parallel"),
                                             vmem_limit_bytes=VMEM_LIMIT),
        name="merge",
    )(x4, attn4, y_t, ga4, gb4, vec(ln_g), vec(ln_b), *ws)


def _ffn_kernel(h_ref, p_ref, wup_ref, wdn_ref, g2_ref, b2_ref, wpg_ref, bpg_ref, wple_ref,
                g3_ref, b3_ref, o_ref, *, ff_chunk):
    half = h_ref.shape[0] // 2
    parts = range(2)
    rs = lambda p: slice(p * half, (p + 1) * half)
    h = [h_ref[rs(p), :] for p in parts]
    hb = [h[p].astype(BF16) for p in parts]
    ff = [jnp.zeros_like(h[p]) for p in parts]
    for c in range(wup_ref.shape[1] // ff_chunk):
        cs = slice(c * ff_chunk, (c + 1) * ff_chunk)
        up = [jnp.maximum(_dot(hb[p], wup_ref[:, cs]), 0.0) for p in parts]
        ff = [ff[p] + _dot((up[p] * up[p]).astype(BF16), wdn_ref[cs, :]) for p in parts]
    ple = [_dot(p_ref[rs(p), :].astype(BF16), wple_ref[...]) for p in parts]
    h2 = [_layer_norm(DEEPNORM_ALPHA * h[p] + ff[p], g2_ref[...], b2_ref[...]) for p in parts]
    gate = [_dot(h2[p].astype(BF16), wpg_ref[...]) for p in parts]
    for p in parts:
        r = DEEPNORM_ALPHA * h2[p] + jax.nn.sigmoid(gate[p] + bpg_ref[...]) * ple[p]
        o_ref[rs(p), :] = _layer_norm(r, g3_ref[...], b3_ref[...])


def _ffn(h1, p2, w_up, w_down, ln2_g, ln2_b, w_pg, b_pg, w_ple, ln3_g, ln3_b, tm):
    t, d = h1.shape
    row = lambda w: pl.BlockSpec((tm, w), lambda i: (i, 0))
    vec = lambda v: v[None]
    ws = [w_up.astype(BF16), w_down.astype(BF16), vec(ln2_g), vec(ln2_b), w_pg.astype(BF16), vec(b_pg),
          w_ple.astype(BF16), vec(ln3_g), vec(ln3_b)]
    return pl.pallas_call(
        functools.partial(_ffn_kernel, ff_chunk=FF_CHUNK),
        grid=(t // tm,),
        in_specs=[row(d), row(p2.shape[1])] + [_const_spec(w.shape) for w in ws],
        out_specs=row(d),
        out_shape=jax.ShapeDtypeStruct((t, d), F32),
        compiler_params=pltpu.CompilerParams(dimension_semantics=("parallel",),
                                             vmem_limit_bytes=VMEM_LIMIT),
        name="ffn",
    )(h1, p2, *ws)


def kernel(x, p, positions, ln_in_g, ln_in_b, w_in, b_gate, q_norm_g, w_uq, kv_norm_g, w_ukv, w_attn_br, a_re, a_im, log_dt, b_re, b_im, c_re, c_im, d_skip, w_glu, b_glu, w_ssm_br, w_o, ln1_g, ln1_b, w_up, w_down, ln2_g, ln2_b, w_ple_gate, b_ple_gate, w_ple, ln3_g, ln3_b):
    assert w_in.shape[0] == DEPTH
    b, s, d = x.shape
    t = b * s
    chunk = SSM_CHUNK
    seq_chunks = s // chunk
    nc = b * seq_chunks
    tm = min(1024, s)
    tq = min(512, s)
    sub = 8
    slabs = chunk // sub
    view = lambda a: a.reshape(nc, slabs, sub, a.shape[-1])
    x4 = view(x)

    q, k, v, u_t, ga, gb = _in_proj(x4, positions, ln_in_g, ln_in_b, w_in[0], b_gate[0],
                                    q_norm_g[0], w_uq[0], kv_norm_g[0], w_ukv[0])
    attn = _attention(q.reshape(b, s, -1), k.reshape(b, s, -1), v.reshape(b, s, -1), tq)
    y_t = _s5(u_t, seq_chunks, a_re[0], a_im[0], log_dt[0], b_re[0], b_im[0], c_re[0], c_im[0], d_skip[0])
    h1 = _merge(x4, view(attn), y_t, ga, gb, ln_in_g, ln_in_b, w_glu[0], b_glu[0], w_ssm_br[0],
                w_attn_br[0], w_o[0], ln1_g[0], ln1_b[0])
    out = _ffn(h1.reshape(t, d), p[0].reshape(t, -1), w_up[0], w_down[0], ln2_g[0], ln2_b[0], w_ple_gate[0],
               b_ple_gate[0], w_ple[0], ln3_g[0], ln3_b[0], tm)
    return out.reshape(b, s, d)
```

```python
import functools
import math

import jax
import jax.numpy as jnp
from jax import lax
from jax.experimental import pallas as pl
from jax.experimental.pallas import tpu as pltpu

F32 = jnp.float32
BF16 = jnp.bfloat16

MLA_HEADS = 8
QK_NOPE_DIM = 64
QK_ROPE_DIM = 32
V_HEAD_DIM = 64
ROPE_THETA = 10000.0
SSM_GROUP = 16
LN_EPS = 1e-5
RMS_EPS = 1e-6
DEPTH = 1
DEEPNORM_ALPHA = (2.0 * DEPTH) ** 0.25

LANES = 128
MXU_DIM = 256
HEAD_PAD = LANES
BF16_ROWS = 16
V_AUG = V_HEAD_DIM + BF16_ROWS
SSM_SUB = MXU_DIM // SSM_GROUP
SSM_CHUNK = 64
SCAN_STEPS = 8
CHUNK_BLOCK = LANES
VMEM_LIMIT = 56 * 1024 * 1024
S5_GROUPS_PER_STEP = 2
MASKED = -1e30
FF_CHUNK = 1024


def _const_spec(shape):
    nd = len(shape)
    return pl.BlockSpec(shape, lambda *_: (0,) * nd, pipeline_mode=pl.Buffered(1))


def _layer_norm(x, g, b):
    mu = jnp.mean(x, axis=-1, keepdims=True)
    xc = x - mu
    var = jnp.mean(xc * xc, axis=-1, keepdims=True)
    return xc * lax.rsqrt(var + LN_EPS) * g + b


def _rms_norm(x, g):
    return x * lax.rsqrt(jnp.mean(x * x, axis=-1, keepdims=True) + RMS_EPS) * g


def _dot(a, b):
    return jnp.dot(a, b, preferred_element_type=F32)


def _in_proj_kernel(x_ref, pos_ref, lng_ref, lnb_ref, wcq_ref, wckv_ref, wkr_ref, wga_ref, wgb_ref, bg_ref,
                    qg_ref, wq_ref, kvg_ref, wk_ref, wv_ref, wu_ref, invf_ref,
                    q_ref, k_ref, v_ref, ut_ref, ga_ref, gb_ref, u_buf, *, scale):
    ncb, sub, d_in = x_ref.shape
    hb = ncb // 2
    rows = hb * sub
    parts = range(2)
    cs = lambda p: slice(p * hb, (p + 1) * hb)
    tile = lambda v: v.reshape(hb, sub, v.shape[-1])
    h = [_layer_norm(x_ref[cs(p)].reshape(rows, d_in), lng_ref[...], lnb_ref[...]).astype(BF16) for p in parts]

    proj = lambda w_ref: [_dot(h[p], w_ref[...]) for p in parts]
    cq, ckv, kr = proj(wcq_ref), proj(wckv_ref), proj(wkr_ref)

    one = jnp.ones((rows, QK_NOPE_DIM), F32)
    pad = HEAD_PAD - QK_NOPE_DIM - QK_ROPE_DIM
    half = QK_ROPE_DIM // 2
    zero = one * 0.0
    cos2, sin_lo, sin_hi, k_rope2 = [], [], [], []
    for p in parts:
        ang_t = invf_ref[...] * pos_ref[:, p * rows:(p + 1) * rows].astype(F32)
        cos_f, sin_f = jnp.cos(ang_t).T, jnp.sin(ang_t).T
        cos = jnp.concatenate([one, cos_f, cos_f, one[:, :pad]], axis=1)
        sin = jnp.concatenate([one * 0.0, sin_f, sin_f, one[:, :pad] * 0.0], axis=1)
        cos2.append(jnp.concatenate([cos, cos], axis=1))
        sin_lo.append(jnp.concatenate([zero, -sin_f, zero[:, :half + pad]], axis=1))
        sin_hi.append(jnp.concatenate([zero, zero[:, :half], sin_f, zero[:, :pad]], axis=1))
        k_rope = kr[p][:, :HEAD_PAD] * cos + kr[p][:, HEAD_PAD:] * sin
        k_rope2.append(jnp.concatenate([k_rope, k_rope], axis=1))

    cqn = [_rms_norm(cq[p], qg_ref[...]).astype(BF16) for p in parts]
    ckvn = [_rms_norm(ckv[p], kvg_ref[...]).astype(BF16) for p in parts]
    width = wk_ref.shape[1]
    for c in range(width // MXU_DIM):
        sl = slice(c * MXU_DIM, (c + 1) * MXU_DIM)
        q = [_dot(cqn[p], wq_ref[:, sl]) for p in parts]
        k = [_dot(ckvn[p], wk_ref[:, sl]) for p in parts]
        for p in parts:
            q_rot = jnp.concatenate(
                [pltpu.roll(q[p][:, hh * HEAD_PAD:(hh + 1) * HEAD_PAD], HEAD_PAD - half, axis=1) * sin_lo[p]
                 + pltpu.roll(q[p][:, hh * HEAD_PAD:(hh + 1) * HEAD_PAD], half, axis=1) * sin_hi[p]
                 for hh in range(MXU_DIM // HEAD_PAD)], axis=1)
            q_ref[cs(p), :, sl] = tile(((q[p] * cos2[p] + q_rot) * scale).astype(BF16))
            k_ref[cs(p), :, sl] = tile((k[p] + k_rope2[p]).astype(BF16))
    v = [_dot(ckvn[p], wv_ref[...]) for p in parts]
    u = [_dot(h[p], wu_ref[...]) for p in parts]
    ga, gb = proj(wga_ref), proj(wgb_ref)
    d = ga_ref.shape[-1]
    for p in parts:
        v_ref[cs(p)] = tile(v[p].astype(BF16))
        for c in range(u_buf.shape[0]):
            u_buf[c, p * rows:(p + 1) * rows, :] = u[p][:, c * LANES:(c + 1) * LANES]
        ga_ref[cs(p)] = tile(jax.nn.sigmoid(ga[p] + bg_ref[:, :d]).astype(BF16))
        gb_ref[cs(p)] = tile(jax.nn.sigmoid(gb[p] + bg_ref[:, d:]).astype(BF16))
    n = ut_ref.shape[1] // sub
    gl = LANES // n
    for j in range(sub):
        for c in range(u_buf.shape[0]):
            u_j = u_buf[c, pl.ds(j, ncb, stride=sub), :].T
            ut_ref[c * gl:(c + 1) * gl, j * n:(j + 1) * n, :] = u_j.astype(BF16).reshape(gl, n, ncb)


def _rope_swap(w):
    half = w.shape[-1] // 2
    return jnp.concatenate([-w[..., half:], w[..., :half]], axis=-1)


def _head_slot(nope, rope):
    k, h, _ = rope.shape
    parts = [nope if nope is not None else jnp.zeros((k, h, QK_NOPE_DIM), rope.dtype), rope]
    used = QK_NOPE_DIM + QK_ROPE_DIM
    parts.append(jnp.zeros((k, h, HEAD_PAD - used), rope.dtype))
    return jnp.concatenate(parts, axis=-1).reshape(k, h * HEAD_PAD)


def _in_proj(x4, positions, ln_g, ln_b, w_in, b_gate, q_norm_g, w_uq, kv_norm_g, w_ukv):
    nc, slabs, sub, d = x4.shape
    ncb = CHUNK_BLOCK
    pos4 = positions.reshape(nc // ncb, ncb, slabs, sub).transpose(0, 2, 1, 3).reshape(nc // ncb, slabs, 1, ncb * sub)
    q_lora = q_norm_g.shape[0]
    kv_lora = kv_norm_g.shape[0]
    ssm_w = w_in.shape[1] - (q_lora + kv_lora + QK_ROPE_DIM + 2 * d)
    heads = MLA_HEADS

    o = 0
    w_cq = w_in[:, o:o + q_lora]; o += q_lora
    w_ckv = w_in[:, o:o + kv_lora]; o += kv_lora
    w_kr = w_in[:, o:o + QK_ROPE_DIM]; o += QK_ROPE_DIM
    w_u = w_in[:, o:o + ssm_w]; o += ssm_w
    w_ga = w_in[:, o:o + d]; o += d
    w_gb = w_in[:, o:o + d]
    kr_slot = _head_slot(None, w_kr[:, None, :])
    kr_sw_slot = _head_slot(None, _rope_swap(w_kr)[:, None, :])
    w_kr = jnp.concatenate([kr_slot, kr_sw_slot], axis=1)
    w_h = [w.astype(BF16) for w in (w_cq, w_ckv, w_kr, w_ga, w_gb)]

    wq = w_uq.reshape(q_lora, heads, QK_NOPE_DIM + QK_ROPE_DIM)
    wq_nope, wq_rope = wq[..., :QK_NOPE_DIM], wq[..., QK_NOPE_DIM:]
    w_q = _head_slot(wq_nope, wq_rope).astype(BF16)
    wkv = w_ukv.reshape(kv_lora, heads, QK_NOPE_DIM + V_HEAD_DIM)
    w_k = _head_slot(wkv[..., :QK_NOPE_DIM], jnp.zeros((kv_lora, heads, QK_ROPE_DIM), F32)).astype(BF16)
    w_v = wkv[..., QK_NOPE_DIM:].reshape(kv_lora, heads * V_HEAD_DIM).astype(BF16)
    w_u = w_u.astype(BF16)

    inv_freq = ROPE_THETA ** (-jnp.arange(0, QK_ROPE_DIM, 2, dtype=F32) / QK_ROPE_DIM)
    invf = inv_freq[:, None]

    hw = heads * HEAD_PAD
    hv = heads * V_HEAD_DIM
    groups = ssm_w // SSM_GROUP
    slab = lambda w: pl.BlockSpec((ncb, None, sub, w), lambda i, c: (c, i, 0, 0))
    scale = (QK_NOPE_DIM + QK_ROPE_DIM) ** -0.5 * math.log2(math.e)
    nat = lambda w: jax.ShapeDtypeStruct((nc, slabs, sub, w), BF16)
    return pl.pallas_call(
        functools.partial(_in_proj_kernel, scale=scale),
        grid=(slabs, nc // ncb),
        in_specs=[slab(d), pl.BlockSpec((None, None, 1, ncb * sub), lambda i, c: (c, i, 0, 0)),
                  _const_spec((1, d)), _const_spec((1, d)), *[_const_spec(w.shape) for w in w_h],
                  _const_spec((1, 2 * d)), _const_spec((1, q_lora)), _const_spec(w_q.shape),
                  _const_spec((1, kv_lora)), _const_spec(w_k.shape), _const_spec(w_v.shape),
                  _const_spec(w_u.shape), _const_spec(invf.shape)],
        out_specs=[slab(hw), slab(hw), slab(hv),
                   pl.BlockSpec((groups, sub * SSM_GROUP, ncb), lambda i, c: (0, i, c)), slab(d), slab(d)],
        out_shape=[nat(hw), nat(hw), nat(hv),
                   jax.ShapeDtypeStruct((groups, slabs * sub * SSM_GROUP, nc), BF16), nat(d), nat(d)],
        scratch_shapes=[pltpu.VMEM((ssm_w // LANES, ncb * sub, LANES), F32)],
        compiler_params=pltpu.CompilerParams(dimension_semantics=("parallel", "parallel"),
                                             vmem_limit_bytes=VMEM_LIMIT),
        name="in_proj",
    )(x4, pos4, ln_g[None], ln_b[None], *w_h, b_gate[None], q_norm_g[None], w_q, kv_norm_g[None],
      w_k, w_v, w_u, invf)


def _attn_kernel(idx_ref, q_ref, k_ref, v_ref, o_ref, s_buf, m_buf, vt_buf, qt_buf, *, tq, tk):
    extra = (lax.broadcasted_iota(jnp.int32, (V_AUG - V_HEAD_DIM, tk), 0) == 0).astype(BF16)
    for j in range(vt_buf.shape[0]):
        v_t = v_ref[0, j * tk:(j + 1) * tk, :].T
        for hh in range(2):
            vt_buf[j, hh] = jnp.concatenate([v_t[hh * V_HEAD_DIM:(hh + 1) * V_HEAD_DIM], extra], axis=0)

    key = lax.broadcasted_iota(jnp.int32, (tk, tq), 0)
    qry = lax.broadcasted_iota(jnp.int32, (tk, tq), 1)
    heads = range(2)

    class QueryBlock:
        def __init__(self, i):
            self.side, self.i, self.quads = i % 2, i, idx_ref[i] // 2
            self.rows = slice(i * tq, (i + 1) * tq)

        def setup(self):
            for hh in heads:
                qt_buf[self.side, hh] = q_ref[0, self.rows, hh * HEAD_PAD:(hh + 1) * HEAD_PAD].T

        def scores(self, j, slot):
            start = pl.multiple_of(j * tk, tk)
            for hh in heads:
                s = _dot(k_ref[0, pl.ds(start, tk), hh * HEAD_PAD:(hh + 1) * HEAD_PAD], qt_buf[self.side, hh])
                s_buf[self.side, slot, hh] = s
                m_buf[self.side, slot, hh] = jnp.max(s, axis=0, keepdims=True)

        def update(self, j, slot, state, mask_offset):
            ps, stats = [], []
            for hh in heads:
                m, _ = state[hh]
                s = s_buf[self.side, slot, hh]
                if mask_offset is not None:
                    s = jnp.where(key + mask_offset <= qry, s, MASKED)
                    m_new = jnp.maximum(m, jnp.max(s, axis=0, keepdims=True))
                else:
                    m_new = jnp.maximum(m, m_buf[self.side, slot, hh])
                stats.append((m_new, jnp.exp2(m - m_new)))
                ps.append(jnp.exp2(s - m_new).astype(BF16))
            out = []
            for hh in heads:
                m_new, a = stats[hh]
                out.append((m_new, a * state[hh][1] + _dot(vt_buf[j, hh], ps[hh])))
            return tuple(out)

        def pair(self, p, state):
            blk = 2 * p
            self.scores(blk + 1, 1)
            state = self.update(blk, 0, state, None)
            self.scores(blk + 2, 0)
            return self.update(blk + 1, 1, state, None)

        def unmasked(self):
            i = self.i
            init = tuple((jnp.full((1, tq), MASKED, F32), jnp.zeros((V_AUG, tq), F32)) for _ in heads)
            state = lax.fori_loop(0, self.quads, lambda qi, st: self.pair(2 * qi + 1, self.pair(2 * qi, st)), init)
            return self.pair(i - 1, state) if i % 2 else state

        def finish(self, state):
            o_t = jnp.concatenate([acc[:V_HEAD_DIM] / acc[V_HEAD_DIM:V_HEAD_DIM + 1] for _, acc in state], axis=0)
            o_ref[0, self.rows, :] = o_t.T.astype(BF16)

    blocks = [QueryBlock(i) for i in range(q_ref.shape[1] // tq)]
    blocks[0].setup()
    blocks[0].scores(0, 0)
    for blk, nxt in zip(blocks, blocks[1:] + [None]):
        if nxt is not None:
            nxt.setup()
        state = blk.unmasked()
        diag = 2 * blk.i
        blk.scores(diag + 1, 1)
        state = blk.update(diag, 0, state, 0)
        if nxt is not None:
            nxt.scores(0, 0)
        blk.finish(blk.update(diag + 1, 1, state, tk))


def _attention(q, k, v, tq):
    b, s, _ = q.shape
    pairs = MLA_HEADS // 2
    tk = tq // 2
    nkv = s // tk
    grid_spec = pltpu.PrefetchScalarGridSpec(
        num_scalar_prefetch=1,
        grid=(b, pairs),
        in_specs=[pl.BlockSpec((1, s, 2 * HEAD_PAD), lambda bi, hp, idx: (bi, 0, hp)),
                  pl.BlockSpec((1, s, 2 * HEAD_PAD), lambda bi, hp, idx: (bi, 0, hp)),
                  pl.BlockSpec((1, s, 2 * V_HEAD_DIM), lambda bi, hp, idx: (bi, 0, hp))],
        out_specs=pl.BlockSpec((1, s, 2 * V_HEAD_DIM), lambda bi, hp, idx: (bi, 0, hp)),
        scratch_shapes=[pltpu.VMEM((2, 2, 2, tk, tq), F32), pltpu.VMEM((2, 2, 2, 1, tq), F32),
                        pltpu.VMEM((nkv, 2, V_AUG, tk), BF16),
                        pltpu.VMEM((2, 2, HEAD_PAD, tq), BF16)])
    return pl.pallas_call(
        functools.partial(_attn_kernel, tq=tq, tk=tk),
        grid_spec=grid_spec,
        out_shape=jax.ShapeDtypeStruct((b, s, MLA_HEADS * V_HEAD_DIM), BF16),
        compiler_params=pltpu.CompilerParams(
            dimension_semantics=("parallel", "parallel"), vmem_limit_bytes=VMEM_LIMIT),
        name="attention",
    )(jnp.arange(s // tq, dtype=jnp.int32), q, k, v)


def _gelu_tanh(x):
    k = -2.0 * math.sqrt(2.0 / math.pi) * math.log2(math.e)
    return x / (1.0 + jnp.exp2(x * (k + (k * 0.044715) * (x * x))))


def _s5_kernel(ut_ref, tt_ref, wt_ref, vt_ref, sc_ref, d_ref, y_ref, *, seq_chunks):
    groups = range(ut_ref.shape[0])
    nc = ut_ref.shape[2]
    tiles = range(tt_ref.shape[1])
    tile = lambda t: slice(t * MXU_DIM, (t + 1) * MXU_DIM)
    z = [_dot(wt_ref[g], ut_ref[g]) for g in groups]
    local = []
    for g in groups:
        for t in tiles:
            acc = _dot(tt_ref[g, t], ut_ref[g, tile(0), :])
            for s in range(1, t + 1):
                acc += _dot(tt_ref[g, t - s], ut_ref[g, tile(s), :])
            local.append(acc)

    half = z[0].shape[0] // 2
    pos = lax.broadcasted_iota(jnp.int32, (1, nc), 1) % seq_chunks
    x = z
    for k in range(seq_chunks.bit_length() - 1):
        sh = 1 << k
        nxt = []
        for g in groups:
            a = sc_ref[g, :, k:k + 1]
            a_re, a_im = a[:half], a[half:]
            xs = jnp.where(pos >= sh, pltpu.roll(x[g], sh, axis=1), 0.0)
            xs_re, xs_im = xs[:half], xs[half:]
            nxt.append(x[g] + jnp.concatenate([a_re * xs_re - a_im * xs_im, a_re * xs_im + a_im * xs_re], axis=0))
        x = nxt
    x_in = [jnp.where(pos >= 1, pltpu.roll(x[g], 1, axis=1), 0.0).astype(BF16) for g in groups]

    carried = [_dot(vt_ref[g, tile(t), :], x_in[g]) for g in groups for t in tiles]
    for g in groups:
        d = d_ref[g]
        d = jnp.broadcast_to(d[None], (MXU_DIM // d.shape[0],) + d.shape).reshape(MXU_DIM, 1)
        for t in tiles:
            n = g * len(tiles) + t
            y = local[n] + carried[n] + d * ut_ref[g, tile(t), :].astype(F32)
            y_ref[g, tile(t), :] = _gelu_tanh(y).astype(BF16)


def _s5_param_kernel(are_ref, aim_ref, ldt_ref, btr_ref, bti_ref, cr_ref, ci_ref,
                     tt_ref, wt_ref, vt_ref, sc_ref, *, chunk):
    n, p = cr_ref.shape[1:]
    hi = lax.Precision.HIGHEST
    dt = jnp.exp(ldt_ref[0])
    lam_re = jnp.minimum(are_ref[0], -1e-4)
    lam_im = aim_ref[0]
    twice = lambda x: jnp.concatenate([x, x], axis=1)
    rate, freq = twice(lam_re) * dt, twice(lam_im) * dt
    first = lax.broadcasted_iota(jnp.int32, (1, 2 * p), 1) < p

    def powers(j):
        ang = freq * j
        return jnp.exp(rate * j) * jnp.where(first, jnp.cos(ang), jnp.sin(ang))

    j0 = lax.broadcasted_iota(jnp.int32, (chunk, 1), 0).astype(F32)
    abar = powers(jnp.ones((1, 1), F32))
    k_idx = lax.broadcasted_iota(jnp.int32, (sc_ref.shape[2], 1), 0).astype(F32)
    sc_ref[0] = powers(jnp.exp2(k_idx) * chunk).T
    nr, ni = abar[:, :p] - 1.0, abar[:, p:]
    den = lam_re * lam_re + lam_im * lam_im
    f_re = (nr * lam_re + ni * lam_im) / den
    f_im = (ni * lam_re - nr * lam_im) / den
    bt_re = f_re * btr_ref[0] - f_im * bti_ref[0]
    bt_im = f_re * bti_ref[0] + f_im * btr_ref[0]

    rows = chunk * n

    def per_step(pw):
        x = jnp.broadcast_to(pw[:, None, :], (chunk, n, 2 * p)).reshape(rows, 2 * p)
        return x, pltpu.roll(x, p, axis=1)

    def per_chan(x):
        return jnp.broadcast_to(twice(x)[None], (chunk, n, 2 * p)).reshape(rows, 2 * p)

    x, x_sw = per_step(powers(j0 + 1.0))
    a, b = per_chan(cr_ref[0]) * x, per_chan(ci_ref[0]) * x_sw
    vt_ref[0] = jnp.where(first, a - b, -(a + b)).astype(BF16)
    x, x_sw = per_step(powers(chunk - 1.0 - j0))
    a, b = x * per_chan(bt_re), x_sw * per_chan(bt_im)
    w = jnp.where(first, a - b, a + b)
    wt_ref[0] = w.T.astype(BF16)

    c_cat = jnp.concatenate([cr_ref[0], -ci_ref[0]], axis=1)
    k_rev = lax.dot_general(c_cat, w, (((1,), (1,)), ((), ())), precision=hi, preferred_element_type=F32)
    sub = SSM_SUB
    k_rev = jnp.concatenate([k_rev, jnp.zeros((n, (sub - 1) * n), F32)], axis=1)
    for d in range(chunk // sub):
        for t in range(sub):
            off = (chunk - 1 - sub * d - t) * n
            tt_ref[0, d, t * n:(t + 1) * n, :] = k_rev[:, off:off + sub * n].astype(BF16)


def _s5_weights(a_re, a_im, log_dt, b_re, b_im, c_re, c_im, chunk):
    g, p = a_re.shape
    n = b_re.shape[-1]
    tiles, tw, cw = chunk // SSM_SUB, SSM_SUB * n, chunk * n
    grp = lambda shape: pl.BlockSpec((1,) + shape, lambda i: (i,) + (0,) * len(shape))
    return pl.pallas_call(
        functools.partial(_s5_param_kernel, chunk=chunk),
        grid=(g,),
        in_specs=[grp((1, p)), grp((1, p)), grp((1, 1)), grp((n, p)), grp((n, p)), grp((n, p)), grp((n, p))],
        out_specs=[grp((tiles, tw, tw)), grp((2 * p, cw)), grp((cw, 2 * p)), grp((2 * p, SCAN_STEPS))],
        out_shape=[jax.ShapeDtypeStruct((g, tiles, tw, tw), BF16), jax.ShapeDtypeStruct((g, 2 * p, cw), BF16),
                   jax.ShapeDtypeStruct((g, cw, 2 * p), BF16), jax.ShapeDtypeStruct((g, 2 * p, SCAN_STEPS), F32)],
        compiler_params=pltpu.CompilerParams(dimension_semantics=("parallel",)),
        name="s5_params",
    )(a_re[:, None], a_im[:, None], log_dt[:, None, None], jnp.swapaxes(b_re, 1, 2),
      jnp.swapaxes(b_im, 1, 2), c_re, c_im)


def _s5(u_t, seq_chunks, a_re, a_im, log_dt, b_re, b_im, c_re, c_im, d_skip):
    g, cw, nc = u_t.shape
    n = SSM_GROUP
    chunk = cw // n
    assert seq_chunks & (seq_chunks - 1) == 0 and seq_chunks <= 1 << SCAN_STEPS
    tt, wt, vt, sc = _s5_weights(a_re, a_im, log_dt, b_re, b_im, c_re, c_im, chunk)
    gb = S5_GROUPS_PER_STEP
    grp = lambda shape: pl.BlockSpec((gb,) + shape, lambda i: (i,) + (0,) * len(shape))
    return pl.pallas_call(
        functools.partial(_s5_kernel, seq_chunks=seq_chunks),
        grid=(g // gb,),
        in_specs=[grp((cw, nc)), grp(tt.shape[1:]), grp(wt.shape[1:]), grp(vt.shape[1:]), grp(sc.shape[1:]),
                  grp((n, 1))],
        out_specs=grp((cw, nc)),
        out_shape=jax.ShapeDtypeStruct((g, cw, nc), BF16),
        compiler_params=pltpu.CompilerParams(dimension_semantics=("parallel",),
                                             vmem_limit_bytes=VMEM_LIMIT),
        name="s5",
    )(u_t, tt, wt, vt, sc, d_skip.reshape(g, n, 1))


def _merge_kernel(x_ref, attn_ref, y_ref, ga_ref, gb_ref, lng_ref, lnb_ref, wglu_ref, bglu_ref,
                  wsb_ref, wab_ref, wo_ref, g1_ref, b1_ref, o_ref, y_buf):
    ncb, sub, d = x_ref.shape
    n = y_ref.shape[1] // sub
    gl = LANES // n
    for j in range(sub):
        for c in range(y_buf.shape[0]):
            y_j = y_ref[c * gl:(c + 1) * gl, j * n:(j + 1) * n, :].reshape(LANES, ncb).astype(F32)
            y_buf[c, pl.ds(j, ncb, stride=sub), :] = y_j.T
    hb = ncb // 2
    parts = range(2)
    flat = lambda ref, p: ref[p * hb:(p + 1) * hb].reshape(hb * sub, ref.shape[-1])
    ys = [jnp.concatenate([y_buf[c, p * hb * sub:(p + 1) * hb * sub, :] for c in range(y_buf.shape[0])], axis=1)
          for p in parts]
    gate = [_dot(ys[p].astype(BF16), wglu_ref[...]) for p in parts]
    branch_a = [_dot(flat(attn_ref, p), wab_ref[...]) for p in parts]
    yg = [(ys[p] * jax.nn.sigmoid(gate[p] + bglu_ref[...])).astype(BF16) for p in parts]
    branch_b = [_dot(yg[p], wsb_ref[...]) for p in parts]
    merged = [(flat(ga_ref, p).astype(F32) * branch_a[p] + flat(gb_ref, p).astype(F32) * branch_b[p]).astype(BF16)
              for p in parts]
    mixed = [_dot(merged[p], wo_ref[...]) for p in parts]
    for p in parts:
        h = _layer_norm(flat(x_ref, p), lng_ref[...], lnb_ref[...])
        r = DEEPNORM_ALPHA * h + mixed[p]
        o_ref[p * hb:(p + 1) * hb] = _layer_norm(r, g1_ref[...], b1_ref[...]).reshape(hb, sub, d)


def _merge(x4, attn4, y_t, ga4, gb4, ln_g, ln_b, w_glu, b_glu, w_ssm_br, w_attn_br, w_o, ln1_g, ln1_b):
    nc, slabs, sub, d = x4.shape
    groups, cw, _ = y_t.shape
    ncb = CHUNK_BLOCK
    slab = lambda a: pl.BlockSpec((ncb, None, sub, a.shape[-1]), lambda i, c: (c, i, 0, 0))
    vec = lambda v: v[None]
    ws = [w_glu.astype(BF16), vec(b_glu), w_ssm_br.astype(BF16), w_attn_br.astype(BF16),
          w_o.astype(BF16), vec(ln1_g), vec(ln1_b)]
    return pl.pallas_call(
        _merge_kernel,
        grid=(slabs, nc // ncb),
        in_specs=[slab(x4), slab(attn4), pl.BlockSpec((groups, cw // slabs, ncb), lambda i, c: (0, i, c)),
                  slab(ga4), slab(gb4), _const_spec((1, d)), _const_spec((1, d))]
                 + [_const_spec(w.shape) for w in ws],
        out_specs=slab(x4),
        out_shape=jax.ShapeDtypeStruct(x4.shape, F32),
        scratch_shapes=[pltpu.VMEM((groups * SSM_GROUP // LANES, ncb * sub, LANES), F32)],
        compiler_params=pltpu.CompilerParams(dimension_semantics=("parallel", "parallel"),
                                             vmem_limit_bytes=VMEM_LIMIT),
        name="merge",
    )(x4, attn4, y_t, ga4, gb4, vec(ln_g), vec(ln_b), *ws)


def _ffn_kernel(h_ref, p_ref, wup_ref, wdn_ref, g2_ref, b2_ref, wpg_ref, bpg_ref, wple_ref,
                g3_ref, b3_ref, o_ref, *, ff_chunk):
    half = h_ref.shape[0] // 2
    parts = range(2)
    rs = lambda p: slice(p * half, (p + 1) * half)
    h = [h_ref[rs(p), :] for p in parts]
    hb = [h[p].astype(BF16) for p in parts]
    ff = [jnp.zeros_like(h[p]) for p in parts]
    for c in range(wup_ref.shape[1] // ff_chunk):
        cs = slice(c * ff_chunk, (c + 1) * ff_chunk)
        up = [jnp.maximum(_dot(hb[p], wup_ref[:, cs]), 0.0) for p in parts]
        ff = [ff[p] + _dot((up[p] * up[p]).astype(BF16), wdn_ref[cs, :]) for p in parts]
    ple = [_dot(p_ref[rs(p), :].astype(BF16), wple_ref[...]) for p in parts]
    h2 = [_layer_norm(DEEPNORM_ALPHA * h[p] + ff[p], g2_ref[...], b2_ref[...]) for p in parts]
    gate = [_dot(h2[p].astype(BF16), wpg_ref[...]) for p in parts]
    for p in parts:
        r = DEEPNORM_ALPHA * h2[p] + jax.nn.sigmoid(gate[p] + bpg_ref[...]) * ple[p]
        o_ref[rs(p), :] = _layer_norm(r, g3_ref[...], b3_ref[...])


def _ffn(h1, p2, w_up, w_down, ln2_g, ln2_b, w_pg, b_pg, w_ple, ln3_g, ln3_b, tm):
    t, d = h1.shape
    row = lambda w: pl.BlockSpec((tm, w), lambda i: (i, 0))
    vec = lambda v: v[None]
    ws = [w_up.astype(BF16), w_down.astype(BF16), vec(ln2_g), vec(ln2_b), w_pg.astype(BF16), vec(b_pg),
          w_ple.astype(BF16), vec(ln3_g), vec(ln3_b)]
    return pl.pallas_call(
        functools.partial(_ffn_kernel, ff_chunk=FF_CHUNK),
        grid=(t // tm,),
        in_specs=[row(d), row(p2.shape[1])] + [_const_spec(w.shape) for w in ws],
        out_specs=row(d),
        out_shape=jax.ShapeDtypeStruct((t, d), F32),
        compiler_params=pltpu.CompilerParams(dimension_semantics=("parallel",),
                                             vmem_limit_bytes=VMEM_LIMIT),
        name="ffn",
    )(h1, p2, *ws)


def kernel(x, p, positions, ln_in_g, ln_in_b, w_in, b_gate, q_norm_g, w_uq, kv_norm_g, w_ukv, w_attn_br, a_re, a_im, log_dt, b_re, b_im, c_re, c_im, d_skip, w_glu, b_glu, w_ssm_br, w_o, ln1_g, ln1_b, w_up, w_down, ln2_g, ln2_b, w_ple_gate, b_ple_gate, w_ple, ln3_g, ln3_b):
    assert w_in.shape[0] == DEPTH
    b, s, d = x.shape
    t = b * s
    chunk = SSM_CHUNK
    seq_chunks = s // chunk
    nc = b * seq_chunks
    tm = min(1024, s)
    tq = min(512, s)
    sub = 8
    slabs = chunk // sub
    view = lambda a: a.reshape(nc, slabs, sub, a.shape[-1])
    x4 = view(x)

    q, k, v, u_t, ga, gb = _in_proj(x4, positions, ln_in_g, ln_in_b, w_in[0], b_gate[0],
                                    q_norm_g[0], w_uq[0], kv_norm_g[0], w_ukv[0])
    attn = _attention(q.reshape(b, s, -1), k.reshape(b, s, -1), v.reshape(b, s, -1), tq)
    y_t = _s5(u_t, seq_chunks, a_re[0], a_im[0], log_dt[0], b_re[0], b_im[0], c_re[0], c_im[0], d_skip[0])
    h1 = _merge(x4, view(attn), y_t, ga, gb, ln_in_g, ln_in_b, w_glu[0], b_glu[0], w_ssm_br[0],
                w_attn_br[0], w_o[0], ln1_g[0], ln1_b[0])
    out = _ffn(h1.reshape(t, d), p[0].reshape(t, -1), w_up[0], w_down[0], ln2_g[0], ln2_b[0], w_ple_gate[0],
               b_ple_gate[0], w_ple[0], ln3_g[0], ln3_b[0], tm)
    return out.reshape(b, s, d)
```

```python
import functools
import math

import jax
import jax.numpy as jnp
from jax import lax
from jax.experimental import pallas as pl
from jax.experimental.pallas import tpu as pltpu

F32 = jnp.float32
BF16 = jnp.bfloat16

MLA_HEADS = 8
QK_NOPE_DIM = 64
QK_ROPE_DIM = 32
V_HEAD_DIM = 64
ROPE_THETA = 10000.0
SSM_GROUP = 16
LN_EPS = 1e-5
RMS_EPS = 1e-6
DEPTH = 1
DEEPNORM_ALPHA = (2.0 * DEPTH) ** 0.25

LANES = 128
MXU_DIM = 256
HEAD_PAD = LANES
BF16_ROWS = 16
V_AUG = V_HEAD_DIM + BF16_ROWS
SSM_SUB = MXU_DIM // SSM_GROUP
SSM_CHUNK = 64
SCAN_STEPS = 8
CHUNK_BLOCK = LANES
VMEM_LIMIT = 56 * 1024 * 1024
S5_GROUPS_PER_STEP = 2
MASKED = -1e30
FF_CHUNK = 1024


def _const_spec(shape):
    nd = len(shape)
    return pl.BlockSpec(shape, lambda *_: (0,) * nd, pipeline_mode=pl.Buffered(1))


def _layer_norm(x, g, b):
    mu = jnp.mean(x, axis=-1, keepdims=True)
    xc = x - mu
    var = jnp.mean(xc * xc, axis=-1, keepdims=True)
    return xc * lax.rsqrt(var + LN_EPS) * g + b


def _rms_norm(x, g):
    return x * lax.rsqrt(jnp.mean(x * x, axis=-1, keepdims=True) + RMS_EPS) * g


def _dot(a, b):
    return jnp.dot(a, b, preferred_element_type=F32)


def _in_proj_kernel(x_ref, pos_ref, lng_ref, lnb_ref, wcq_ref, wckv_ref, wkr_ref,
                    qg_ref, wq_ref, kvg_ref, wk_ref, wv_ref, wu_ref, invf_ref,
                    q_ref, k_ref, v_ref, ut_ref, u_buf, *, scale):
    ncb, sub, d_in = x_ref.shape
    hb = ncb // 2
    rows = hb * sub
    parts = range(2)
    cs = lambda p: slice(p * hb, (p + 1) * hb)
    tile = lambda v: v.reshape(hb, sub, v.shape[-1])
    h = [_layer_norm(x_ref[cs(p)].reshape(rows, d_in), lng_ref[...], lnb_ref[...]).astype(BF16) for p in parts]

    proj = lambda w_ref: [_dot(h[p], w_ref[...]) for p in parts]
    cq, ckv, kr = proj(wcq_ref), proj(wckv_ref), proj(wkr_ref)

    one = jnp.ones((rows, QK_NOPE_DIM), F32)
    pad = HEAD_PAD - QK_NOPE_DIM - QK_ROPE_DIM
    half = QK_ROPE_DIM // 2
    zero = one * 0.0
    cos2, sin_lo, sin_hi, k_rope2 = [], [], [], []
    for p in parts:
        ang_t = invf_ref[...] * pos_ref[:, p * rows:(p + 1) * rows].astype(F32)
        cos_f, sin_f = jnp.cos(ang_t).T, jnp.sin(ang_t).T
        cos = jnp.concatenate([one, cos_f, cos_f, one[:, :pad]], axis=1)
        sin = jnp.concatenate([one * 0.0, sin_f, sin_f, one[:, :pad] * 0.0], axis=1)
        cos2.append(jnp.concatenate([cos, cos], axis=1))
        sin_lo.append(jnp.concatenate([zero, -sin_f, zero[:, :half + pad]], axis=1))
        sin_hi.append(jnp.concatenate([zero, zero[:, :half], sin_f, zero[:, :pad]], axis=1))
        k_rope = kr[p][:, :HEAD_PAD] * cos + kr[p][:, HEAD_PAD:] * sin
        k_rope2.append(jnp.concatenate([k_rope, k_rope], axis=1))

    cqn = [_rms_norm(cq[p], qg_ref[...]).astype(BF16) for p in parts]
    ckvn = [_rms_norm(ckv[p], kvg_ref[...]).astype(BF16) for p in parts]
    width = wk_ref.shape[1]
    for c in range(width // MXU_DIM):
        sl = slice(c * MXU_DIM, (c + 1) * MXU_DIM)
        q = [_dot(cqn[p], wq_ref[:, sl]) for p in parts]
        k = [_dot(ckvn[p], wk_ref[:, sl]) for p in parts]
        for p in parts:
            q_rot = jnp.concatenate(
                [pltpu.roll(q[p][:, hh * HEAD_PAD:(hh + 1) * HEAD_PAD], HEAD_PAD - half, axis=1) * sin_lo[p]
                 + pltpu.roll(q[p][:, hh * HEAD_PAD:(hh + 1) * HEAD_PAD], half, axis=1) * sin_hi[p]
                 for hh in range(MXU_DIM // HEAD_PAD)], axis=1)
            q_ref[cs(p), :, sl] = tile(((q[p] * cos2[p] + q_rot) * scale).astype(BF16))
            k_ref[cs(p), :, sl] = tile((k[p] + k_rope2[p]).astype(BF16))
    v = [_dot(ckvn[p], wv_ref[...]) for p in parts]
    u = [_dot(h[p], wu_ref[...]) for p in parts]
    for p in parts:
        v_ref[cs(p)] = tile(v[p].astype(BF16))
        for c in range(u_buf.shape[0]):
            u_buf[c, p * rows:(p + 1) * rows, :] = u[p][:, c * LANES:(c + 1) * LANES]
    n = ut_ref.shape[1] // sub
    gl = LANES // n
    for j in range(sub):
        for c in range(u_buf.shape[0]):
            u_j = u_buf[c, pl.ds(j, ncb, stride=sub), :].T
            ut_ref[c * gl:(c + 1) * gl, j * n:(j + 1) * n, :] = u_j.astype(BF16).reshape(gl, n, ncb)


def _rope_swap(w):
    half = w.shape[-1] // 2
    return jnp.concatenate([-w[..., half:], w[..., :half]], axis=-1)


def _head_slot(nope, rope):
    k, h, _ = rope.shape
    parts = [nope if nope is not None else jnp.zeros((k, h, QK_NOPE_DIM), rope.dtype), rope]
    used = QK_NOPE_DIM + QK_ROPE_DIM
    parts.append(jnp.zeros((k, h, HEAD_PAD - used), rope.dtype))
    return jnp.concatenate(parts, axis=-1).reshape(k, h * HEAD_PAD)


def _in_proj(x4, positions, ln_g, ln_b, w_in, q_norm_g, w_uq, kv_norm_g, w_ukv):
    nc, slabs, sub, d = x4.shape
    ncb = CHUNK_BLOCK
    pos4 = positions.reshape(nc // ncb, ncb, slabs, sub).transpose(0, 2, 1, 3).reshape(nc // ncb, slabs, 1, ncb * sub)
    q_lora = q_norm_g.shape[0]
    kv_lora = kv_norm_g.shape[0]
    ssm_w = w_in.shape[1] - (q_lora + kv_lora + QK_ROPE_DIM + 2 * d)
    heads = MLA_HEADS

    o = 0
    w_cq = w_in[:, o:o + q_lora]; o += q_lora
    w_ckv = w_in[:, o:o + kv_lora]; o += kv_lora
    w_kr = w_in[:, o:o + QK_ROPE_DIM]; o += QK_ROPE_DIM
    w_u = w_in[:, o:o + ssm_w]; o += ssm_w
    kr_slot = _head_slot(None, w_kr[:, None, :])
    kr_sw_slot = _head_slot(None, _rope_swap(w_kr)[:, None, :])
    w_kr = jnp.concatenate([kr_slot, kr_sw_slot], axis=1)
    w_h = [w.astype(BF16) for w in (w_cq, w_ckv, w_kr)]

    wq = w_uq.reshape(q_lora, heads, QK_NOPE_DIM + QK_ROPE_DIM)
    wq_nope, wq_rope = wq[..., :QK_NOPE_DIM], wq[..., QK_NOPE_DIM:]
    w_q = _head_slot(wq_nope, wq_rope).astype(BF16)
    wkv = w_ukv.reshape(kv_lora, heads, QK_NOPE_DIM + V_HEAD_DIM)
    w_k = _head_slot(wkv[..., :QK_NOPE_DIM], jnp.zeros((kv_lora, heads, QK_ROPE_DIM), F32)).astype(BF16)
    w_v = wkv[..., QK_NOPE_DIM:].reshape(kv_lora, heads * V_HEAD_DIM).astype(BF16)
    w_u = w_u.astype(BF16)

    inv_freq = ROPE_THETA ** (-jnp.arange(0, QK_ROPE_DIM, 2, dtype=F32) / QK_ROPE_DIM)
    invf = inv_freq[:, None]

    hw = heads * HEAD_PAD
    hv = heads * V_HEAD_DIM
    groups = ssm_w // SSM_GROUP
    slab = lambda w: pl.BlockSpec((ncb, None, sub, w), lambda i, c: (c, i, 0, 0))
    scale = (QK_NOPE_DIM + QK_ROPE_DIM) ** -0.5 * math.log2(math.e)
    nat = lambda w: jax.ShapeDtypeStruct((nc, slabs, sub, w), BF16)
    return pl.pallas_call(
        functools.partial(_in_proj_kernel, scale=scale),
        grid=(slabs, nc // ncb),
        in_specs=[slab(d), pl.BlockSpec((None, None, 1, ncb * sub), lambda i, c: (c, i, 0, 0)),
                  _const_spec((1, d)), _const_spec((1, d)), *[_const_spec(w.shape) for w in w_h],
                  _const_spec((1, q_lora)), _const_spec(w_q.shape),
                  _const_spec((1, kv_lora)), _const_spec(w_k.shape), _const_spec(w_v.shape),
                  _const_spec(w_u.shape), _const_spec(invf.shape)],
        out_specs=[slab(hw), slab(hw), slab(hv),
                   pl.BlockSpec((groups, sub * SSM_GROUP, ncb), lambda i, c: (0, i, c))],
        out_shape=[nat(hw), nat(hw), nat(hv),
                   jax.ShapeDtypeStruct((groups, slabs * sub * SSM_GROUP, nc), BF16)],
        scratch_shapes=[pltpu.VMEM((ssm_w // LANES, ncb * sub, LANES), F32)],
        compiler_params=pltpu.CompilerParams(dimension_semantics=("parallel", "parallel"),
                                             vmem_limit_bytes=VMEM_LIMIT),
        name="in_proj",
    )(x4, pos4, ln_g[None], ln_b[None], *w_h, q_norm_g[None], w_q, kv_norm_g[None],
      w_k, w_v, w_u, invf)


def _attn_kernel(idx_ref, q_ref, k_ref, v_ref, o_ref, s_buf, m_buf, vt_buf, qt_buf, *, tq, tk):
    extra = (lax.broadcasted_iota(jnp.int32, (V_AUG - V_HEAD_DIM, tk), 0) == 0).astype(BF16)
    for j in range(vt_buf.shape[0]):
        v_t = v_ref[0, j * tk:(j + 1) * tk, :].T
        for hh in range(2):
            vt_buf[j, hh] = jnp.concatenate([v_t[hh * V_HEAD_DIM:(hh + 1) * V_HEAD_DIM], extra], axis=0)

    key = lax.broadcasted_iota(jnp.int32, (tk, tq), 0)
    qry = lax.broadcasted_iota(jnp.int32, (tk, tq), 1)
    heads = range(2)

    class QueryBlock:
        def __init__(self, i):
            self.side, self.i, self.quads = i % 2, i, idx_ref[i] // 2
            self.rows = slice(i * tq, (i + 1) * tq)

        def setup(self):
            for hh in heads:
                qt_buf[self.side, hh] = q_ref[0, self.rows, hh * HEAD_PAD:(hh + 1) * HEAD_PAD].T

        def scores(self, j, slot):
            start = pl.multiple_of(j * tk, tk)
            for hh in heads:
                s = _dot(k_ref[0, pl.ds(start, tk), hh * HEAD_PAD:(hh + 1) * HEAD_PAD], qt_buf[self.side, hh])
                s_buf[self.side, slot, hh] = s
                m_buf[self.side, slot, hh] = jnp.max(s, axis=0, keepdims=True)

        def update(self, j, slot, state, mask_offset):
            ps, stats = [], []
            for hh in heads:
                m, _ = state[hh]
                s = s_buf[self.side, slot, hh]
                if mask_offset is not None:
                    s = jnp.where(key + mask_offset <= qry, s, MASKED)
                    m_new = jnp.maximum(m, jnp.max(s, axis=0, keepdims=True))
                else:
                    m_new = jnp.maximum(m, m_buf[self.side, slot, hh])
                stats.append((m_new, jnp.exp2(m - m_new)))
                ps.append(jnp.exp2(s - m_new).astype(BF16))
            out = []
            for hh in heads:
                m_new, a = stats[hh]
                out.append((m_new, a * state[hh][1] + _dot(vt_buf[j, hh], ps[hh])))
            return tuple(out)

        def pair(self, p, state):
            blk = 2 * p
            self.scores(blk + 1, 1)
            state = self.update(blk, 0, state, None)
            self.scores(blk + 2, 0)
            return self.update(blk + 1, 1, state, None)

        def unmasked(self):
            i = self.i
            init = tuple((jnp.full((1, tq), MASKED, F32), jnp.zeros((V_AUG, tq), F32)) for _ in heads)
            state = lax.fori_loop(0, self.quads, lambda qi, st: self.pair(2 * qi + 1, self.pair(2 * qi, st)), init)
            return self.pair(i - 1, state) if i % 2 else state

        def finish(self, state):
            o_t = jnp.concatenate([acc[:V_HEAD_DIM] / acc[V_HEAD_DIM:V_HEAD_DIM + 1] for _, acc in state], axis=0)
            o_ref[0, self.rows, :] = o_t.T.astype(BF16)

    blocks = [QueryBlock(i) for i in range(q_ref.shape[1] // tq)]
    blocks[0].setup()
    blocks[0].scores(0, 0)
    for blk, nxt in zip(blocks, blocks[1:] + [None]):
        if nxt is not None:
            nxt.setup()
        state = blk.unmasked()
        diag = 2 * blk.i
        blk.scores(diag + 1, 1)
        state = blk.update(diag, 0, state, 0)
        if nxt is not None:
            nxt.scores(0, 0)
        blk.finish(blk.update(diag + 1, 1, state, tk))


def _attention(q, k, v, tq):
    b, s, _ = q.shape
    pairs = MLA_HEADS // 2
    tk = tq // 2
    nkv = s // tk
    grid_spec = pltpu.PrefetchScalarGridSpec(
        num_scalar_prefetch=1,
        grid=(b, pairs),
        in_specs=[pl.BlockSpec((1, s, 2 * HEAD_PAD), lambda bi, hp, idx: (bi, 0, hp)),
                  pl.BlockSpec((1, s, 2 * HEAD_PAD), lambda bi, hp, idx: (bi, 0, hp)),
                  pl.BlockSpec((1, s, 2 * V_HEAD_DIM), lambda bi, hp, idx: (bi, 0, hp))],
        out_specs=pl.BlockSpec((1, s, 2 * V_HEAD_DIM), lambda bi, hp, idx: (bi, 0, hp)),
        scratch_shapes=[pltpu.VMEM((2, 2, 2, tk, tq), F32), pltpu.VMEM((2, 2, 2, 1, tq), F32),
                        pltpu.VMEM((nkv, 2, V_AUG, tk), BF16),
                        pltpu.VMEM((2, 2, HEAD_PAD, tq), BF16)])
    return pl.pallas_call(
        functools.partial(_attn_kernel, tq=tq, tk=tk),
        grid_spec=grid_spec,
        out_shape=jax.ShapeDtypeStruct((b, s, MLA_HEADS * V_HEAD_DIM), BF16),
        compiler_params=pltpu.CompilerParams(
            dimension_semantics=("parallel", "parallel"), vmem_limit_bytes=VMEM_LIMIT),
        name="attention",
    )(jnp.arange(s // tq, dtype=jnp.int32), q, k, v)


def _gelu_tanh(x):
    k = -2.0 * math.sqrt(2.0 / math.pi) * math.log2(math.e)
    return x / (1.0 + jnp.exp2(x * (k + (k * 0.044715) * (x * x))))


def _s5_kernel(ut_ref, tt_ref, wt_ref, vt_ref, sc_ref, d_ref, y_ref, *, seq_chunks):
    groups = range(ut_ref.shape[0])
    nc = ut_ref.shape[2]
    tiles = range(tt_ref.shape[1])
    tile = lambda t: slice(t * MXU_DIM, (t + 1) * MXU_DIM)
    z = [_dot(wt_ref[g], ut_ref[g]) for g in groups]
    local = []
    for g in groups:
        for t in tiles:
            acc = _dot(tt_ref[g, t], ut_ref[g, tile(0), :])
            for s in range(1, t + 1):
                acc += _dot(tt_ref[g, t - s], ut_ref[g, tile(s), :])
            local.append(acc)

    half = z[0].shape[0] // 2
    pos = lax.broadcasted_iota(jnp.int32, (1, nc), 1) % seq_chunks
    x = z
    for k in range(seq_chunks.bit_length() - 1):
        sh = 1 << k
        nxt = []
        for g in groups:
            a = sc_ref[g, :, k:k + 1]
            a_re, a_im = a[:half], a[half:]
            xs = jnp.where(pos >= sh, pltpu.roll(x[g], sh, axis=1), 0.0)
            xs_re, xs_im = xs[:half], xs[half:]
            nxt.append(x[g] + jnp.concatenate([a_re * xs_re - a_im * xs_im, a_re * xs_im + a_im * xs_re], axis=0))
        x = nxt
    x_in = [jnp.where(pos >= 1, pltpu.roll(x[g], 1, axis=1), 0.0).astype(BF16) for g in groups]

    carried = [_dot(vt_ref[g, tile(t), :], x_in[g]) for g in groups for t in tiles]
    for g in groups:
        d = d_ref[g]
        d = jnp.broadcast_to(d[None], (MXU_DIM // d.shape[0],) + d.shape).reshape(MXU_DIM, 1)
        for t in tiles:
            n = g * len(tiles) + t
            y = local[n] + carried[n] + d * ut_ref[g, tile(t), :].astype(F32)
            y_ref[g, tile(t), :] = _gelu_tanh(y).astype(BF16)


def _s5_param_kernel(are_ref, aim_ref, ldt_ref, btr_ref, bti_ref, cr_ref, ci_ref,
                     tt_ref, wt_ref, vt_ref, sc_ref, *, chunk):
    n, p = cr_ref.shape[1:]
    hi = lax.Precision.HIGHEST
    dt = jnp.exp(ldt_ref[0])
    lam_re = jnp.minimum(are_ref[0], -1e-4)
    lam_im = aim_ref[0]
    twice = lambda x: jnp.concatenate([x, x], axis=1)
    rate, freq = twice(lam_re) * dt, twice(lam_im) * dt
    first = lax.broadcasted_iota(jnp.int32, (1, 2 * p), 1) < p

    def powers(j):
        ang = freq * j
        return jnp.exp(rate * j) * jnp.where(first, jnp.cos(ang), jnp.sin(ang))

    j0 = lax.broadcasted_iota(jnp.int32, (chunk, 1), 0).astype(F32)
    abar = powers(jnp.ones((1, 1), F32))
    k_idx = lax.broadcasted_iota(jnp.int32, (sc_ref.shape[2], 1), 0).astype(F32)
    sc_ref[0] = powers(jnp.exp2(k_idx) * chunk).T
    nr, ni = abar[:, :p] - 1.0, abar[:, p:]
    den = lam_re * lam_re + lam_im * lam_im
    f_re = (nr * lam_re + ni * lam_im) / den
    f_im = (ni * lam_re - nr * lam_im) / den
    bt_re = f_re * btr_ref[0] - f_im * bti_ref[0]
    bt_im = f_re * bti_ref[0] + f_im * btr_ref[0]

    rows = chunk * n

    def per_step(pw):
        x = jnp.broadcast_to(pw[:, None, :], (chunk, n, 2 * p)).reshape(rows, 2 * p)
        return x, pltpu.roll(x, p, axis=1)

    def per_chan(x):
        return jnp.broadcast_to(twice(x)[None], (chunk, n, 2 * p)).reshape(rows, 2 * p)

    x, x_sw = per_step(powers(j0 + 1.0))
    a, b = per_chan(cr_ref[0]) * x, per_chan(ci_ref[0]) * x_sw
    vt_ref[0] = jnp.where(first, a - b, -(a + b)).astype(BF16)
    x, x_sw = per_step(powers(chunk - 1.0 - j0))
    a, b = x * per_chan(bt_re), x_sw * per_chan(bt_im)
    w = jnp.where(first, a - b, a + b)
    wt_ref[0] = w.T.astype(BF16)

    c_cat = jnp.concatenate([cr_ref[0], -ci_ref[0]], axis=1)
    k_rev = lax.dot_general(c_cat, w, (((1,), (1,)), ((), ())), precision=hi, preferred_element_type=F32)
    sub = SSM_SUB
    k_rev = jnp.concatenate([k_rev, jnp.zeros((n, (sub - 1) * n), F32)], axis=1)
    for d in range(chunk // sub):
        for t in range(sub):
            off = (chunk - 1 - sub * d - t) * n
            tt_ref[0, d, t * n:(t + 1) * n, :] = k_rev[:, off:off + sub * n].astype(BF16)


def _s5_weights(a_re, a_im, log_dt, b_re, b_im, c_re, c_im, chunk):
    g, p = a_re.shape
    n = b_re.shape[-1]
    tiles, tw, cw = chunk // SSM_SUB, SSM_SUB * n, chunk * n
    grp = lambda shape: pl.BlockSpec((1,) + shape, lambda i: (i,) + (0,) * len(shape))
    return pl.pallas_call(
        functools.partial(_s5_param_kernel, chunk=chunk),
        grid=(g,),
        in_specs=[grp((1, p)), grp((1, p)), grp((1, 1)), grp((n, p)), grp((n, p)), grp((n, p)), grp((n, p))],
        out_specs=[grp((tiles, tw, tw)), grp((2 * p, cw)), grp((cw, 2 * p)), grp((2 * p, SCAN_STEPS))],
        out_shape=[jax.ShapeDtypeStruct((g, tiles, tw, tw), BF16), jax.ShapeDtypeStruct((g, 2 * p, cw), BF16),
                   jax.ShapeDtypeStruct((g, cw, 2 * p), BF16), jax.ShapeDtypeStruct((g, 2 * p, SCAN_STEPS), F32)],
        compiler_params=pltpu.CompilerParams(dimension_semantics=("parallel",)),
        name="s5_params",
    )(a_re[:, None], a_im[:, None], log_dt[:, None, None], jnp.swapaxes(b_re, 1, 2),
      jnp.swapaxes(b_im, 1, 2), c_re, c_im)


def _s5(u_t, seq_chunks, a_re, a_im, log_dt, b_re, b_im, c_re, c_im, d_skip):
    g, cw, nc = u_t.shape
    n = SSM_GROUP
    chunk = cw // n
    assert seq_chunks & (seq_chunks - 1) == 0 and seq_chunks <= 1 << SCAN_STEPS
    tt, wt, vt, sc = _s5_weights(a_re, a_im, log_dt, b_re, b_im, c_re, c_im, chunk)
    gb = S5_GROUPS_PER_STEP
    grp = lambda shape: pl.BlockSpec((gb,) + shape, lambda i: (i,) + (0,) * len(shape))
    return pl.pallas_call(
        functools.partial(_s5_kernel, seq_chunks=seq_chunks),
        grid=(g // gb,),
        in_specs=[grp((cw, nc)), grp(tt.shape[1:]), grp(wt.shape[1:]), grp(vt.shape[1:]), grp(sc.shape[1:]),
                  grp((n, 1))],
        out_specs=grp((cw, nc)),
        out_shape=jax.ShapeDtypeStruct((g, cw, nc), BF16),
        compiler_params=pltpu.CompilerParams(dimension_semantics=("parallel",),
                                             vmem_limit_bytes=VMEM_LIMIT),
        name="s5",
    )(u_t, tt, wt, vt, sc, d_skip.reshape(g, n, 1))


def _merge_kernel(x_ref, attn_ref, y_ref, lng_ref, lnb_ref, wga_ref, wgb_ref, bg_ref, wglu_ref, bglu_ref,
                  wsb_ref, wab_ref, wo_ref, g1_ref, b1_ref, o_ref, y_buf):
    ncb, sub, d = x_ref.shape
    n = y_ref.shape[1] // sub
    gl = LANES // n
    for j in range(sub):
        for c in range(y_buf.shape[0]):
            y_j = y_ref[c * gl:(c + 1) * gl, j * n:(j + 1) * n, :].reshape(LANES, ncb).astype(F32)
            y_buf[c, pl.ds(j, ncb, stride=sub), :] = y_j.T
    hb = ncb // 2
    parts = range(2)
    flat = lambda ref, p: ref[p * hb:(p + 1) * hb].reshape(hb * sub, ref.shape[-1])
    h = [_layer_norm(flat(x_ref, p), lng_ref[...], lnb_ref[...]) for p in parts]
    h_bf = [h[p].astype(BF16) for p in parts]
    gate_a = [_dot(h_bf[p], wga_ref[...]) for p in parts]
    gate_b = [_dot(h_bf[p], wgb_ref[...]) for p in parts]
    ys = [jnp.concatenate([y_buf[c, p * hb * sub:(p + 1) * hb * sub, :] for c in range(y_buf.shape[0])], axis=1)
          for p in parts]
    gate = [_dot(ys[p].astype(BF16), wglu_ref[...]) for p in parts]
    branch_a = [_dot(flat(attn_ref, p), wab_ref[...]) for p in parts]
    yg = [(ys[p] * jax.nn.sigmoid(gate[p] + bglu_ref[...])).astype(BF16) for p in parts]
    branch_b = [_dot(yg[p], wsb_ref[...]) for p in parts]
    merged = [(jax.nn.sigmoid(gate_a[p] + bg_ref[:, :d]) * branch_a[p]
               + jax.nn.sigmoid(gate_b[p] + bg_ref[:, d:]) * branch_b[p]).astype(BF16) for p in parts]
    mixed = [_dot(merged[p], wo_ref[...]) for p in parts]
    for p in parts:
        r = DEEPNORM_ALPHA * h[p] + mixed[p]
        o_ref[p * hb:(p + 1) * hb] = _layer_norm(r, g1_ref[...], b1_ref[...]).reshape(hb, sub, d)


def _merge(x4, attn4, y_t, ln_g, ln_b, w_ga, w_gb, b_gate, w_glu, b_glu, w_ssm_br, w_attn_br, w_o, ln1_g, ln1_b):
    nc, slabs, sub, d = x4.shape
    groups, cw, _ = y_t.shape
    ncb = CHUNK_BLOCK
    slab = lambda a: pl.BlockSpec((ncb, None, sub, a.shape[-1]), lambda i, c: (c, i, 0, 0))
    vec = lambda v: v[None]
    ws = [w_ga.astype(BF16), w_gb.astype(BF16), vec(b_gate), w_glu.astype(BF16), vec(b_glu), w_ssm_br.astype(BF16), w_attn_br.astype(BF16),
          w_o.astype(BF16), vec(ln1_g), vec(ln1_b)]
    return pl.pallas_call(
        _merge_kernel,
        grid=(slabs, nc // ncb),
        in_specs=[slab(x4), slab(attn4), pl.BlockSpec((groups, cw // slabs, ncb), lambda i, c: (0, i, c)),
                  _const_spec((1, d)), _const_spec((1, d))]
                 + [_const_spec(w.shape) for w in ws],
        out_specs=slab(x4),
        out_shape=jax.ShapeDtypeStruct(x4.shape, F32),
        scratch_shapes=[pltpu.VMEM((groups * SSM_GROUP // LANES, ncb * sub, LANES), F32)],
        compiler_params=pltpu.CompilerParams(dimension_semantics=("parallel", "parallel"),
                                             vmem_limit_bytes=VMEM_LIMIT),
        name="merge",
    )(x4, attn4, y_t, vec(ln_g), vec(ln_b), *ws)


def _ffn_kernel(h_ref, p_ref, wup_ref, wdn_ref, g2_ref, b2_ref, wpg_ref, bpg_ref, wple_ref,
                g3_ref, b3_ref, o_ref, *, ff_chunk):
    half = h_ref.shape[0] // 2
    parts = range(2)
    rs = lambda p: slice(p * half, (p + 1) * half)
    h = [h_ref[rs(p), :] for p in parts]
    hb = [h[p].astype(BF16) for p in parts]
    ff = [jnp.zeros_like(h[p]) for p in parts]
    for c in range(wup_ref.shape[1] // ff_chunk):
        cs = slice(c * ff_chunk, (c + 1) * ff_chunk)
        up = [jnp.maximum(_dot(hb[p], wup_ref[:, cs]), 0.0) for p in parts]
        ff = [ff[p] + _dot((up[p] * up[p]).astype(BF16), wdn_ref[cs, :]) for p in parts]
    ple = [_dot(p_ref[rs(p), :].astype(BF16), wple_ref[...]) for p in parts]
    h2 = [_layer_norm(DEEPNORM_ALPHA * h[p] + ff[p], g2_ref[...], b2_ref[...]) for p in parts]
    gate = [_dot(h2[p].astype(BF16), wpg_ref[...]) for p in parts]
    for p in parts:
        r = DEEPNORM_ALPHA * h2[p] + jax.nn.sigmoid(gate[p] + bpg_ref[...]) * ple[p]
        o_ref[rs(p), :] = _layer_norm(r, g3_ref[...], b3_ref[...])


def _ffn(h1, p2, w_up, w_down, ln2_g, ln2_b, w_pg, b_pg, w_ple, ln3_g, ln3_b, tm):
    t, d = h1.shape
    row = lambda w: pl.BlockSpec((tm, w), lambda i: (i, 0))
    vec = lambda v: v[None]
    ws = [w_up.astype(BF16), w_down.astype(BF16), vec(ln2_g), vec(ln2_b), w_pg.astype(BF16), vec(b_pg),
          w_ple.astype(BF16), vec(ln3_g), vec(ln3_b)]
    return pl.pallas_call(
        functools.partial(_ffn_kernel, ff_chunk=FF_CHUNK),
        grid=(t // tm,),
        in_specs=[row(d), row(p2.shape[1])] + [_const_spec(w.shape) for w in ws],
        out_specs=row(d),
        out_shape=jax.ShapeDtypeStruct((t, d), F32),
        compiler_params=pltpu.CompilerParams(dimension_semantics=("parallel",),
                                             vmem_limit_bytes=VMEM_LIMIT),
        name="ffn",
    )(h1, p2, *ws)


def kernel(x, p, positions, ln_in_g, ln_in_b, w_in, b_gate, q_norm_g, w_uq, kv_norm_g, w_ukv, w_attn_br, a_re, a_im, log_dt, b_re, b_im, c_re, c_im, d_skip, w_glu, b_glu, w_ssm_br, w_o, ln1_g, ln1_b, w_up, w_down, ln2_g, ln2_b, w_ple_gate, b_ple_gate, w_ple, ln3_g, ln3_b):
    assert w_in.shape[0] == DEPTH
    b, s, d = x.shape
    t = b * s
    chunk = SSM_CHUNK
    seq_chunks = s // chunk
    nc = b * seq_chunks
    tm = min(1024, s)
    tq = min(512, s)
    sub = 8
    slabs = chunk // sub
    view = lambda a: a.reshape(nc, slabs, sub, a.shape[-1])
    x4 = view(x)

    q, k, v, u_t = _in_proj(x4, positions, ln_in_g, ln_in_b, w_in[0],
                            q_norm_g[0], w_uq[0], kv_norm_g[0], w_ukv[0])
    attn = _attention(q.reshape(b, s, -1), k.reshape(b, s, -1), v.reshape(b, s, -1), tq)
    y_t = _s5(u_t, seq_chunks, a_re[0], a_im[0], log_dt[0], b_re[0], b_im[0], c_re[0], c_im[0], d_skip[0])
    h1 = _merge(x4, view(attn), y_t, ln_in_g, ln_in_b, w_in[0][:, -2 * d:-d], w_in[0][:, -d:], b_gate[0], w_glu[0], b_glu[0], w_ssm_br[0],
                w_attn_br[0], w_o[0], ln1_g[0], ln1_b[0])
    out = _ffn(h1.reshape(t, d), p[0].reshape(t, -1), w_up[0], w_down[0], ln2_g[0], ln2_b[0], w_ple_gate[0],
               b_ple_gate[0], w_ple[0], ln3_g[0], ln3_b[0], tm)
    return out.reshape(b, s, d)
```

```python
import functools
import math

import jax
import jax.numpy as jnp
from jax import lax
from jax.experimental import pallas as pl
from jax.experimental.pallas import tpu as pltpu

F32 = jnp.float32
BF16 = jnp.bfloat16

MLA_HEADS = 8
QK_NOPE_DIM = 64
QK_ROPE_DIM = 32
V_HEAD_DIM = 64
ROPE_THETA = 10000.0
SSM_GROUP = 16
LN_EPS = 1e-5
RMS_EPS = 1e-6
DEPTH = 1
DEEPNORM_ALPHA = (2.0 * DEPTH) ** 0.25

LANES = 128
MXU_DIM = 256
HEAD_PAD = LANES
BF16_ROWS = 16
V_AUG = V_HEAD_DIM + BF16_ROWS
SSM_SUB = MXU_DIM // SSM_GROUP
SSM_CHUNK = 64
SCAN_STEPS = 8
CHUNK_BLOCK = LANES
VMEM_LIMIT = 56 * 1024 * 1024
SMALL_VMEM_LIMIT = 32 * 1024 * 1024
S5_GROUPS_PER_STEP = 2
MASKED = -1e30
FF_CHUNK = 1024


def _const_spec(shape):
    nd = len(shape)
    return pl.BlockSpec(shape, lambda *_: (0,) * nd, pipeline_mode=pl.Buffered(1))


def _layer_norm(x, g, b):
    mu = jnp.mean(x, axis=-1, keepdims=True)
    xc = x - mu
    var = jnp.mean(xc * xc, axis=-1, keepdims=True)
    return xc * lax.rsqrt(var + LN_EPS) * g + b


def _rms_norm(x, g):
    return x * lax.rsqrt(jnp.mean(x * x, axis=-1, keepdims=True) + RMS_EPS) * g


def _dot(a, b):
    return jnp.dot(a, b, preferred_element_type=F32)


def _in_proj_kernel(x_ref, pos_ref, lng_ref, lnb_ref, wcq_ref, wckv_ref, wkr_ref, wga_ref, wgb_ref, bg_ref,
                    qg_ref, wq_ref, kvg_ref, wk_ref, wv_ref, wu_ref, invf_ref,
                    q_ref, k_ref, v_ref, ut_ref, ga_ref, gb_ref, u_buf, *, scale):
    ncb, sub, d_in = x_ref.shape
    hb = ncb // 2
    rows = hb * sub
    parts = range(2)
    cs = lambda p: slice(p * hb, (p + 1) * hb)
    tile = lambda v: v.reshape(hb, sub, v.shape[-1])
    h = [_layer_norm(x_ref[cs(p)].reshape(rows, d_in), lng_ref[...], lnb_ref[...]).astype(BF16) for p in parts]

    proj = lambda w_ref: [_dot(h[p], w_ref[...]) for p in parts]
    cq, ckv, kr = proj(wcq_ref), proj(wckv_ref), proj(wkr_ref)

    one = jnp.ones((rows, QK_NOPE_DIM), F32)
    pad = HEAD_PAD - QK_NOPE_DIM - QK_ROPE_DIM
    half = QK_ROPE_DIM // 2
    zero = one * 0.0
    cos2, sin_lo, sin_hi, k_rope2 = [], [], [], []
    for p in parts:
        ang_t = invf_ref[...] * pos_ref[:, p * rows:(p + 1) * rows].astype(F32)
        cos_f, sin_f = jnp.cos(ang_t).T, jnp.sin(ang_t).T
        cos = jnp.concatenate([one, cos_f, cos_f, one[:, :pad]], axis=1)
        sin = jnp.concatenate([one * 0.0, sin_f, sin_f, one[:, :pad] * 0.0], axis=1)
        cos2.append(jnp.concatenate([cos, cos], axis=1))
        sin_lo.append(jnp.concatenate([zero, -sin_f, zero[:, :half + pad]], axis=1))
        sin_hi.append(jnp.concatenate([zero, zero[:, :half], sin_f, zero[:, :pad]], axis=1))
        k_rope = kr[p][:, :HEAD_PAD] * cos + kr[p][:, HEAD_PAD:] * sin
        k_rope2.append(jnp.concatenate([k_rope, k_rope], axis=1))

    cqn = [_rms_norm(cq[p], qg_ref[...]).astype(BF16) for p in parts]
    ckvn = [_rms_norm(ckv[p], kvg_ref[...]).astype(BF16) for p in parts]
    width = wk_ref.shape[1]
    for c in range(width // MXU_DIM):
        sl = slice(c * MXU_DIM, (c + 1) * MXU_DIM)
        q = [_dot(cqn[p], wq_ref[:, sl]) for p in parts]
        k = [_dot(ckvn[p], wk_ref[:, sl]) for p in parts]
        for p in parts:
            q_rot = jnp.concatenate(
                [pltpu.roll(q[p][:, hh * HEAD_PAD:(hh + 1) * HEAD_PAD], HEAD_PAD - half, axis=1) * sin_lo[p]
                 + pltpu.roll(q[p][:, hh * HEAD_PAD:(hh + 1) * HEAD_PAD], half, axis=1) * sin_hi[p]
                 for hh in range(MXU_DIM // HEAD_PAD)], axis=1)
            q_ref[cs(p), :, sl] = tile(((q[p] * cos2[p] + q_rot) * scale).astype(BF16))
            k_ref[cs(p), :, sl] = tile((k[p] + k_rope2[p]).astype(BF16))
    v = [_dot(ckvn[p], wv_ref[...]) for p in parts]
    u = [_dot(h[p], wu_ref[...]) for p in parts]
    ga, gb = proj(wga_ref), proj(wgb_ref)
    d = ga_ref.shape[-1]
    for p in parts:
        v_ref[cs(p)] = tile(v[p].astype(BF16))
        for c in range(u_buf.shape[0]):
            u_buf[c, p * rows:(p + 1) * rows, :] = u[p][:, c * LANES:(c + 1) * LANES]
        ga_ref[cs(p)] = tile(jax.nn.sigmoid(ga[p] + bg_ref[:, :d]).astype(BF16))
        gb_ref[cs(p)] = tile(jax.nn.sigmoid(gb[p] + bg_ref[:, d:]).astype(BF16))
    n = ut_ref.shape[1] // sub
    gl = LANES // n
    for j in range(sub):
        for c in range(u_buf.shape[0]):
            u_j = u_buf[c, pl.ds(j, ncb, stride=sub), :].T
            ut_ref[c * gl:(c + 1) * gl, j * n:(j + 1) * n, :] = u_j.astype(BF16).reshape(gl, n, ncb)


def _rope_swap(w):
    half = w.shape[-1] // 2
    return jnp.concatenate([-w[..., half:], w[..., :half]], axis=-1)


def _head_slot(nope, rope):
    k, h, _ = rope.shape
    parts = [nope if nope is not None else jnp.zeros((k, h, QK_NOPE_DIM), rope.dtype), rope]
    used = QK_NOPE_DIM + QK_ROPE_DIM
    parts.append(jnp.zeros((k, h, HEAD_PAD - used), rope.dtype))
    return jnp.concatenate(parts, axis=-1).reshape(k, h * HEAD_PAD)


def _in_proj(x4, positions, ln_g, ln_b, w_in, b_gate, q_norm_g, w_uq, kv_norm_g, w_ukv):
    nc, slabs, sub, d = x4.shape
    ncb = CHUNK_BLOCK
    pos4 = positions.reshape(nc // ncb, ncb, slabs, sub).transpose(0, 2, 1, 3).reshape(nc // ncb, slabs, 1, ncb * sub)
    q_lora = q_norm_g.shape[0]
    kv_lora = kv_norm_g.shape[0]
    ssm_w = w_in.shape[1] - (q_lora + kv_lora + QK_ROPE_DIM + 2 * d)
    heads = MLA_HEADS

    o = 0
    w_cq = w_in[:, o:o + q_lora]; o += q_lora
    w_ckv = w_in[:, o:o + kv_lora]; o += kv_lora
    w_kr = w_in[:, o:o + QK_ROPE_DIM]; o += QK_ROPE_DIM
    w_u = w_in[:, o:o + ssm_w]; o += ssm_w
    w_ga = w_in[:, o:o + d]; o += d
    w_gb = w_in[:, o:o + d]
    kr_slot = _head_slot(None, w_kr[:, None, :])
    kr_sw_slot = _head_slot(None, _rope_swap(w_kr)[:, None, :])
    w_kr = jnp.concatenate([kr_slot, kr_sw_slot], axis=1)
    w_h = [w.astype(BF16) for w in (w_cq, w_ckv, w_kr, w_ga, w_gb)]

    wq = w_uq.reshape(q_lora, heads, QK_NOPE_DIM + QK_ROPE_DIM)
    wq_nope, wq_rope = wq[..., :QK_NOPE_DIM], wq[..., QK_NOPE_DIM:]
    w_q = _head_slot(wq_nope, wq_rope).astype(BF16)
    wkv = w_ukv.reshape(kv_lora, heads, QK_NOPE_DIM + V_HEAD_DIM)
    w_k = _head_slot(wkv[..., :QK_NOPE_DIM], jnp.zeros((kv_lora, heads, QK_ROPE_DIM), F32)).astype(BF16)
    w_v = wkv[..., QK_NOPE_DIM:].reshape(kv_lora, heads * V_HEAD_DIM).astype(BF16)
    w_u = w_u.astype(BF16)

    inv_freq = ROPE_THETA ** (-jnp.arange(0, QK_ROPE_DIM, 2, dtype=F32) / QK_ROPE_DIM)
    invf = inv_freq[:, None]

    hw = heads * HEAD_PAD
    hv = heads * V_HEAD_DIM
    groups = ssm_w // SSM_GROUP
    slab = lambda w: pl.BlockSpec((ncb, None, sub, w), lambda i, c: (c, i, 0, 0))
    scale = (QK_NOPE_DIM + QK_ROPE_DIM) ** -0.5 * math.log2(math.e)
    nat = lambda w: jax.ShapeDtypeStruct((nc, slabs, sub, w), BF16)
    return pl.pallas_call(
        functools.partial(_in_proj_kernel, scale=scale),
        grid=(slabs, nc // ncb),
        in_specs=[slab(d), pl.BlockSpec((None, None, 1, ncb * sub), lambda i, c: (c, i, 0, 0)),
                  _const_spec((1, d)), _const_spec((1, d)), *[_const_spec(w.shape) for w in w_h],
                  _const_spec((1, 2 * d)), _const_spec((1, q_lora)), _const_spec(w_q.shape),
                  _const_spec((1, kv_lora)), _const_spec(w_k.shape), _const_spec(w_v.shape),
                  _const_spec(w_u.shape), _const_spec(invf.shape)],
        out_specs=[slab(hw), slab(hw), slab(hv),
                   pl.BlockSpec((groups, sub * SSM_GROUP, ncb), lambda i, c: (0, i, c)), slab(d), slab(d)],
        out_shape=[nat(hw), nat(hw), nat(hv),
                   jax.ShapeDtypeStruct((groups, slabs * sub * SSM_GROUP, nc), BF16), nat(d), nat(d)],
        scratch_shapes=[pltpu.VMEM((ssm_w // LANES, ncb * sub, LANES), F32)],
        compiler_params=pltpu.CompilerParams(dimension_semantics=("parallel", "parallel"),
                                             vmem_limit_bytes=VMEM_LIMIT),
        name="in_proj",
    )(x4, pos4, ln_g[None], ln_b[None], *w_h, b_gate[None], q_norm_g[None], w_q, kv_norm_g[None],
      w_k, w_v, w_u, invf)


def _attn_kernel(idx_ref, q_ref, k_ref, v_ref, o_ref, s_buf, m_buf, vt_buf, qt_buf, *, tq, tk):
    extra = (lax.broadcasted_iota(jnp.int32, (V_AUG - V_HEAD_DIM, tk), 0) == 0).astype(BF16)
    for j in range(vt_buf.shape[0]):
        v_t = v_ref[0, j * tk:(j + 1) * tk, :].T
        for hh in range(2):
            vt_buf[j, hh] = jnp.concatenate([v_t[hh * V_HEAD_DIM:(hh + 1) * V_HEAD_DIM], extra], axis=0)

    key = lax.broadcasted_iota(jnp.int32, (tk, tq), 0)
    qry = lax.broadcasted_iota(jnp.int32, (tk, tq), 1)
    heads = range(2)

    class QueryBlock:
        def __init__(self, i):
            self.side, self.i, self.quads = i % 2, i, idx_ref[i] // 2
            self.rows = slice(i * tq, (i + 1) * tq)

        def setup(self):
            for hh in heads:
                qt_buf[self.side, hh] = q_ref[0, self.rows, hh * HEAD_PAD:(hh + 1) * HEAD_PAD].T

        def scores(self, j, slot):
            start = pl.multiple_of(j * tk, tk)
            for hh in heads:
                s = _dot(k_ref[0, pl.ds(start, tk), hh * HEAD_PAD:(hh + 1) * HEAD_PAD], qt_buf[self.side, hh])
                s_buf[self.side, slot, hh] = s
                m_buf[self.side, slot, hh] = jnp.max(s, axis=0, keepdims=True)

        def update(self, j, slot, state, mask_offset):
            ps, stats = [], []
            for hh in heads:
                m, _ = state[hh]
                s = s_buf[self.side, slot, hh]
                if mask_offset is not None:
                    s = jnp.where(key + mask_offset <= qry, s, MASKED)
                    m_new = jnp.maximum(m, jnp.max(s, axis=0, keepdims=True))
                else:
                    m_new = jnp.maximum(m, m_buf[self.side, slot, hh])
                stats.append((m_new, jnp.exp2(m - m_new)))
                ps.append(jnp.exp2(s - m_new).astype(BF16))
            out = []
            for hh in heads:
                m_new, a = stats[hh]
                out.append((m_new, a * state[hh][1] + _dot(vt_buf[j, hh], ps[hh])))
            return tuple(out)

        def pair(self, p, state):
            blk = 2 * p
            self.scores(blk + 1, 1)
            state = self.update(blk, 0, state, None)
            self.scores(blk + 2, 0)
            return self.update(blk + 1, 1, state, None)

        def unmasked(self):
            i = self.i
            init = tuple((jnp.full((1, tq), MASKED, F32), jnp.zeros((V_AUG, tq), F32)) for _ in heads)
            state = lax.fori_loop(0, self.quads, lambda qi, st: self.pair(2 * qi + 1, self.pair(2 * qi, st)), init)
            return self.pair(i - 1, state) if i % 2 else state

        def finish(self, state):
            o_t = jnp.concatenate([acc[:V_HEAD_DIM] / acc[V_HEAD_DIM:V_HEAD_DIM + 1] for _, acc in state], axis=0)
            o_ref[0, self.rows, :] = o_t.T.astype(BF16)

    blocks = [QueryBlock(i) for i in range(q_ref.shape[1] // tq)]
    blocks[0].setup()
    blocks[0].scores(0, 0)
    for blk, nxt in zip(blocks, blocks[1:] + [None]):
        if nxt is not None:
            nxt.setup()
        state = blk.unmasked()
        diag = 2 * blk.i
        blk.scores(diag + 1, 1)
        state = blk.update(diag, 0, state, 0)
        if nxt is not None:
            nxt.scores(0, 0)
        blk.finish(blk.update(diag + 1, 1, state, tk))


def _attention(q, k, v, tq):
    b, s, _ = q.shape
    pairs = MLA_HEADS // 2
    tk = tq // 2
    nkv = s // tk
    grid_spec = pltpu.PrefetchScalarGridSpec(
        num_scalar_prefetch=1,
        grid=(b, pairs),
        in_specs=[pl.BlockSpec((1, s, 2 * HEAD_PAD), lambda bi, hp, idx: (bi, 0, hp)),
                  pl.BlockSpec((1, s, 2 * HEAD_PAD), lambda bi, hp, idx: (bi, 0, hp)),
                  pl.BlockSpec((1, s, 2 * V_HEAD_DIM), lambda bi, hp, idx: (bi, 0, hp))],
        out_specs=pl.BlockSpec((1, s, 2 * V_HEAD_DIM), lambda bi, hp, idx: (bi, 0, hp)),
        scratch_shapes=[pltpu.VMEM((2, 2, 2, tk, tq), F32), pltpu.VMEM((2, 2, 2, 1, tq), F32),
                        pltpu.VMEM((nkv, 2, V_AUG, tk), BF16),
                        pltpu.VMEM((2, 2, HEAD_PAD, tq), BF16)])
    return pl.pallas_call(
        functools.partial(_attn_kernel, tq=tq, tk=tk),
        grid_spec=grid_spec,
        out_shape=jax.ShapeDtypeStruct((b, s, MLA_HEADS * V_HEAD_DIM), BF16),
        compiler_params=pltpu.CompilerParams(
            dimension_semantics=("parallel", "parallel"), vmem_limit_bytes=SMALL_VMEM_LIMIT),
        name="attention",
    )(jnp.arange(s // tq, dtype=jnp.int32), q, k, v)


def _gelu_tanh(x):
    k = -2.0 * math.sqrt(2.0 / math.pi) * math.log2(math.e)
    return x / (1.0 + jnp.exp2(x * (k + (k * 0.044715) * (x * x))))


def _s5_kernel(ut_ref, tt_ref, wt_ref, vt_ref, sc_ref, d_ref, y_ref, *, seq_chunks):
    groups = range(ut_ref.shape[0])
    nc = ut_ref.shape[2]
    tiles = range(tt_ref.shape[1])
    tile = lambda t: slice(t * MXU_DIM, (t + 1) * MXU_DIM)
    z = [_dot(wt_ref[g], ut_ref[g]) for g in groups]
    local = []
    for g in groups:
        for t in tiles:
            acc = _dot(tt_ref[g, t], ut_ref[g, tile(0), :])
            for s in range(1, t + 1):
                acc += _dot(tt_ref[g, t - s], ut_ref[g, tile(s), :])
            local.append(acc)

    half = z[0].shape[0] // 2
    pos = lax.broadcasted_iota(jnp.int32, (1, nc), 1) % seq_chunks
    x = z
    for k in range(seq_chunks.bit_length() - 1):
        sh = 1 << k
        nxt = []
        for g in groups:
            a = sc_ref[g, :, k:k + 1]
            a_re, a_im = a[:half], a[half:]
            xs = jnp.where(pos >= sh, pltpu.roll(x[g], sh, axis=1), 0.0)
            xs_re, xs_im = xs[:half], xs[half:]
            nxt.append(x[g] + jnp.concatenate([a_re * xs_re - a_im * xs_im, a_re * xs_im + a_im * xs_re], axis=0))
        x = nxt
    x_in = [jnp.where(pos >= 1, pltpu.roll(x[g], 1, axis=1), 0.0).astype(BF16) for g in groups]

    carried = [_dot(vt_ref[g, tile(t), :], x_in[g]) for g in groups for t in tiles]
    for g in groups:
        d = d_ref[g]
        d = jnp.broadcast_to(d[None], (MXU_DIM // d.shape[0],) + d.shape).reshape(MXU_DIM, 1)
        for t in tiles:
            n = g * len(tiles) + t
            y = local[n] + carried[n] + d * ut_ref[g, tile(t), :].astype(F32)
            y_ref[g, tile(t), :] = _gelu_tanh(y).astype(BF16)


def _s5_param_kernel(are_ref, aim_ref, ldt_ref, btr_ref, bti_ref, cr_ref, ci_ref,
                     tt_ref, wt_ref, vt_ref, sc_ref, *, chunk):
    n, p = cr_ref.shape[1:]
    hi = lax.Precision.HIGHEST
    dt = jnp.exp(ldt_ref[0])
    lam_re = jnp.minimum(are_ref[0], -1e-4)
    lam_im = aim_ref[0]
    twice = lambda x: jnp.concatenate([x, x], axis=1)
    rate, freq = twice(lam_re) * dt, twice(lam_im) * dt
    first = lax.broadcasted_iota(jnp.int32, (1, 2 * p), 1) < p

    def powers(j):
        ang = freq * j
        return jnp.exp(rate * j) * jnp.where(first, jnp.cos(ang), jnp.sin(ang))

    j0 = lax.broadcasted_iota(jnp.int32, (chunk, 1), 0).astype(F32)
    abar = powers(jnp.ones((1, 1), F32))
    k_idx = lax.broadcasted_iota(jnp.int32, (sc_ref.shape[2], 1), 0).astype(F32)
    sc_ref[0] = powers(jnp.exp2(k_idx) * chunk).T
    nr, ni = abar[:, :p] - 1.0, abar[:, p:]
    den = lam_re * lam_re + lam_im * lam_im
    f_re = (nr * lam_re + ni * lam_im) / den
    f_im = (ni * lam_re - nr * lam_im) / den
    bt_re = f_re * btr_ref[0] - f_im * bti_ref[0]
    bt_im = f_re * bti_ref[0] + f_im * btr_ref[0]

    rows = chunk * n

    def per_step(pw):
        x = jnp.broadcast_to(pw[:, None, :], (chunk, n, 2 * p)).reshape(rows, 2 * p)
        return x, pltpu.roll(x, p, axis=1)

    def per_chan(x):
        return jnp.broadcast_to(twice(x)[None], (chunk, n, 2 * p)).reshape(rows, 2 * p)

    x, x_sw = per_step(powers(j0 + 1.0))
    a, b = per_chan(cr_ref[0]) * x, per_chan(ci_ref[0]) * x_sw
    vt_ref[0] = jnp.where(first, a - b, -(a + b)).astype(BF16)
    x, x_sw = per_step(powers(chunk - 1.0 - j0))
    a, b = x * per_chan(bt_re), x_sw * per_chan(bt_im)
    w = jnp.where(first, a - b, a + b)
    wt_ref[0] = w.T.astype(BF16)

    c_cat = jnp.concatenate([cr_ref[0], -ci_ref[0]], axis=1)
    k_rev = lax.dot_general(c_cat, w, (((1,), (1,)), ((), ())), precision=hi, preferred_element_type=F32)
    sub = SSM_SUB
    k_rev = jnp.concatenate([k_rev, jnp.zeros((n, (sub - 1) * n), F32)], axis=1)
    for d in range(chunk // sub):
        for t in range(sub):
            off = (chunk - 1 - sub * d - t) * n
            tt_ref[0, d, t * n:(t + 1) * n, :] = k_rev[:, off:off + sub * n].astype(BF16)


def _s5_weights(a_re, a_im, log_dt, b_re, b_im, c_re, c_im, chunk):
    g, p = a_re.shape
    n = b_re.shape[-1]
    tiles, tw, cw = chunk // SSM_SUB, SSM_SUB * n, chunk * n
    grp = lambda shape: pl.BlockSpec((1,) + shape, lambda i: (i,) + (0,) * len(shape))
    return pl.pallas_call(
        functools.partial(_s5_param_kernel, chunk=chunk),
        grid=(g,),
        in_specs=[grp((1, p)), grp((1, p)), grp((1, 1)), grp((n, p)), grp((n, p)), grp((n, p)), grp((n, p))],
        out_specs=[grp((tiles, tw, tw)), grp((2 * p, cw)), grp((cw, 2 * p)), grp((2 * p, SCAN_STEPS))],
        out_shape=[jax.ShapeDtypeStruct((g, tiles, tw, tw), BF16), jax.ShapeDtypeStruct((g, 2 * p, cw), BF16),
                   jax.ShapeDtypeStruct((g, cw, 2 * p), BF16), jax.ShapeDtypeStruct((g, 2 * p, SCAN_STEPS), F32)],
        compiler_params=pltpu.CompilerParams(dimension_semantics=("parallel",)),
        name="s5_params",
    )(a_re[:, None], a_im[:, None], log_dt[:, None, None], jnp.swapaxes(b_re, 1, 2),
      jnp.swapaxes(b_im, 1, 2), c_re, c_im)


def _s5(u_t, seq_chunks, a_re, a_im, log_dt, b_re, b_im, c_re, c_im, d_skip):
    g, cw, nc = u_t.shape
    n = SSM_GROUP
    chunk = cw // n
    assert seq_chunks & (seq_chunks - 1) == 0 and seq_chunks <= 1 << SCAN_STEPS
    tt, wt, vt, sc = _s5_weights(a_re, a_im, log_dt, b_re, b_im, c_re, c_im, chunk)
    gb = S5_GROUPS_PER_STEP
    grp = lambda shape: pl.BlockSpec((gb,) + shape, lambda i: (i,) + (0,) * len(shape))
    return pl.pallas_call(
        functools.partial(_s5_kernel, seq_chunks=seq_chunks),
        grid=(g // gb,),
        in_specs=[grp((cw, nc)), grp(tt.shape[1:]), grp(wt.shape[1:]), grp(vt.shape[1:]), grp(sc.shape[1:]),
                  grp((n, 1))],
        out_specs=grp((cw, nc)),
        out_shape=jax.ShapeDtypeStruct((g, cw, nc), BF16),
        compiler_params=pltpu.CompilerParams(dimension_semantics=("parallel",),
                                             vmem_limit_bytes=SMALL_VMEM_LIMIT),
        name="s5",
    )(u_t, tt, wt, vt, sc, d_skip.reshape(g, n, 1))


def _merge_kernel(x_ref, attn_ref, y_ref, ga_ref, gb_ref, lng_ref, lnb_ref, wglu_ref, bglu_ref,
                  wsb_ref, wab_ref, wo_ref, g1_ref, b1_ref, o_ref, y_buf):
    ncb, sub, d = x_ref.shape
    n = y_ref.shape[1] // sub
    gl = LANES // n
    for j in range(sub):
        for c in range(y_buf.shape[0]):
            y_j = y_ref[c * gl:(c + 1) * gl, j * n:(j + 1) * n, :].reshape(LANES, ncb).astype(F32)
            y_buf[c, pl.ds(j, ncb, stride=sub), :] = y_j.T
    hb = ncb // 2
    parts = range(2)
    flat = lambda ref, p: ref[p * hb:(p + 1) * hb].reshape(hb * sub, ref.shape[-1])
    ys = [jnp.concatenate([y_buf[c, p * hb * sub:(p + 1) * hb * sub, :] for c in range(y_buf.shape[0])], axis=1)
          for p in parts]
    gate = [_dot(ys[p].astype(BF16), wglu_ref[...]) for p in parts]
    branch_a = [_dot(flat(attn_ref, p), wab_ref[...]) for p in parts]
    yg = [(ys[p] * jax.nn.sigmoid(gate[p] + bglu_ref[...])).astype(BF16) for p in parts]
    branch_b = [_dot(yg[p], wsb_ref[...]) for p in parts]
    merged = [(flat(ga_ref, p).astype(F32) * branch_a[p] + flat(gb_ref, p).astype(F32) * branch_b[p]).astype(BF16)
              for p in parts]
    mixed = [_dot(merged[p], wo_ref[...]) for p in parts]
    for p in parts:
        h = _layer_norm(flat(x_ref, p), lng_ref[...], lnb_ref[...])
        r = DEEPNORM_ALPHA * h + mixed[p]
        o_ref[p * hb:(p + 1) * hb] = _layer_norm(r, g1_ref[...], b1_ref[...]).reshape(hb, sub, d)


def _merge(x4, attn4, y_t, ga4, gb4, ln_g, ln_b, w_glu, b_glu, w_ssm_br, w_attn_br, w_o, ln1_g, ln1_b):
    nc, slabs, sub, d = x4.shape
    groups, cw, _ = y_t.shape
    ncb = CHUNK_BLOCK
    slab = lambda a: pl.BlockSpec((ncb, None, sub, a.shape[-1]), lambda i, c: (c, i, 0, 0))
    vec = lambda v: v[None]
    ws = [w_glu.astype(BF16), vec(b_glu), w_ssm_br.astype(BF16), w_attn_br.astype(BF16),
          w_o.astype(BF16), vec(ln1_g), vec(ln1_b)]
    return pl.pallas_call(
        _merge_kernel,
        grid=(slabs, nc // ncb),
        in_specs=[slab(x4), slab(attn4), pl.BlockSpec((groups, cw // slabs, ncb), lambda i, c: (0, i, c)),
                  slab(ga4), slab(gb4), _const_spec((1, d)), _const_spec((1, d))]
                 + [_const_spec(w.shape) for w in ws],
        out_specs=slab(x4),
        out_shape=jax.ShapeDtypeStruct(x4.shape, F32),
        scratch_shapes=[pltpu.VMEM((groups * SSM_GROUP // LANES, ncb * sub, LANES), F32)],
        compiler_params=pltpu.CompilerParams(dimension_semantics=("parallel", "parallel"),
                                             vmem_limit_bytes=VMEM_LIMIT),
        name="merge",
    )(x4, attn4, y_t, ga4, gb4, vec(ln_g), vec(ln_b), *ws)


def _ffn_kernel(h_ref, p_ref, wup_ref, wdn_ref, g2_ref, b2_ref, wpg_ref, bpg_ref, wple_ref,
                g3_ref, b3_ref, o_ref, *, ff_chunk):
    half = h_ref.shape[0] // 2
    parts = range(2)
    rs = lambda p: slice(p * half, (p + 1) * half)
    h = [h_ref[rs(p), :] for p in parts]
    hb = [h[p].astype(BF16) for p in parts]
    ff = [jnp.zeros_like(h[p]) for p in parts]
    for c in range(wup_ref.shape[1] // ff_chunk):
        cs = slice(c * ff_chunk, (c + 1) * ff_chunk)
        up = [jnp.maximum(_dot(hb[p], wup_ref[:, cs]), 0.0) for p in parts]
        ff = [ff[p] + _dot((up[p] * up[p]).astype(BF16), wdn_ref[cs, :]) for p in parts]
    ple = [_dot(p_ref[rs(p), :].astype(BF16), wple_ref[...]) for p in parts]
    h2 = [_layer_norm(DEEPNORM_ALPHA * h[p] + ff[p], g2_ref[...], b2_ref[...]) for p in parts]
    gate = [_dot(h2[p].astype(BF16), wpg_ref[...]) for p in parts]
    for p in parts:
        r = DEEPNORM_ALPHA * h2[p] + jax.nn.sigmoid(gate[p] + bpg_ref[...]) * ple[p]
        o_ref[rs(p), :] = _layer_norm(r, g3_ref[...], b3_ref[...])


def _ffn(h1, p2, w_up, w_down, ln2_g, ln2_b, w_pg, b_pg, w_ple, ln3_g, ln3_b, tm):
    t, d = h1.shape
    row = lambda w: pl.BlockSpec((tm, w), lambda i: (i, 0))
    vec = lambda v: v[None]
    ws = [w_up.astype(BF16), w_down.astype(BF16), vec(ln2_g), vec(ln2_b), w_pg.astype(BF16), vec(b_pg),
          w_ple.astype(BF16), vec(ln3_g), vec(ln3_b)]
    return pl.pallas_call(
        functools.partial(_ffn_kernel, ff_chunk=FF_CHUNK),
        grid=(t // tm,),
        in_specs=[row(d), row(p2.shape[1])] + [_const_spec(w.shape) for w in ws],
        out_specs=row(d),
        out_shape=jax.ShapeDtypeStruct((t, d), F32),
        compiler_params=pltpu.CompilerParams(dimension_semantics=("parallel",),
                                             vmem_limit_bytes=VMEM_LIMIT),
        name="ffn",
    )(h1, p2, *ws)


def kernel(x, p, positions, ln_in_g, ln_in_b, w_in, b_gate, q_norm_g, w_uq, kv_norm_g, w_ukv, w_attn_br, a_re, a_im, log_dt, b_re, b_im, c_re, c_im, d_skip, w_glu, b_glu, w_ssm_br, w_o, ln1_g, ln1_b, w_up, w_down, ln2_g, ln2_b, w_ple_gate, b_ple_gate, w_ple, ln3_g, ln3_b):
    assert w_in.shape[0] == DEPTH
    b, s, d = x.shape
    t = b * s
    chunk = SSM_CHUNK
    seq_chunks = s // chunk
    nc = b * seq_chunks
    tm = min(1024, s)
    tq = min(512, s)
    sub = 8
    slabs = chunk // sub
    view = lambda a: a.reshape(nc, slabs, sub, a.shape[-1])
    x4 = view(x)

    q, k, v, u_t, ga, gb = _in_proj(x4, positions, ln_in_g, ln_in_b, w_in[0], b_gate[0],
                                    q_norm_g[0], w_uq[0], kv_norm_g[0], w_ukv[0])
    attn = _attention(q.reshape(b, s, -1), k.reshape(b, s, -1), v.reshape(b, s, -1), tq)
    y_t = _s5(u_t, seq_chunks, a_re[0], a_im[0], log_dt[0], b_re[0], b_im[0], c_re[0], c_im[0], d_skip[0])
    h1 = _merge(x4, view(attn), y_t, ga, gb, ln_in_g, ln_in_b, w_glu[0], b_glu[0], w_ssm_br[0],
                w_attn_br[0], w_o[0], ln1_g[0], ln1_b[0])
    out = _ffn(h1.reshape(t, d), p[0].reshape(t, -1), w_up[0], w_down[0], ln2_g[0], ln2_b[0], w_ple_gate[0],
               b_ple_gate[0], w_ple[0], ln3_g[0], ln3_b[0], tm)
    return out.reshape(b, s, d)
```
